```python
import jax, jax.numpy as jnp
from jax import lax
import numpy as np

D_MODEL = 2048
BATCH = 16
SEQ = 256
DEPTH = 2
DEC_BATCH = 8
DEC_SEQ = 2048
PAST_LEN = 512

GRID_W = 64
D_MIX = D_MODEL
FN_HEADS = 4
FN_DH = 128
FN_WIDTH = FN_HEADS * FN_DH
HG_HEADS = 4
HG_DK = 128
HG_DV = 128
HG_QK = HG_HEADS * HG_DK
HG_WIDTH = HG_HEADS * HG_DV
MLA_HEADS = 8
MLA_NOPE = 128
MLA_ROPE = 64
MLA_V = 128
Q_LORA = 768
KV_LORA = 512
MLA_WIDTH = MLA_HEADS * MLA_V
D_FF = 4 * D_MODEL
CHUNK = 64
Q_BLOCK = 128
ROPE_BASE = 10000.0
EPS = 1e-6
F_FLOOR = 1e-30
IN_SIZES = (FN_WIDTH, HG_QK, HG_WIDTH, HG_QK, HG_QK, HG_WIDTH, Q_LORA, KV_LORA, MLA_ROPE)
IN_COLS = sum(IN_SIZES)
f32 = jnp.float32

kernel_name = "hybrid_fourier_hgrn2_mla_dit_step"


def rmsnorm(x, g):
    xf = x.astype(f32)
    y = xf * lax.rsqrt(jnp.mean(xf * xf, axis=-1, keepdims=True) + EPS) * g.astype(f32)
    return y.astype(x.dtype)


def axial_rope_tables(rows):
    t = jnp.arange(rows * GRID_W)
    r = (t // GRID_W).astype(f32)
    col = (t % GRID_W).astype(f32)
    nf = MLA_ROPE // 4
    inv = ROPE_BASE ** (-jnp.arange(nf, dtype=f32) / nf)
    ar = r[:, None] * inv
    ac = col[:, None] * inv
    ang = jnp.concatenate([ar, ar, ac, ac], axis=-1)
    return jnp.cos(ang), jnp.sin(ang)


def apply_axial_rope(x, cos, sin):
    xf = x.astype(f32)
    xs = xf.reshape(x.shape[:-1] + (2, 2, MLA_ROPE // 4))
    rot = jnp.concatenate([-xs[..., 1:, :], xs[..., :1, :]], axis=-2).reshape(x.shape)
    return (xf * cos + rot * sin).astype(x.dtype)


def forget_gate(pre, lb):
    lbf = lb.astype(f32)
    f = lbf + (1.0 - lbf) * jax.nn.sigmoid(pre.astype(f32))
    return jnp.log(jnp.maximum(f, F_FLOOR)), 1.0 - f


def hgrn_chunked(q, k, v, log_f, s0):
    b_sz, n, h, _ = q.shape
    n_chunks = n // CHUNK

    def to_chunks(t):
        return t.reshape(b_sz, n_chunks, CHUNK, h, t.shape[-1]).transpose(1, 0, 3, 2, 4)

    lower = jnp.tril(jnp.ones((CHUNK, CHUNK), dtype=bool))[:, :, None]

    def step(state, blk):
        qc, kc, vc, fc = blk
        cum = jnp.cumsum(fc, axis=2)
        diff = cum[:, :, :, None, :] - cum[:, :, None, :, :]
        rel = jnp.where(lower, jnp.exp(jnp.minimum(diff, 0.0)), 0.0)
        scores = jnp.einsum("bhtk,bhtsk,bhsk->bhts", qc, rel, kc)
        out = (jnp.einsum("bhts,bhsv->bhtv", scores, vc)
               + jnp.einsum("bhtk,bhkv->bhtv", qc * jnp.exp(cum), state))
        last = cum[:, :, -1:, :]
        state = (jnp.exp(last[:, :, 0, :])[..., None] * state
                 + jnp.einsum("bhsk,bhsv->bhkv", kc * jnp.exp(last - cum), vc))
        return state, out

    s_final, out = lax.scan(step, s0, (to_chunks(q), to_chunks(k), to_chunks(v), to_chunks(log_f)))
    out = out.transpose(1, 0, 3, 2, 4).reshape(b_sz, n, h, v.shape[-1])
    return out, s_final


def mla_expand(c_kv, k_rope, w_kv_b):
    b_sz, n, _ = c_kv.shape
    kv = (c_kv @ w_kv_b).reshape(b_sz, n, MLA_HEADS, MLA_NOPE + MLA_V)
    k = jnp.concatenate(
        [kv[..., :MLA_NOPE], jnp.broadcast_to(k_rope[:, :, None, :], (b_sz, n, MLA_HEADS, MLA_ROPE))],
        axis=-1)
    return k, kv[..., MLA_NOPE:]


def attend(q, k, v):
    b_sz, sq, h, dh = q.shape
    nb = sq // Q_BLOCK
    qb = q.reshape(b_sz, nb, Q_BLOCK, h, dh).transpose(1, 0, 2, 3, 4)
    scale = dh ** -0.5

    def one_block(qi):
        s = jnp.einsum("bqhd,bkhd->bhqk", qi, k, preferred_element_type=f32) * scale
        p = jax.nn.softmax(s, axis=-1)
        return jnp.einsum("bhqk,bkhd->bqhd", p.astype(v.dtype), v)

    o = lax.map(one_block, qb)
    return o.transpose(1, 0, 2, 3, 4).reshape(b_sz, sq, h, v.shape[-1])


def token_mixers(h, lp, rope, ctx):
    b_sz, n, _ = h.shape
    proj = h @ lp["w_in"]
    pts = []
    acc = 0
    for s in IN_SIZES[:-1]:
        acc += s
        pts.append(acc)
    u_fn, hq, hv, hff, hfb, hgate, qa, kva, kr = jnp.split(proj, pts, axis=-1)

    u4 = u_fn.reshape(b_sz, n, FN_HEADS, FN_DH).astype(f32)
    y_fn = jnp.real(jnp.fft.fft2(u4, axes=(1, 3), norm="ortho")).reshape(b_sz, n, FN_WIDTH).astype(h.dtype)

    lb = lp["lb"]
    logf_f, key_f = forget_gate(hff.reshape(b_sz, n, HG_HEADS, HG_DK), lb[0].reshape(HG_HEADS, HG_DK))
    logf_b, key_b = forget_gate(hfb.reshape(b_sz, n, HG_HEADS, HG_DK), lb[1].reshape(HG_HEADS, HG_DK))
    q_hg = hq.reshape(b_sz, n, HG_HEADS, HG_DK).astype(f32)
    v_hg = hv.reshape(b_sz, n, HG_HEADS, HG_DV).astype(f32)
    if ctx is None:
        s0_f = jnp.zeros((b_sz, HG_HEADS, HG_DK, HG_DV), f32)
        s0_b = s0_f
    else:
        s0_f = ctx[2].astype(f32)
        s0_b = ctx[3].astype(f32)
    o_f, s_f = hgrn_chunked(q_hg, key_f, v_hg, logf_f, s0_f)
    flip = lambda t: jnp.flip(t, axis=1)
    o_b, s_b = hgrn_chunked(flip(q_hg), flip(key_b), flip(v_hg), flip(logf_b), s0_b)
    o_hg = rmsnorm(o_f + flip(o_b), lp["hg_gain"].reshape(HG_HEADS, HG_DV)).astype(h.dtype)
    y_hg = o_hg.reshape(b_sz, n, HG_WIDTH) * jax.nn.silu(hgate)

    c_kv = rmsnorm(kva, lp["kv_norm"])
    q = (rmsnorm(qa, lp["q_norm"]) @ lp["w_q_b"]).reshape(b_sz, n, MLA_HEADS, MLA_NOPE + MLA_ROPE)
    q_nope, q_rope = q[..., :MLA_NOPE], q[..., MLA_NOPE:]
    k_rope = kr
    if rope is not None:
        cos, sin = rope
        q_rope = apply_axial_rope(q_rope, cos[:, None, :], sin[:, None, :])
        k_rope = apply_axial_rope(kr, cos, sin)
    k_all, v_all = mla_expand(c_kv, k_rope, lp["w_kv_b"])
    if ctx is not None:
        k_c, v_c = mla_expand(ctx[0], ctx[1], lp["w_kv_b"])
        k_all = jnp.concatenate([k_c, k_all], axis=1)
        v_all = jnp.concatenate([v_c, v_all], axis=1)
    y_mla = attend(jnp.concatenate([q_nope, q_rope], axis=-1), k_all, v_all).reshape(b_sz, n, MLA_WIDTH)

    y = jnp.concatenate([y_fn, y_hg, y_mla], axis=-1) @ lp["w_out"]
    return y, (c_kv, kr, s_f.astype(h.dtype), s_b.astype(h.dtype))


def trunk_layer(x, cond, lp, rope, ctx):
    mod = jax.nn.silu(cond) @ lp["w_ada"] + lp["b_ada"]
    sh1, sc1, g1, sh2, sc2, g2 = jnp.split(mod[:, None, :], 6, axis=-1)
    h = rmsnorm(x, lp["g_pre_mix"]) * (1.0 + sc1) + sh1
    y, ctx_out = token_mixers(h, lp, rope, ctx)
    x = x + g1 * rmsnorm(y, lp["g_post_mix"])
    h = rmsnorm(x, lp["g_pre_ff"]) * (1.0 + sc2) + sh2
    f = jnp.square(jax.nn.relu(h @ lp["w_ff1"])) @ lp["w_ff2"]
    x = x + g2 * rmsnorm(f, lp["g_post_ff"])
    return x, ctx_out


def setup_inputs(seed: int = 0) -> dict:
    key = jax.random.key(seed)
    ks = jax.random.split(key, 32)
    nrm = lambda k, shape, scale: jax.random.normal(k, shape, f32) * scale
    gain = lambda k, shape: 1.0 + 0.05 * jax.random.normal(k, shape, f32)
    return {
        "x_prompt": nrm(ks[0], (BATCH, SEQ, D_MODEL), 1.0),
        "x_sample": nrm(ks[1], (DEC_BATCH, DEC_SEQ, D_MODEL), 1.0),
        "c": nrm(ks[2], (DEC_BATCH, D_MODEL), 1.0),
        "cache_ckv": nrm(ks[3], (DEC_BATCH, DEPTH, PAST_LEN, KV_LORA), 1.0),
        "cache_krope": nrm(ks[4], (DEC_BATCH, DEPTH, PAST_LEN, MLA_ROPE), 1.0),
        "state_hgrn": nrm(ks[5], (DEC_BATCH, DEPTH, 2, HG_HEADS, HG_DK, HG_DV), 0.5),
        "c_ctx": nrm(ks[6], (D_MODEL,), 1.0),
        "w_ada": nrm(ks[7], (DEPTH, D_MODEL, 6 * D_MODEL), 0.5 * D_MODEL ** -0.5),
        "b_ada": nrm(ks[8], (DEPTH, 6 * D_MODEL), 0.02),
        "g_pre_mix": gain(ks[9], (DEPTH, D_MODEL)),
        "g_post_mix": gain(ks[10], (DEPTH, D_MODEL)),
        "g_pre_ff": gain(ks[11], (DEPTH, D_MODEL)),
        "g_post_ff": gain(ks[12], (DEPTH, D_MODEL)),
        "w_in": nrm(ks[13], (DEPTH, D_MODEL, IN_COLS), D_MODEL ** -0.5),
        "hg_lb": nrm(ks[14], (DEPTH, 2, HG_QK), 1.0),
        "hg_gain": gain(ks[15], (DEPTH, HG_WIDTH)),
        "mla_q_norm": gain(ks[16], (DEPTH, Q_LORA)),
        "mla_kv_norm": gain(ks[17], (DEPTH, KV_LORA)),
        "w_q_b": nrm(ks[18], (DEPTH, Q_LORA, MLA_HEADS * (MLA_NOPE + MLA_ROPE)), Q_LORA ** -0.5),
        "w_kv_b": nrm(ks[19], (DEPTH, KV_LORA, MLA_HEADS * (MLA_NOPE + MLA_V)), KV_LORA ** -0.5),
        "w_out": nrm(ks[20], (DEPTH, D_MIX, D_MODEL), D_MIX ** -0.5),
        "w_ff1": nrm(ks[21], (DEPTH, D_MODEL, D_FF), D_MODEL ** -0.5),
        "w_ff2": nrm(ks[22], (DEPTH, D_FF, D_MODEL), D_FF ** -0.5),
    }


def reference(x_prompt, x_sample, c, cache_ckv, cache_krope, state_hgrn, c_ctx,
              w_ada, b_ada, g_pre_mix, g_post_mix, g_pre_ff, g_post_ff,
              w_in, hg_lb, hg_gain, mla_q_norm, mla_kv_norm, w_q_b, w_kv_b,
              w_out, w_ff1, w_ff2):
    p_lb = jax.nn.softmax(hg_lb.astype(f32), axis=0)
    lb_all = jnp.cumsum(p_lb, axis=0) - p_lb[0]

    rows = x_sample.shape[1] // GRID_W
    rope = axial_rope_tables(rows)
    cond_ctx = jnp.broadcast_to(c_ctx[None, :], (x_prompt.shape[0], c_ctx.shape[0]))

    yp = x_prompt
    ys = x_sample
    ckv_list, kr_list, st_list = [], [], []
    for l in range(DEPTH):
        lp = {
            "w_ada": w_ada[l], "b_ada": b_ada[l],
            "g_pre_mix": g_pre_mix[l], "g_post_mix": g_post_mix[l],
            "g_pre_ff": g_pre_ff[l], "g_post_ff": g_post_ff[l],
            "w_in": w_in[l], "lb": lb_all[l], "hg_gain": hg_gain[l],
            "q_norm": mla_q_norm[l], "kv_norm": mla_kv_norm[l],
            "w_q_b": w_q_b[l], "w_kv_b": w_kv_b[l], "w_out": w_out[l],
            "w_ff1": w_ff1[l], "w_ff2": w_ff2[l],
        }
        yp, (ckv_l, kr_l, sf_l, sb_l) = trunk_layer(yp, cond_ctx, lp, None, None)
        ckv_list.append(ckv_l)
        kr_list.append(kr_l)
        st_list.append(jnp.stack([sf_l, sb_l], axis=1))
        ctx = (cache_ckv[:, l], cache_krope[:, l], state_hgrn[:, l, 0], state_hgrn[:, l, 1])
        ys, _ = trunk_layer(ys, c, lp, rope, ctx)

    new_ckv = jnp.stack(ckv_list, axis=1)
    new_krope = jnp.stack(kr_list, axis=1)
    new_state_hgrn = jnp.stack(st_list, axis=1)
    return (yp, ys, new_ckv, new_krope, new_state_hgrn)
```

```python
import functools

import numpy as np
import jax
import jax.numpy as jnp
from jax import lax
from jax.experimental import pallas as pl
from jax.experimental.pallas import tpu as pltpu

f32 = jnp.float32
bf16 = jnp.bfloat16

D_MODEL = 2048
DEPTH = 2
GRID_W = 64
FN_HEADS = 4
FN_DH = 128
FN_WIDTH = FN_HEADS * FN_DH
HG_HEADS = 4
HG_DK = 128
HG_DV = 128
HG_QK = HG_HEADS * HG_DK
HG_WIDTH = HG_HEADS * HG_DV
MLA_HEADS = 8
MLA_NOPE = 128
MLA_ROPE = 64
MLA_V = 128
Q_LORA = 768
KV_LORA = 512
MLA_WIDTH = MLA_HEADS * MLA_V
D_FF = 4 * D_MODEL
ROPE_BASE = 10000.0
EPS = 1e-6
F_FLOOR = 1e-30

LANES = 128
IN_COLS = 4416
IN_COLS_PAD = 4480
IN_TILE = 640
COL_HQ, COL_HV, COL_HFF, COL_HFB, COL_HGATE = 1, 2, 3, 4, 5
COL_QA = 4
COL_KVR = 6
QHEAD_PAD = 256
HG_CHUNK = 64
HG_LEVELS = (32, 16, 8, 4, 2, 1)
VMEM_LIMIT = 56 * 1024 * 1024


def _params(semantics):
    return pltpu.CompilerParams(dimension_semantics=semantics, vmem_limit_bytes=VMEM_LIMIT)


def _dot(a, b):
    return jnp.dot(a, b, preferred_element_type=f32)


def _dot_nt(a, b):
    return lax.dot_general(a, b, (((1,), (1,)), ((), ())), preferred_element_type=f32)


def _dot_tn(a, b):
    return lax.dot_general(a, b, (((0,), (0,)), ((), ())), preferred_element_type=f32)


def _rms(x):
    return x * lax.rsqrt(jnp.mean(x * x, axis=-1, keepdims=True) + EPS)


def _ada_kernel(cond_ref, w_ref, b_ref, o_ref):
    cnd = cond_ref[...]
    act = cnd * jax.nn.sigmoid(cnd)
    o_ref[0] = _dot(act.astype(bf16), w_ref[0].astype(bf16)) + b_ref[0]


def _ada(cond, w_ada, b_ada):
    rows = cond.shape[0]
    tn = 1024
    return pl.pallas_call(
        _ada_kernel,
        grid=(DEPTH, 6 * D_MODEL // tn),
        in_specs=[
            pl.BlockSpec((rows, D_MODEL), lambda l, j: (0, 0)),
            pl.BlockSpec((1, D_MODEL, tn), lambda l, j: (l, 0, j)),
            pl.BlockSpec((1, 1, tn), lambda l, j: (l, 0, j)),
        ],
        out_specs=pl.BlockSpec((1, rows, tn), lambda l, j: (l, 0, j)),
        out_shape=jax.ShapeDtypeStruct((DEPTH, rows, 6 * D_MODEL), f32),
        compiler_params=_params(("parallel", "parallel")),
        name="ada_mod",
    )(cond, w_ada, b_ada.reshape(DEPTH, 1, 6 * D_MODEL))


def _inproj_kernel(x_ref, mod_ref, g_ref, w_ref, o_ref, h_scr):
    @pl.when(pl.program_id(2) == 0)
    def _():
        y = _rms(x_ref[0]) * g_ref[...]
        h_scr[...] = (y * (1.0 + mod_ref[0, 1:2, :]) + mod_ref[0, 0:1, :]).astype(bf16)

    o_ref[0] = _dot(h_scr[...], w_ref[...])


def _inproj(x, mod, g, w_in, tm):
    b_sz, n, _ = x.shape
    return pl.pallas_call(
        _inproj_kernel,
        grid=(b_sz, n // tm, IN_COLS_PAD // IN_TILE),
        in_specs=[
            pl.BlockSpec((1, tm, D_MODEL), lambda b, i, j: (b, i, 0)),
            pl.BlockSpec((1, 6, D_MODEL), lambda b, i, j: (b, 0, 0)),
            pl.BlockSpec((1, D_MODEL), lambda b, i, j: (0, 0)),
            pl.BlockSpec((D_MODEL, IN_TILE), lambda b, i, j: (0, j)),
        ],
        out_specs=pl.BlockSpec((1, tm, IN_TILE), lambda b, i, j: (b, i, j)),
        out_shape=jax.ShapeDtypeStruct((b_sz, n, IN_COLS_PAD), f32),
        scratch_shapes=[pltpu.VMEM((tm, D_MODEL), bf16)],
        compiler_params=_params(("parallel", "parallel", "arbitrary")),
        name="inproj",
    )(x, mod, g, w_in)


def _fourier_kernel(u_ref, ccs_ref, cn_ref, sn_ref, o_ref, ucs_scr):
    @pl.when(pl.program_id(1) == 0)
    def _():
        ucs_scr[...] = _dot(u_ref[0].astype(bf16), ccs_ref[...]).astype(bf16)

    y = _dot(cn_ref[...], ucs_scr[:, :FN_WIDTH]) + _dot(sn_ref[...], ucs_scr[:, FN_WIDTH:])
    o_ref[0] = y.astype(bf16)


def _fourier(proj, ccs, cn, sn_neg, tr):
    b_sz, n, _ = proj.shape
    return pl.pallas_call(
        _fourier_kernel,
        grid=(b_sz, n // tr),
        in_specs=[
            pl.BlockSpec((1, n, FN_WIDTH), lambda b, i: (b, 0, 0)),
            pl.BlockSpec((FN_WIDTH, 2 * FN_WIDTH), lambda b, i: (0, 0)),
            pl.BlockSpec((tr, n), lambda b, i: (i, 0)),
            pl.BlockSpec((tr, n), lambda b, i: (i, 0)),
        ],
        out_specs=pl.BlockSpec((1, tr, FN_WIDTH), lambda b, i: (b, i, 0)),
        out_shape=jax.ShapeDtypeStruct((b_sz, n, FN_WIDTH), bf16),
        scratch_shapes=[pltpu.VMEM((n, 2 * FN_WIDTH), bf16)],
        compiler_params=_params(("parallel", "arbitrary")),
        name="fourier",
    )(proj, ccs, cn, sn_neg)


def _dft_tables(n):
    j = jnp.arange(n, dtype=jnp.int32)
    ang = ((j[:, None] * j[None, :]) % n).astype(f32) * (2.0 * np.pi / n)
    scale_n = 1.0 / np.sqrt(n)
    cn = (jnp.cos(ang) * scale_n).astype(bf16)
    sn_neg = (-jnp.sin(ang) * scale_n).astype(bf16)
    return cn, sn_neg


def _channel_tables():
    k = np.arange(FN_DH)
    ang = 2.0 * np.pi * ((k[:, None] * k[None, :]) % FN_DH) / FN_DH
    eye = np.eye(FN_HEADS)
    cc = np.kron(eye, np.cos(ang)) / np.sqrt(FN_DH)
    sc = np.kron(eye, np.sin(ang)) / np.sqrt(FN_DH)
    return jnp.asarray(np.concatenate([cc, sc], axis=1), dtype=bf16)


def _hgrn_tables():
    c = HG_CHUNK
    idx = np.arange(c)
    blocks = [(idx[None, :] <= idx[:, None]), (idx[None, :] > idx[:, None])]
    masks = []
    for h in HG_LEVELS:
        mid = (idx // (2 * h)) * (2 * h) + h
        upper = idx >= mid
        row_up = (idx[None, :] >= mid[:, None]) & (idx[None, :] <= idx[:, None])
        row_lo = (idx[None, :] > idx[:, None]) & (idx[None, :] < mid[:, None])
        blocks.append(np.where(upper[:, None], row_up, row_lo))
        same = (idx[:, None] // (2 * h)) == (idx[None, :] // (2 * h))
        masks.append(same & upper[:, None] & ~upper[None, :])
    masks.append(np.eye(c, dtype=bool))
    a_f = np.concatenate([b.astype(np.float32) for b in blocks], axis=0)
    a_b = np.concatenate([b[::-1, ::-1].astype(np.float32) for b in blocks], axis=0)
    m_f = np.stack([m.astype(np.float32) for m in masks])
    m_b = np.stack([m[::-1, ::-1].astype(np.float32) for m in masks])
    rep = lambda a: jnp.asarray(np.concatenate([a, a, a], axis=1), dtype=bf16)
    return rep(a_f), rep(a_b), jnp.asarray(m_f), jnp.asarray(m_b)


def _hgrn_direction(q, v, pre, lb, a_ref, m_ref, st_scr, dirn, o_ref, total_row):
    c = HG_CHUNK
    nl = len(HG_LEVELS)
    f = lb + (1.0 - lb) * jax.nn.sigmoid(pre)
    lf = jnp.log(jnp.maximum(f, F_FLOOR))
    kk = 1.0 - f
    p1 = lf.astype(bf16)
    r1 = lf - p1.astype(f32)
    p2 = r1.astype(bf16)
    p3 = (r1 - p2.astype(f32)).astype(bf16)
    dec = jnp.exp(_dot(a_ref[...], jnp.concatenate([p1, p2, p3], axis=0)))
    e_cum = dec[0:c]
    e_rest = dec[c:2 * c]
    total = dec[total_row:total_row + 1]
    for h in range(HG_HEADS):
        sl = slice(h * HG_DK, (h + 1) * HG_DK)
        qh = q[:, sl]
        kh = kk[:, sl]
        vh = v[:, sl].astype(bf16)
        sc = m_ref[nl] * _dot_nt(qh.astype(bf16), kh.astype(bf16))
        for li in range(nl):
            el = dec[(2 + li) * c:(3 + li) * c, sl]
            sc = sc + m_ref[li] * _dot_nt((qh * el).astype(bf16), (kh * el).astype(bf16))
        st = st_scr[dirn, h]
        o = _dot(sc.astype(bf16), vh) + _dot_nt((qh * e_cum[:, sl]).astype(bf16), st.astype(bf16))
        o_ref[0, :, sl] = o
        st_scr[dirn, h] = total[:, sl] * st + _dot_tn(vh, (kh * e_rest[:, sl]).astype(bf16))


def _hgrn_kernel(layer, has_s0, has_sout, *refs):
    (qf_ref, vf_ref, ff_ref, qb_ref, vb_ref, fb_ref, lb_ref, af_ref, ab_ref, mf_ref, mb_ref) = refs[:11]
    pos = 11
    s0_ref = None
    if has_s0:
        s0_ref = refs[pos]
        pos += 1
    of_ref, ob_ref = refs[pos], refs[pos + 1]
    pos += 2
    sout_ref = None
    if has_sout:
        sout_ref = refs[pos]
        pos += 1
    st_scr = refs[pos]
    ci = pl.program_id(1)

    @pl.when(ci == 0)
    def _():
        for d in range(2):
            for h in range(HG_HEADS):
                if has_s0:
                    st_scr[d, h] = s0_ref[0, d, h].T
                else:
                    st_scr[d, h] = jnp.zeros((HG_DV, HG_DK), f32)

    raw = lb_ref[...]
    mx = raw[0]
    for i in range(1, DEPTH):
        mx = jnp.maximum(mx, raw[i])
    ex = [jnp.exp(raw[i] - mx) for i in range(DEPTH)]
    den = ex[0]
    for i in range(1, DEPTH):
        den = den + ex[i]
    lb = jnp.zeros_like(den)
    for i in range(1, layer + 1):
        lb = lb + ex[i] / den

    c = HG_CHUNK
    _hgrn_direction(qf_ref[0], vf_ref[0], ff_ref[0], lb[0:1], af_ref, mf_ref, st_scr, 0, of_ref, c - 1)
    _hgrn_direction(qb_ref[0], vb_ref[0], fb_ref[0], lb[1:2], ab_ref, mb_ref, st_scr, 1, ob_ref, 0)

    if has_sout:
        @pl.when(ci == pl.num_programs(1) - 1)
        def _():
            for d in range(2):
                for h in range(HG_HEADS):
                    sout_ref[0, d, h] = st_scr[d, h].T


def _hgrn(proj, hg_lb, tables, layer, s0, want_state):
    b_sz, n, _ = proj.shape
    nc = n // HG_CHUNK
    a_f, a_b, m_f, m_b = tables
    c = HG_CHUNK
    fwd = lambda col: pl.BlockSpec((1, c, HG_QK), lambda b, i: (b, i, col))
    bwd = lambda col: pl.BlockSpec((1, c, HG_QK), lambda b, i: (b, nc - 1 - i, col))
    const = lambda shape: pl.BlockSpec(shape, lambda b, i: (0,) * len(shape))
    in_specs = [fwd(COL_HQ), fwd(COL_HV), fwd(COL_HFF), bwd(COL_HQ), bwd(COL_HV), bwd(COL_HFB),
                const(hg_lb.shape), const(a_f.shape), const(a_b.shape), const(m_f.shape), const(m_b.shape)]
    args = [proj, proj, proj, proj, proj, proj, hg_lb, a_f, a_b, m_f, m_b]
    state_spec = pl.BlockSpec((1, 2, HG_HEADS, HG_DK, HG_DV), lambda b, i: (b, 0, 0, 0, 0))
    if s0 is not None:
        in_specs.append(state_spec)
        args.append(s0)
    out_specs = [pl.BlockSpec((1, c, HG_WIDTH), lambda b, i: (b, i, 0)),
                 pl.BlockSpec((1, c, HG_WIDTH), lambda b, i: (b, nc - 1 - i, 0))]
    out_shape = [jax.ShapeDtypeStruct((b_sz, n, HG_WIDTH), f32)] * 2
    if want_state:
        out_specs.append(state_spec)
        out_shape.append(jax.ShapeDtypeStruct((b_sz, 2, HG_HEADS, HG_DK, HG_DV), f32))
    return pl.pallas_call(
        functools.partial(_hgrn_kernel, layer, s0 is not None, want_state),
        grid=(b_sz, nc),
        in_specs=in_specs,
        out_specs=out_specs,
        out_shape=out_shape,
        scratch_shapes=[pltpu.VMEM((2, HG_HEADS, HG_DV, HG_DK), f32)],
        compiler_params=_params(("parallel", "arbitrary")),
        name="hgrn",
    )(*args)


def _rope_rotate(x, cos, sin):
    lane = lax.broadcasted_iota(jnp.int32, x.shape, 1)
    first_half = (lane % (MLA_ROPE // 2)) < (MLA_ROPE // 4)
    quarter = MLA_ROPE // 4
    rot = jnp.where(first_half, -pltpu.roll(x, LANES - quarter, 1), pltpu.roll(x, quarter, 1))
    return x * cos + rot * sin


def _mla_kernel(n_ctx, use_rope, want_cache, tq, *refs):
    qa_ref, kvr_ref, wq_ref, wkv_ref, qn_ref, kvn_ref = refs[:6]
    pos = 6
    if n_ctx:
        cckv_ref, ckr_ref = refs[pos], refs[pos + 1]
        pos += 2
    if use_rope:
        cosq_ref, sinq_ref, cosk_ref, sink_ref = refs[pos:pos + 4]
        pos += 4
    y_ref = refs[pos]
    pos += 1
    if want_cache:
        ockv_ref, okr_ref = refs[pos], refs[pos + 1]
        pos += 2
    kv_scr, kr_scr = refs[pos], refs[pos + 1]
    n = kvr_ref.shape[1]
    rows = 512 if n % 512 == 0 else n

    @pl.when(pl.program_id(1) == 0)
    def _():
        if n_ctx:
            kv_scr[0:n_ctx] = _dot(cckv_ref[0, 0].astype(bf16), wkv_ref[...]).astype(bf16)
            kr_scr[0:n_ctx, 0:MLA_ROPE] = ckr_ref[0, 0].astype(bf16)
            kr_scr[0:n_ctx, MLA_ROPE:LANES] = jnp.zeros((n_ctx, LANES - MLA_ROPE), bf16)
        for r0 in range(0, n, rows):
            blk = kvr_ref[0, r0:r0 + rows]
            c_kv = _rms(blk[:, :KV_LORA]) * kvn_ref[...]
            krp = blk[:, KV_LORA:]
            if want_cache:
                ockv_ref[0, r0:r0 + rows] = c_kv
                okr_ref[0, r0:r0 + rows] = krp[:, :MLA_ROPE]
            if use_rope:
                krp = _rope_rotate(krp, cosk_ref[r0:r0 + rows], sink_ref[r0:r0 + rows])
            kv_scr[n_ctx + r0:n_ctx + r0 + rows] = _dot(c_kv.astype(bf16), wkv_ref[...]).astype(bf16)
            kr_scr[n_ctx + r0:n_ctx + r0 + rows] = krp.astype(bf16)

    scale = float(MLA_NOPE + MLA_ROPE) ** -0.5
    qn = (_rms(qa_ref[0]) * qn_ref[...]).astype(bf16)
    q = _dot(qn, wq_ref[...]) * scale
    kr_all = kr_scr[...]
    for h in range(MLA_HEADS):
        base = h * QHEAD_PAD
        q_nope = q[:, base:base + MLA_NOPE].astype(bf16)
        q_rope = q[:, base + MLA_NOPE:base + QHEAD_PAD]
        if use_rope:
            q_rope = _rope_rotate(q_rope, cosq_ref[...], sinq_ref[...])
        s = _dot_nt(q_nope, kv_scr[:, base:base + MLA_NOPE]) + _dot_nt(q_rope.astype(bf16), kr_all)
        p = jnp.exp(s - jnp.max(s, axis=-1, keepdims=True))
        denom = jnp.sum(p, axis=-1, keepdims=True)
        o = _dot(p.astype(bf16), kv_scr[:, base + MLA_NOPE:base + QHEAD_PAD])
        y_ref[0, :, h * MLA_V:(h + 1) * MLA_V] = (o / denom).astype(bf16)


def _mla(proj, wq, wkv, q_norm, kv_norm, ctx, layer, rope, want_cache, tq):
    b_sz, n, _ = proj.shape
    n_ctx = 0 if ctx is None else ctx[0].shape[2]
    sk = n_ctx + n
    const = lambda shape: pl.BlockSpec(shape, lambda b, i: (0,) * len(shape))
    in_specs = [
        pl.BlockSpec((1, tq, Q_LORA), lambda b, i: (b, i, COL_QA)),
        pl.BlockSpec((1, n, IN_TILE), lambda b, i: (b, 0, COL_KVR)),
        const(wq.shape), const(wkv.shape), const(q_norm.shape), const(kv_norm.shape),
    ]
    args = [proj, proj, wq, wkv, q_norm, kv_norm]
    if ctx is not None:
        in_specs += [pl.BlockSpec((1, 1, n_ctx, KV_LORA), lambda b, i: (b, layer, 0, 0)),
                     pl.BlockSpec((1, 1, n_ctx, MLA_ROPE), lambda b, i: (b, layer, 0, 0))]
        args += [ctx[0], ctx[1]]
    if rope is not None:
        cos, sin = rope
        in_specs += [pl.BlockSpec((tq, LANES), lambda b, i: (i, 0)),
                     pl.BlockSpec((tq, LANES), lambda b, i: (i, 0)),
                     const(cos.shape), const(sin.shape)]
        args += [cos, sin, cos, sin]
    out_specs = [pl.BlockSpec((1, tq, MLA_WIDTH), lambda b, i: (b, i, 0))]
    out_shape = [jax.ShapeDtypeStruct((b_sz, n, MLA_WIDTH), bf16)]
    if want_cache:
        out_specs += [pl.BlockSpec((1, n, KV_LORA), lambda b, i: (b, 0, 0)),
                      pl.BlockSpec((1, n, MLA_ROPE), lambda b, i: (b, 0, 0))]
        out_shape += [jax.ShapeDtypeStruct((b_sz, n, KV_LORA), f32),
                      jax.ShapeDtypeStruct((b_sz, n, MLA_ROPE), f32)]
    return pl.pallas_call(
        functools.partial(_mla_kernel, n_ctx, rope is not None, want_cache, tq),
        grid=(b_sz, n // tq),
        in_specs=in_specs,
        out_specs=out_specs,
        out_shape=out_shape,
        scratch_shapes=[pltpu.VMEM((sk, MLA_HEADS * QHEAD_PAD), bf16), pltpu.VMEM((sk, LANES), bf16)],
        compiler_params=_params(("parallel", "arbitrary")),
        name="mla",
    )(*args)


def _rope_tables(n):
    t = jnp.arange(n)
    r = (t // GRID_W).astype(f32)
    col = (t % GRID_W).astype(f32)
    nf = MLA_ROPE // 4
    inv = ROPE_BASE ** (-jnp.arange(nf, dtype=f32) / nf)
    ar = r[:, None] * inv
    ac = col[:, None] * inv
    ang = jnp.concatenate([ar, ar, ac, ac], axis=-1)
    pad = LANES - MLA_ROPE
    cos = jnp.concatenate([jnp.cos(ang), jnp.ones((n, pad), f32)], axis=-1)
    sin = jnp.concatenate([jnp.sin(ang), jnp.zeros((n, pad), f32)], axis=-1)
    return cos, sin


def _outproj_kernel(x_ref, yfn_ref, of_ref, ob_ref, gate_ref, ymla_ref, w_ref, hgg_ref, gpost_ref,
                    mod_ref, o_ref):
    o_hg = of_ref[0] + ob_ref[0]
    gate = gate_ref[0]
    gate = gate * jax.nn.sigmoid(gate)
    parts = []
    for h in range(HG_HEADS):
        sl = slice(h * HG_DV, (h + 1) * HG_DV)
        parts.append((_rms(o_hg[:, sl]) * hgg_ref[:, sl] * gate[:, sl]).astype(bf16))
    y = _dot(yfn_ref[0], w_ref[0:FN_WIDTH])
    for h in range(HG_HEADS):
        r0 = FN_WIDTH + h * HG_DV
        y = y + _dot(parts[h], w_ref[r0:r0 + HG_DV])
    y = y + _dot(ymla_ref[0], w_ref[FN_WIDTH + HG_WIDTH:])
    o_ref[0] = x_ref[0] + mod_ref[0, 2:3, :] * (_rms(y) * gpost_ref[...])


def _outproj(x, y_fn, o_f, o_b, proj, y_mla, w_out, hg_gain, g_post, mod, tm):
    b_sz, n, _ = x.shape
    tok = lambda width, col=0: pl.BlockSpec((1, tm, width), lambda b, i: (b, i, col))
    return pl.pallas_call(
        _outproj_kernel,
        grid=(b_sz, n // tm),
        in_specs=[
            tok(D_MODEL), tok(FN_WIDTH), tok(HG_WIDTH), tok(HG_WIDTH), tok(HG_WIDTH, COL_HGATE),
            tok(MLA_WIDTH),
            pl.BlockSpec((D_MODEL, D_MODEL), lambda b, i: (0, 0)),
            pl.BlockSpec((1, HG_WIDTH), lambda b, i: (0, 0)),
            pl.BlockSpec((1, D_MODEL), lambda b, i: (0, 0)),
            pl.BlockSpec((1, 6, D_MODEL), lambda b, i: (b, 0, 0)),
        ],
        out_specs=tok(D_MODEL),
        out_shape=jax.ShapeDtypeStruct((b_sz, n, D_MODEL), f32),
        compiler_params=_params(("parallel", "parallel")),
        name="outproj",
    )(x, y_fn, o_f, o_b, proj, y_mla, w_out, hg_gain, g_post, mod)


def _ffn_kernel(x_ref, mod_ref, gpre_ref, gpost_ref, w1_ref, w2_ref, o_ref, h_scr, acc_scr):
    k = pl.program_id(2)

    @pl.when(k == 0)
    def _():
        y = _rms(x_ref[0]) * gpre_ref[...]
        h_scr[...] = (y * (1.0 + mod_ref[0, 4:5, :]) + mod_ref[0, 3:4, :]).astype(bf16)
        acc_scr[...] = jnp.zeros_like(acc_scr)

    a = jnp.maximum(_dot(h_scr[...], w1_ref[...]), 0.0)
    acc_scr[...] += _dot((a * a).astype(bf16), w2_ref[...])

    @pl.when(k == pl.num_programs(2) - 1)
    def _():
        o_ref[0] = x_ref[0] + mod_ref[0, 5:6, :] * (_rms(acc_scr[...]) * gpost_ref[...])


def _ffn(x, mod, g_pre, g_post, w1, w2, tm, tf):
    b_sz, n, _ = x.shape
    return pl.pallas_call(
        _ffn_kernel,
        grid=(b_sz, n // tm, D_FF // tf),
        in_specs=[
            pl.BlockSpec((1, tm, D_MODEL), lambda b, i, k: (b, i, 0)),
            pl.BlockSpec((1, 6, D_MODEL), lambda b, i, k: (b, 0, 0)),
            pl.BlockSpec((1, D_MODEL), lambda b, i, k: (0, 0)),
            pl.BlockSpec((1, D_MODEL), lambda b, i, k: (0, 0)),
            pl.BlockSpec((D_MODEL, tf), lambda b, i, k: (0, k)),
            pl.BlockSpec((tf, D_MODEL), lambda b, i, k: (k, 0)),
        ],
        out_specs=pl.BlockSpec((1, tm, D_MODEL), lambda b, i, k: (b, i, 0)),
        out_shape=jax.ShapeDtypeStruct((b_sz, n, D_MODEL), f32),
        scratch_shapes=[pltpu.VMEM((tm, D_MODEL), bf16), pltpu.VMEM((tm, D_MODEL), f32)],
        compiler_params=_params(("parallel", "parallel", "arbitrary")),
        name="ffn",
    )(x, mod, g_pre, g_post, w1, w2)


def _layer(x, mod, wl, hg_tables, dft, layer, n_seq, ctx, s0, rope, is_context, tm, tq):
    b_sz, n, _ = x.shape
    n_seqs = b_sz * n // n_seq
    proj = _inproj(x, mod, wl["g_pre_mix"], wl["w_in"], tm).reshape(n_seqs, n_seq, IN_COLS_PAD)
    y_fn = _fourier(proj, dft[0], dft[1], dft[2], min(n_seq, 512))
    hg = _hgrn(proj, wl["hg_lb"], hg_tables, layer, s0, is_context)
    mla = _mla(proj, wl["wq"], wl["wkv"], wl["q_norm"], wl["kv_norm"], ctx, layer, rope, is_context, tq)
    flat = lambda a: a.reshape(b_sz, n, a.shape[-1])
    x = _outproj(x, flat(y_fn), flat(hg[0]), flat(hg[1]), flat(proj), flat(mla[0]), wl["w_out"],
                 wl["hg_gain"], wl["g_post_mix"], mod, tm)
    x = _ffn(x, mod, wl["g_pre_ff"], wl["g_post_ff"], wl["w_ff1"], wl["w_ff2"], tm, 512)
    extras = (mla[1], mla[2], hg[2]) if is_context else None
    return x, extras


def kernel(x_prompt, x_sample, c, cache_ckv, cache_krope, state_hgrn, c_ctx, w_ada, b_ada, g_pre_mix,
           g_post_mix, g_pre_ff, g_post_ff, w_in, hg_lb, hg_gain, mla_q_norm, mla_kv_norm, w_q_b, w_kv_b,
           w_out, w_ff1, w_ff2):
    batch, seq, _ = x_prompt.shape
    dec_batch, dec_seq, _ = x_sample.shape

    rows = 16
    cond = jnp.concatenate([c_ctx[None, :], c, jnp.zeros((rows - 1 - dec_batch, D_MODEL), f32)], axis=0)
    mod = _ada(cond, w_ada, b_ada).reshape(DEPTH, rows, 6, D_MODEL)

    hg_tables = _hgrn_tables()
    ccs = _channel_tables()
    dft_p = (ccs,) + _dft_tables(seq)
    dft_s = (ccs,) + _dft_tables(dec_seq)
    rope = _rope_tables(dec_seq)

    yp = x_prompt.reshape(1, batch * seq, D_MODEL)
    ys = x_sample
    ckv_list, kr_list, st_list = [], [], []
    for l in range(DEPTH):
        wq = w_q_b[l].reshape(Q_LORA, MLA_HEADS, MLA_NOPE + MLA_ROPE)
        wq = jnp.pad(wq, ((0, 0), (0, 0), (0, QHEAD_PAD - MLA_NOPE - MLA_ROPE)))
        wl = {
            "w_in": jnp.pad(w_in[l].astype(bf16), ((0, 0), (0, IN_COLS_PAD - IN_COLS))),
            "wq": wq.reshape(Q_LORA, MLA_HEADS * QHEAD_PAD).astype(bf16),
            "wkv": w_kv_b[l].astype(bf16),
            "w_out": w_out[l].astype(bf16),
            "w_ff1": w_ff1[l].astype(bf16),
            "w_ff2": w_ff2[l].astype(bf16),
            "g_pre_mix": g_pre_mix[l][None, :], "g_post_mix": g_post_mix[l][None, :],
            "g_pre_ff": g_pre_ff[l][None, :], "g_post_ff": g_post_ff[l][None, :],
            "hg_lb": hg_lb, "hg_gain": hg_gain[l][None, :],
            "q_norm": mla_q_norm[l][None, :], "kv_norm": mla_kv_norm[l][None, :],
        }
        yp, (ckv_l, kr_l, st_l) = _layer(yp, mod[l, 0:1], wl, hg_tables, dft_p, l, seq, None, None, None,
                                         True, 512, seq)
        ckv_list.append(ckv_l)
        kr_list.append(kr_l)
        st_list.append(st_l)
        ys, _ = _layer(ys, mod[l, 1:1 + dec_batch], wl, hg_tables, dft_s, l, dec_seq,
                       (cache_ckv, cache_krope), state_hgrn[:, l], rope, False, 512, 256)

    return (yp.reshape(batch, seq, D_MODEL), ys, jnp.stack(ckv_list, axis=1), jnp.stack(kr_list, axis=1),
            jnp.stack(st_list, axis=1))
```

```python
import functools

import numpy as np
import jax
import jax.numpy as jnp
from jax import lax
from jax.experimental import pallas as pl
from jax.experimental.pallas import tpu as pltpu

f32 = jnp.float32
bf16 = jnp.bfloat16

D_MODEL = 2048
DEPTH = 2
GRID_W = 64
FN_HEADS = 4
FN_DH = 128
FN_WIDTH = FN_HEADS * FN_DH
HG_HEADS = 4
HG_DK = 128
HG_DV = 128
HG_QK = HG_HEADS * HG_DK
HG_WIDTH = HG_HEADS * HG_DV
MLA_HEADS = 8
MLA_NOPE = 128
MLA_ROPE = 64
MLA_V = 128
Q_LORA = 768
KV_LORA = 512
MLA_WIDTH = MLA_HEADS * MLA_V
D_FF = 4 * D_MODEL
ROPE_BASE = 10000.0
EPS = 1e-6
F_FLOOR = 1e-30

LANES = 128
SUBLANES = 8
MXU_DIM = 256
IN_COLS = 4416
IN_COLS_PAD = 4608
IN_TILE = 1536
COL_HQ, COL_HV, COL_HFF, COL_HFB, COL_HGATE = 1, 2, 3, 4, 5
COL_QA = 4
KVR_WIDTH = 640
COL_KVR = 6
QHEAD_PAD = 256
HG_CHUNK = 128
HG_LEVELS = (64, 32, 16, 8, 4, 2, 1)
HG_SPLIT = 2
MLA_KEY_CHUNK = 1280
FFN_TM = 1024
FFN_TF = 512
VMEM_LIMIT = 56 * 1024 * 1024


def _params(semantics):
    return pltpu.CompilerParams(dimension_semantics=semantics, vmem_limit_bytes=VMEM_LIMIT)


def _dot(a, b):
    return jnp.dot(a, b, preferred_element_type=f32)


def _dot_nt(a, b):
    return lax.dot_general(a, b, (((1,), (1,)), ((), ())), preferred_element_type=f32)


def _dot_tn(a, b):
    return lax.dot_general(a, b, (((0,), (0,)), ((), ())), preferred_element_type=f32)


def _rms(x):
    return x * lax.rsqrt(jnp.mean(x * x, axis=-1, keepdims=True) + EPS)


def _ada_kernel(cond_ref, w_ref, b_ref, o_ref):
    cnd = cond_ref[...]
    act = cnd * jax.nn.sigmoid(cnd)
    o_ref[0] = _dot(act.astype(bf16), w_ref[0].astype(bf16)) + b_ref[0]


def _ada(cond, w_ada, b_ada):
    rows = cond.shape[0]
    tn = 1024
    return pl.pallas_call(
        _ada_kernel,
        grid=(DEPTH, 6 * D_MODEL // tn),
        in_specs=[
            pl.BlockSpec((rows, D_MODEL), lambda l, j: (0, 0)),
            pl.BlockSpec((1, D_MODEL, tn), lambda l, j: (l, 0, j)),
            pl.BlockSpec((1, 1, tn), lambda l, j: (l, 0, j)),
        ],
        out_specs=pl.BlockSpec((1, rows, tn), lambda l, j: (l, 0, j)),
        out_shape=jax.ShapeDtypeStruct((DEPTH, rows, 6 * D_MODEL), f32),
        compiler_params=_params(("parallel", "parallel")),
        name="ada_mod",
    )(cond, w_ada, b_ada.reshape(DEPTH, 1, 6 * D_MODEL))


def _inproj_kernel(x_ref, mod_ref, g_ref, w_ref, o_ref, h_scr):
    @pl.when(pl.program_id(2) == 0)
    def _():
        y = _rms(x_ref[0]) * g_ref[...]
        h_scr[...] = (y * (1.0 + mod_ref[0, 1:2, :]) + mod_ref[0, 0:1, :]).astype(bf16)

    o_ref[0] = _dot(h_scr[...], w_ref[...])


def _inproj(x, mod, g, w_in, tm):
    b_sz, n, _ = x.shape
    return pl.pallas_call(
        _inproj_kernel,
        grid=(b_sz, n // tm, IN_COLS_PAD // IN_TILE),
        in_specs=[
            pl.BlockSpec((1, tm, D_MODEL), lambda b, i, j: (b, i, 0)),
            pl.BlockSpec((1, 6, D_MODEL), lambda b, i, j: (b, 0, 0)),
            pl.BlockSpec((1, D_MODEL), lambda b, i, j: (0, 0)),
            pl.BlockSpec((D_MODEL, IN_TILE), lambda b, i, j: (0, j)),
        ],
        out_specs=pl.BlockSpec((1, tm, IN_TILE), lambda b, i, j: (b, i, j)),
        out_shape=jax.ShapeDtypeStruct((b_sz, n, IN_COLS_PAD), f32),
        scratch_shapes=[pltpu.VMEM((tm, D_MODEL), bf16)],
        compiler_params=_params(("parallel", "parallel", "arbitrary")),
        name="inproj",
    )(x, mod, g, w_in)


def _fourier_kernel(u_ref, ccs_ref, cn_ref, sn_ref, o_ref, ucs_scr):
    @pl.when(pl.program_id(1) == 0)
    def _():
        ucs_scr[...] = _dot(u_ref[0].astype(bf16), ccs_ref[...]).astype(bf16)

    y = _dot(cn_ref[...], ucs_scr[:, :FN_WIDTH]) + _dot(sn_ref[...], ucs_scr[:, FN_WIDTH:])
    o_ref[0] = y.astype(bf16)


def _fourier(proj, ccs, cn, sn_neg, tr):
    b_sz, n, _ = proj.shape
    return pl.pallas_call(
        _fourier_kernel,
        grid=(b_sz, n // tr),
        in_specs=[
            pl.BlockSpec((1, n, FN_WIDTH), lambda b, i: (b, 0, 0)),
            pl.BlockSpec((FN_WIDTH, 2 * FN_WIDTH), lambda b, i: (0, 0)),
            pl.BlockSpec((tr, n), lambda b, i: (i, 0)),
            pl.BlockSpec((tr, n), lambda b, i: (i, 0)),
        ],
        out_specs=pl.BlockSpec((1, tr, FN_WIDTH), lambda b, i: (b, i, 0)),
        out_shape=jax.ShapeDtypeStruct((b_sz, n, FN_WIDTH), bf16),
        scratch_shapes=[pltpu.VMEM((n, 2 * FN_WIDTH), bf16)],
        compiler_params=_params(("parallel", "arbitrary")),
        name="fourier",
    )(proj, ccs, cn, sn_neg)


def _dft_tables(n):
    j = jnp.arange(n, dtype=jnp.int32)
    ang = ((j[:, None] * j[None, :]) % n).astype(f32) * (2.0 * np.pi / n)
    scale_n = 1.0 / np.sqrt(n)
    cn = (jnp.cos(ang) * scale_n).astype(bf16)
    sn_neg = (-jnp.sin(ang) * scale_n).astype(bf16)
    return cn, sn_neg


def _channel_tables():
    k = np.arange(FN_DH)
    ang = 2.0 * np.pi * ((k[:, None] * k[None, :]) % FN_DH) / FN_DH
    eye = np.eye(FN_HEADS)
    cc = np.kron(eye, np.cos(ang)) / np.sqrt(FN_DH)
    sc = np.kron(eye, np.sin(ang)) / np.sqrt(FN_DH)
    return jnp.asarray(np.concatenate([cc, sc], axis=1), dtype=bf16)


def _hgrn_tables():
    c = HG_CHUNK
    idx = np.arange(c)
    blocks = [(idx[None, :] <= idx[:, None])]
    masks = []
    for h in HG_LEVELS:
        mid = (idx // (2 * h)) * (2 * h) + h
        upper = idx >= mid
        if h < SUBLANES:
            row_up = (idx[None, :] >= mid[:, None]) & (idx[None, :] <= idx[:, None])
            row_lo = (idx[None, :] > idx[:, None]) & (idx[None, :] < mid[:, None])
            blocks.append(np.where(upper[:, None], row_up, row_lo))
        same = (idx[:, None] // (2 * h)) == (idx[None, :] // (2 * h))
        masks.append(same & upper[:, None] & ~upper[None, :])
    masks.append(np.eye(c, dtype=bool))
    a_f = np.concatenate([b.astype(np.float32) for b in blocks], axis=0)
    a_b = np.concatenate([b[::-1, ::-1].astype(np.float32) for b in blocks], axis=0)
    m_f = np.stack([m.astype(np.float32) for m in masks])
    m_b = np.stack([m[::-1, ::-1].astype(np.float32) for m in masks])
    rep = lambda a: jnp.asarray(np.concatenate([a] * HG_SPLIT, axis=1), dtype=bf16)
    return rep(a_f), rep(a_b), jnp.asarray(m_f), jnp.asarray(m_b)


def _block_row(x, h, row):
    c, w = x.shape
    xb = x.reshape(c // (2 * h), 2 * h, w)
    return jnp.broadcast_to(xb[:, row:row + 1, :], xb.shape).reshape(c, w)


def _hgrn_direction(q, v, pre, lb, a_ref, m_ref, states, backward):
    c = HG_CHUNK
    nl = len(HG_LEVELS)
    f = lb + (1.0 - lb) * jax.nn.sigmoid(pre)
    lf = jnp.log(jnp.maximum(f, F_FLOOR))
    kk = 1.0 - f
    pieces = []
    rem = lf
    for _ in range(HG_SPLIT):
        piece = rem.astype(bf16)
        pieces.append(piece)
        rem = rem - piece.astype(f32)
    sums = _dot(a_ref[...], jnp.concatenate(pieces, axis=0))
    cum = sums[0:c]
    total = cum[0:1] if backward else cum[c - 1:c]
    e_cum = jnp.exp(cum)
    e_rest = jnp.exp(jnp.minimum(total - cum, 0.0))
    e_total = jnp.exp(total)
    e_levels = []
    fine = 0
    for h in HG_LEVELS:
        if h >= SUBLANES:
            ref = _block_row(cum, h, h if backward else h - 1)
            e_levels.append(jnp.exp(-jnp.abs(cum - ref)))
        else:
            fine += 1
            e_levels.append(jnp.exp(sums[fine * c:(fine + 1) * c]))
    outs, new_states = [], []
    for h in range(HG_HEADS):
        sl = slice(h * HG_DK, (h + 1) * HG_DK)
        qh = q[:, sl]
        kh = kk[:, sl]
        vh = v[:, sl].astype(bf16)
        sc = m_ref[nl] * _dot_nt(qh.astype(bf16), kh.astype(bf16))
        for li in range(nl):
            el = e_levels[li][:, sl]
            sc = sc + m_ref[li] * _dot_nt((qh * el).astype(bf16), (kh * el).astype(bf16))
        st = states[h]
        outs.append(_dot(sc.astype(bf16), vh) + _dot_nt((qh * e_cum[:, sl]).astype(bf16), st.astype(bf16)))
        new_states.append(e_total[:, sl] * st + _dot_tn(vh, (kh * e_rest[:, sl]).astype(bf16)))
    return outs, new_states


def _hgrn_kernel(layer, has_s0, has_sout, *refs):
    (qf_ref, vf_ref, ff_ref, qb_ref, vb_ref, fb_ref, lb_ref, af_ref, ab_ref, mf_ref, mb_ref) = refs[:11]
    pos = 11
    s0_ref = None
    if has_s0:
        s0_ref = refs[pos]
        pos += 1
    of_ref, ob_ref = refs[pos], refs[pos + 1]
    pos += 2
    sout_ref = None
    if has_sout:
        sout_ref = refs[pos]
        pos += 1
    st_scr = refs[pos]
    ci = pl.program_id(1)

    @pl.when(ci == 0)
    def _():
        for d in range(2):
            for h in range(HG_HEADS):
                if has_s0:
                    st_scr[d, h] = s0_ref[0, d, h].T
                else:
                    st_scr[d, h] = jnp.zeros((HG_DV, HG_DK), f32)

    raw = lb_ref[...]
    mx = raw[0]
    for i in range(1, DEPTH):
        mx = jnp.maximum(mx, raw[i])
    ex = [jnp.exp(raw[i] - mx) for i in range(DEPTH)]
    den = ex[0]
    for i in range(1, DEPTH):
        den = den + ex[i]
    lb = jnp.zeros_like(den)
    for i in range(1, layer + 1):
        lb = lb + ex[i] / den

    c = HG_CHUNK
    st_f = [st_scr[0, h] for h in range(HG_HEADS)]
    st_b = [st_scr[1, h] for h in range(HG_HEADS)]
    o_f, st_f = _hgrn_direction(qf_ref[0], vf_ref[0], ff_ref[0], lb[0:1], af_ref, mf_ref, st_f, False)
    o_b, st_b = _hgrn_direction(qb_ref[0], vb_ref[0], fb_ref[0], lb[1:2], ab_ref, mb_ref, st_b, True)
    of_ref[0] = jnp.concatenate(o_f, axis=1)
    ob_ref[0] = jnp.concatenate(o_b, axis=1)
    for h in range(HG_HEADS):
        st_scr[0, h] = st_f[h]
        st_scr[1, h] = st_b[h]

    if has_sout:
        @pl.when(ci == pl.num_programs(1) - 1)
        def _():
            for d in range(2):
                for h in range(HG_HEADS):
                    sout_ref[0, d, h] = st_scr[d, h].T


def _hgrn(proj, hg_lb, tables, layer, s0, want_state):
    b_sz, n, _ = proj.shape
    nc = n // HG_CHUNK
    a_f, a_b, m_f, m_b = tables
    c = HG_CHUNK
    fwd = lambda col: pl.BlockSpec((1, c, HG_QK), lambda b, i: (b, i, col))
    bwd = lambda col: pl.BlockSpec((1, c, HG_QK), lambda b, i: (b, nc - 1 - i, col))
    const = lambda shape: pl.BlockSpec(shape, lambda b, i: (0,) * len(shape))
    in_specs = [fwd(COL_HQ), fwd(COL_HV), fwd(COL_HFF), bwd(COL_HQ), bwd(COL_HV), bwd(COL_HFB),
                const(hg_lb.shape), const(a_f.shape), const(a_b.shape), const(m_f.shape), const(m_b.shape)]
    args = [proj, proj, proj, proj, proj, proj, hg_lb, a_f, a_b, m_f, m_b]
    state_spec = pl.BlockSpec((1, 2, HG_HEADS, HG_DK, HG_DV), lambda b, i: (b, 0, 0, 0, 0))
    if s0 is not None:
        in_specs.append(state_spec)
        args.append(s0)
    out_specs = [pl.BlockSpec((1, c, HG_WIDTH), lambda b, i: (b, i, 0)),
                 pl.BlockSpec((1, c, HG_WIDTH), lambda b, i: (b, nc - 1 - i, 0))]
    out_shape = [jax.ShapeDtypeStruct((b_sz, n, HG_WIDTH), f32)] * 2
    if want_state:
        out_specs.append(state_spec)
        out_shape.append(jax.ShapeDtypeStruct((b_sz, 2, HG_HEADS, HG_DK, HG_DV), f32))
    return pl.pallas_call(
        functools.partial(_hgrn_kernel, layer, s0 is not None, want_state),
        grid=(b_sz, nc),
        in_specs=in_specs,
        out_specs=out_specs,
        out_shape=out_shape,
        scratch_shapes=[pltpu.VMEM((2, HG_HEADS, HG_DV, HG_DK), f32)],
        compiler_params=_params(("parallel", "arbitrary")),
        name="hgrn",
    )(*args)


def _rope_rotate(x, cos, sin):
    lane = lax.broadcasted_iota(jnp.int32, x.shape, 1)
    first_half = (lane % (MLA_ROPE // 2)) < (MLA_ROPE // 4)
    quarter = MLA_ROPE // 4
    rot = jnp.where(first_half, -pltpu.roll(x, LANES - quarter, 1), pltpu.roll(x, quarter, 1))
    return x * cos + rot * sin


def _mla_kernel(n_ctx, use_rope, want_cache, tq, *refs):
    qa_ref, kvr_ref, wq_ref, wkv_ref, qn_ref, kvn_ref = refs[:6]
    pos = 6
    if n_ctx:
        cckv_ref, ckr_ref = refs[pos], refs[pos + 1]
        pos += 2
    if use_rope:
        cosq_ref, sinq_ref, cosk_ref, sink_ref = refs[pos:pos + 4]
        pos += 4
    y_ref = refs[pos]
    pos += 1
    if want_cache:
        ockv_ref, okr_ref = refs[pos], refs[pos + 1]
        pos += 2
    kcat_scr, v_scr, o_scr = refs[pos:pos + 3]
    n = kvr_ref.shape[1]
    rows = 512 if n % 512 == 0 else n

    def expand(r0, nrows, c_kv, krp):
        kv = _dot(c_kv, wkv_ref[...])
        for h in range(MLA_HEADS):
            base = h * (MLA_NOPE + MLA_V)
            kcat_scr[h, r0:r0 + nrows, 0:MLA_NOPE] = kv[:, base:base + MLA_NOPE].astype(bf16)
            kcat_scr[h, r0:r0 + nrows, MLA_NOPE:QHEAD_PAD] = krp
            v_scr[h, r0:r0 + nrows] = kv[:, base + MLA_NOPE:base + MLA_NOPE + MLA_V].astype(bf16)

    @pl.when(pl.program_id(1) == 0)
    def _():
        if n_ctx:
            zeros = jnp.zeros((n_ctx, LANES - MLA_ROPE), f32)
            krp_ctx = jnp.concatenate([ckr_ref[0, 0], zeros], axis=1).astype(bf16)
            expand(0, n_ctx, cckv_ref[0, 0].astype(bf16), krp_ctx)
        for r0 in range(0, n, rows):
            blk = kvr_ref[0, r0:r0 + rows]
            c_kv = _rms(blk[:, :KV_LORA]) * kvn_ref[...]
            krp = blk[:, KV_LORA:]
            if want_cache:
                ockv_ref[0, r0:r0 + rows] = c_kv
                okr_ref[0, r0:r0 + rows] = krp[:, :MLA_ROPE]
            if use_rope:
                krp = _rope_rotate(krp, cosk_ref[r0:r0 + rows], sink_ref[r0:r0 + rows])
            expand(n_ctx + r0, rows, c_kv.astype(bf16), krp.astype(bf16))

    scale = float(MLA_NOPE + MLA_ROPE) ** -0.5
    qn = (_rms(qa_ref[0]) * qn_ref[...]).astype(bf16)
    sk = kcat_scr.shape[1]
    n_kc = 2 if sk > MLA_KEY_CHUNK else 1
    kc = sk // n_kc

    def head(h, carry):
        q = _dot(qn, wq_ref[h]) * scale
        q_rope = q[:, MLA_NOPE:]
        if use_rope:
            q_rope = _rope_rotate(q_rope, cosq_ref[...], sinq_ref[...])
        q_cat = jnp.concatenate([q[:, :MLA_NOPE], q_rope], axis=1).astype(bf16)
        m_run = denom = o = None
        for c in range(n_kc):
            s = _dot_nt(q_cat, kcat_scr[h, c * kc:(c + 1) * kc])
            m_c = jnp.max(s, axis=-1, keepdims=True)
            m_new = m_c if c == 0 else jnp.maximum(m_run, m_c)
            p = jnp.exp(s - m_new)
            pv = _dot(p.astype(bf16), v_scr[h, c * kc:(c + 1) * kc])
            if c == 0:
                denom, o = jnp.sum(p, axis=-1, keepdims=True), pv
            else:
                alpha = jnp.exp(m_run - m_new)
                denom = alpha * denom + jnp.sum(p, axis=-1, keepdims=True)
                o = alpha * o + pv
            m_run = m_new
        o_scr[h] = (o / denom).astype(bf16)
        return carry

    lax.fori_loop(0, MLA_HEADS, head, 0, unroll=2)
    for h in range(MLA_HEADS):
        y_ref[0, :, h * MLA_V:(h + 1) * MLA_V] = o_scr[h]


def _mla(proj, wq, wkv, q_norm, kv_norm, ctx, layer, rope, want_cache, tq):
    b_sz, n, _ = proj.shape
    n_ctx = 0 if ctx is None else ctx[0].shape[2]
    sk = n_ctx + n
    single = pl.Buffered(1)
    const = lambda shape: pl.BlockSpec(shape, lambda b, i: (0,) * len(shape), pipeline_mode=single)
    in_specs = [
        pl.BlockSpec((1, tq, Q_LORA), lambda b, i: (b, i, COL_QA)),
        pl.BlockSpec((1, n, KVR_WIDTH), lambda b, i: (b, 0, COL_KVR), pipeline_mode=single),
        const(wq.shape), const(wkv.shape), const(q_norm.shape), const(kv_norm.shape),
    ]
    args = [proj, proj, wq, wkv, q_norm, kv_norm]
    if ctx is not None:
        in_specs += [
            pl.BlockSpec((1, 1, n_ctx, KV_LORA), lambda b, i: (b, layer, 0, 0), pipeline_mode=single),
            pl.BlockSpec((1, 1, n_ctx, MLA_ROPE), lambda b, i: (b, layer, 0, 0), pipeline_mode=single)]
        args += [ctx[0], ctx[1]]
    if rope is not None:
        cos, sin = rope
        in_specs += [pl.BlockSpec((tq, LANES), lambda b, i: (i, 0)),
                     pl.BlockSpec((tq, LANES), lambda b, i: (i, 0)),
                     const(cos.shape), const(sin.shape)]
        args += [cos, sin, cos, sin]
    out_specs = [pl.BlockSpec((1, tq, MLA_WIDTH), lambda b, i: (b, i, 0))]
    out_shape = [jax.ShapeDtypeStruct((b_sz, n, MLA_WIDTH), bf16)]
    if want_cache:
        out_specs += [pl.BlockSpec((1, n, KV_LORA), lambda b, i: (b, 0, 0)),
                      pl.BlockSpec((1, n, MLA_ROPE), lambda b, i: (b, 0, 0))]
        out_shape += [jax.ShapeDtypeStruct((b_sz, n, KV_LORA), f32),
                      jax.ShapeDtypeStruct((b_sz, n, MLA_ROPE), f32)]
    return pl.pallas_call(
        functools.partial(_mla_kernel, n_ctx, rope is not None, want_cache, tq),
        grid=(b_sz, n // tq),
        in_specs=in_specs,
        out_specs=out_specs,
        out_shape=out_shape,
        scratch_shapes=[pltpu.VMEM((MLA_HEADS, sk, QHEAD_PAD), bf16), pltpu.VMEM((MLA_HEADS, sk, MLA_V), bf16),
                        pltpu.VMEM((MLA_HEADS, tq, MLA_V), bf16)],
        compiler_params=_params(("parallel", "arbitrary")),
        name="mla",
    )(*args)


def _rope_tables(n):
    t = jnp.arange(n)
    r = (t // GRID_W).astype(f32)
    col = (t % GRID_W).astype(f32)
    nf = MLA_ROPE // 4
    inv = ROPE_BASE ** (-jnp.arange(nf, dtype=f32) / nf)
    ar = r[:, None] * inv
    ac = col[:, None] * inv
    ang = jnp.concatenate([ar, ar, ac, ac], axis=-1)
    pad = LANES - MLA_ROPE
    cos = jnp.concatenate([jnp.cos(ang), jnp.ones((n, pad), f32)], axis=-1)
    sin = jnp.concatenate([jnp.sin(ang), jnp.zeros((n, pad), f32)], axis=-1)
    return cos, sin


def _outproj_kernel(x_ref, yfn_ref, of_ref, ob_ref, gate_ref, ymla_ref, w_ref, hgg_ref, gpost_ref,
                    mod_ref, o_ref):
    o_hg = of_ref[0] + ob_ref[0]
    gate = gate_ref[0]
    gate = gate * jax.nn.sigmoid(gate)
    parts = []
    for h in range(HG_HEADS):
        sl = slice(h * HG_DV, (h + 1) * HG_DV)
        parts.append((_rms(o_hg[:, sl]) * hgg_ref[:, sl] * gate[:, sl]).astype(bf16))
    y = _dot(yfn_ref[0], w_ref[0:FN_WIDTH])
    for h in range(HG_HEADS):
        r0 = FN_WIDTH + h * HG_DV
        y = y + _dot(parts[h], w_ref[r0:r0 + HG_DV])
    y = y + _dot(ymla_ref[0], w_ref[FN_WIDTH + HG_WIDTH:])
    o_ref[0] = x_ref[0] + mod_ref[0, 2:3, :] * (_rms(y) * gpost_ref[...])


def _outproj(x, y_fn, o_f, o_b, proj, y_mla, w_out, hg_gain, g_post, mod, tm):
    b_sz, n, _ = x.shape
    tok = lambda width, col=0: pl.BlockSpec((1, tm, width), lambda b, i: (b, i, col))
    return pl.pallas_call(
        _outproj_kernel,
        grid=(b_sz, n // tm),
        in_specs=[
            tok(D_MODEL), tok(FN_WIDTH), tok(HG_WIDTH), tok(HG_WIDTH), tok(HG_WIDTH, COL_HGATE),
            tok(MLA_WIDTH),
            pl.BlockSpec((D_MODEL, D_MODEL), lambda b, i: (0, 0)),
            pl.BlockSpec((1, HG_WIDTH), lambda b, i: (0, 0)),
            pl.BlockSpec((1, D_MODEL), lambda b, i: (0, 0)),
            pl.BlockSpec((1, 6, D_MODEL), lambda b, i: (b, 0, 0)),
        ],
        out_specs=tok(D_MODEL),
        out_shape=jax.ShapeDtypeStruct((b_sz, n, D_MODEL), f32),
        compiler_params=_params(("parallel", "parallel")),
        name="outproj",
    )(x, y_fn, o_f, o_b, proj, y_mla, w_out, hg_gain, g_post, mod)


def _ffn_kernel(x_ref, mod_ref, gpre_ref, gpost_ref, w1_ref, w2_ref, o_ref, h_scr):
    k = pl.program_id(2)

    @pl.when(k == 0)
    def _():
        y = _rms(x_ref[0]) * gpre_ref[...]
        h_scr[...] = (y * (1.0 + mod_ref[0, 4:5, :]) + mod_ref[0, 3:4, :]).astype(bf16)
        o_ref[...] = jnp.zeros_like(o_ref)

    a = jnp.maximum(_dot(h_scr[...], w1_ref[...]), 0.0)
    o_ref[0] += _dot((a * a).astype(bf16), w2_ref[...])

    @pl.when(k == pl.num_programs(2) - 1)
    def _():
        o_ref[0] = x_ref[0] + mod_ref[0, 5:6, :] * (_rms(o_ref[0]) * gpost_ref[...])


def _ffn(x, mod, g_pre, g_post, w1, w2, tm, tf):
    b_sz, n, _ = x.shape
    return pl.pallas_call(
        _ffn_kernel,
        grid=(b_sz, n // tm, D_FF // tf),
        in_specs=[
            pl.BlockSpec((1, tm, D_MODEL), lambda b, i, k: (b, i, 0)),
            pl.BlockSpec((1, 6, D_MODEL), lambda b, i, k: (b, 0, 0)),
            pl.BlockSpec((1, D_MODEL), lambda b, i, k: (0, 0)),
            pl.BlockSpec((1, D_MODEL), lambda b, i, k: (0, 0)),
            pl.BlockSpec((D_MODEL, tf), lambda b, i, k: (0, k)),
            pl.BlockSpec((tf, D_MODEL), lambda b, i, k: (k, 0)),
        ],
        out_specs=pl.BlockSpec((1, tm, D_MODEL), lambda b, i, k: (b, i, 0)),
        out_shape=jax.ShapeDtypeStruct((b_sz, n, D_MODEL), f32),
        scratch_shapes=[pltpu.VMEM((tm, D_MODEL), bf16)],
        compiler_params=_params(("parallel", "parallel", "arbitrary")),
        name="ffn",
    )(x, mod, g_pre, g_post, w1, w2)


def _layer(x, mod, wl, hg_tables, dft, layer, n_seq, ctx, s0, rope, is_context, tm, tq):
    b_sz, n, _ = x.shape
    n_seqs = b_sz * n // n_seq
    proj = _inproj(x, mod, wl["g_pre_mix"], wl["w_in"], tm).reshape(n_seqs, n_seq, IN_COLS_PAD)
    y_fn = _fourier(proj, dft[0], dft[1], dft[2], min(n_seq, 512))
    hg = _hgrn(proj, wl["hg_lb"], hg_tables, layer, s0, is_context)
    mla = _mla(proj, wl["wq"], wl["wkv"], wl["q_norm"], wl["kv_norm"], ctx, layer, rope, is_context, tq)
    flat = lambda a: a.reshape(b_sz, n, a.shape[-1])
    x = _outproj(x, flat(y_fn), flat(hg[0]), flat(hg[1]), flat(proj), flat(mla[0]), wl["w_out"],
                 wl["hg_gain"], wl["g_post_mix"], mod, tm)
    x = _ffn(x, mod, wl["g_pre_ff"], wl["g_post_ff"], wl["w_ff1"], wl["w_ff2"], FFN_TM, FFN_TF)
    extras = (mla[1], mla[2], hg[2]) if is_context else None
    return x, extras


def kernel(x_prompt, x_sample, c, cache_ckv, cache_krope, state_hgrn, c_ctx, w_ada, b_ada, g_pre_mix,
           g_post_mix, g_pre_ff, g_post_ff, w_in, hg_lb, hg_gain, mla_q_norm, mla_kv_norm, w_q_b, w_kv_b,
           w_out, w_ff1, w_ff2):
    batch, seq, _ = x_prompt.shape
    dec_batch, dec_seq, _ = x_sample.shape

    rows = 16
    cond = jnp.concatenate([c_ctx[None, :], c, jnp.zeros((rows - 1 - dec_batch, D_MODEL), f32)], axis=0)
    mod = _ada(cond, w_ada, b_ada).reshape(DEPTH, rows, 6, D_MODEL)

    hg_tables = _hgrn_tables()
    ccs = _channel_tables()
    dft_p = (ccs,) + _dft_tables(seq)
    dft_s = (ccs,) + _dft_tables(dec_seq)
    rope = _rope_tables(dec_seq)

    yp = x_prompt.reshape(1, batch * seq, D_MODEL)
    ys = x_sample
    ckv_list, kr_list, st_list = [], [], []
    for l in range(DEPTH):
        wq = w_q_b[l].reshape(Q_LORA, MLA_HEADS, MLA_NOPE + MLA_ROPE)
        wq = jnp.pad(wq, ((0, 0), (0, 0), (0, QHEAD_PAD - MLA_NOPE - MLA_ROPE)))
        wl = {
            "w_in": jnp.pad(w_in[l].astype(bf16), ((0, 0), (0, IN_COLS_PAD - IN_COLS))),
            "wq": wq.transpose(1, 0, 2).astype(bf16),
            "wkv": w_kv_b[l].astype(bf16),
            "w_out": w_out[l].astype(bf16),
            "w_ff1": w_ff1[l].astype(bf16),
            "w_ff2": w_ff2[l].astype(bf16),
            "g_pre_mix": g_pre_mix[l][None, :], "g_post_mix": g_post_mix[l][None, :],
            "g_pre_ff": g_pre_ff[l][None, :], "g_post_ff": g_post_ff[l][None, :],
            "hg_lb": hg_lb, "hg_gain": hg_gain[l][None, :],
            "q_norm": mla_q_norm[l][None, :], "kv_norm": mla_kv_norm[l][None, :],
        }
        yp, (ckv_l, kr_l, st_l) = _layer(yp, mod[l, 0:1], wl, hg_tables, dft_p, l, seq, None, None, None,
                                         True, 512, seq)
        ckv_list.append(ckv_l)
        kr_list.append(kr_l)
        st_list.append(st_l)
        ys, _ = _layer(ys, mod[l, 1:1 + dec_batch], wl, hg_tables, dft_s, l, dec_seq,
                       (cache_ckv, cache_krope), state_hgrn[:, l], rope, False, 512, 512)

    return (yp.reshape(batch, seq, D_MODEL), ys, jnp.stack(ckv_list, axis=1), jnp.stack(kr_list, axis=1),
            jnp.stack(st_list, axis=1))
```

```python
import functools

import numpy as np
import jax
import jax.numpy as jnp
from jax import lax
from jax.experimental import pallas as pl
from jax.experimental.pallas import tpu as pltpu

f32 = jnp.float32
bf16 = jnp.bfloat16

D_MODEL = 2048
DEPTH = 2
GRID_W = 64
FN_HEADS = 4
FN_DH = 128
FN_WIDTH = FN_HEADS * FN_DH
HG_HEADS = 4
HG_DK = 128
HG_DV = 128
HG_QK = HG_HEADS * HG_DK
HG_WIDTH = HG_HEADS * HG_DV
MLA_HEADS = 8
MLA_NOPE = 128
MLA_ROPE = 64
MLA_V = 128
Q_LORA = 768
KV_LORA = 512
MLA_WIDTH = MLA_HEADS * MLA_V
D_FF = 4 * D_MODEL
ROPE_BASE = 10000.0
EPS = 1e-6
F_FLOOR = 1e-30

LANES = 128
SUBLANES = 8
MXU_DIM = 256
IN_COLS = 4416
IN_COLS_PAD = 4608
IN_TILE = 1536
COL_HQ, COL_HV, COL_HFF, COL_HFB, COL_HGATE = 1, 2, 3, 4, 5
COL_QA = 4
KVR_WIDTH = 640
COL_KVR = 6
QHEAD_PAD = 256
HG_CHUNK = 128
HG_LEVELS = (64, 32, 16, 8, 4, 2, 1)
HG_SPLIT = 2
MLA_KEY_CHUNK = 1280
FFN_TM = 1024
FFN_TF = 512
VMEM_LIMIT = 56 * 1024 * 1024


def _params(semantics):
    return pltpu.CompilerParams(dimension_semantics=semantics, vmem_limit_bytes=VMEM_LIMIT)


def _dot(a, b):
    return jnp.dot(a, b, preferred_element_type=f32)


def _dot_nt(a, b):
    return lax.dot_general(a, b, (((1,), (1,)), ((), ())), preferred_element_type=f32)


def _dot_tn(a, b):
    return lax.dot_general(a, b, (((0,), (0,)), ((), ())), preferred_element_type=f32)


def _rms(x):
    return x * lax.rsqrt(jnp.mean(x * x, axis=-1, keepdims=True) + EPS)


def _ada_kernel(cond_ref, w_ref, b_ref, o_ref):
    cnd = cond_ref[...]
    act = cnd * jax.nn.sigmoid(cnd)
    o_ref[0] = _dot(act.astype(bf16), w_ref[0].astype(bf16)) + b_ref[0]


def _ada(cond, w_ada, b_ada):
    rows = cond.shape[0]
    tn = 1024
    return pl.pallas_call(
        _ada_kernel,
        grid=(DEPTH, 6 * D_MODEL // tn),
        in_specs=[
            pl.BlockSpec((rows, D_MODEL), lambda l, j: (0, 0)),
            pl.BlockSpec((1, D_MODEL, tn), lambda l, j: (l, 0, j)),
            pl.BlockSpec((1, 1, tn), lambda l, j: (l, 0, j)),
        ],
        out_specs=pl.BlockSpec((1, rows, tn), lambda l, j: (l, 0, j)),
        out_shape=jax.ShapeDtypeStruct((DEPTH, rows, 6 * D_MODEL), f32),
        compiler_params=_params(("parallel", "parallel")),
        name="ada_mod",
    )(cond, w_ada, b_ada.reshape(DEPTH, 1, 6 * D_MODEL))


def _inproj_kernel(x_ref, mod_ref, g_ref, w_ref, o_ref, h_scr):
    @pl.when(pl.program_id(2) == 0)
    def _():
        y = _rms(x_ref[0]) * g_ref[...]
        h_scr[...] = (y * (1.0 + mod_ref[0, 1:2, :]) + mod_ref[0, 0:1, :]).astype(bf16)

    o_ref[0] = _dot(h_scr[...], w_ref[0])


def _inproj(x, mod, g, w_in, layer, tm):
    b_sz, n, _ = x.shape
    return pl.pallas_call(
        _inproj_kernel,
        grid=(b_sz, n // tm, IN_COLS_PAD // IN_TILE),
        in_specs=[
            pl.BlockSpec((1, tm, D_MODEL), lambda b, i, j: (b, i, 0)),
            pl.BlockSpec((1, 6, D_MODEL), lambda b, i, j: (b, 0, 0)),
            pl.BlockSpec((1, D_MODEL), lambda b, i, j: (0, 0)),
            pl.BlockSpec((1, D_MODEL, IN_TILE), lambda b, i, j: (layer, 0, j)),
        ],
        out_specs=pl.BlockSpec((1, tm, IN_TILE), lambda b, i, j: (b, i, j)),
        out_shape=jax.ShapeDtypeStruct((b_sz, n, IN_COLS_PAD), f32),
        scratch_shapes=[pltpu.VMEM((tm, D_MODEL), bf16)],
        compiler_params=_params(("parallel", "parallel", "arbitrary")),
        name="inproj",
    )(x, mod, g, w_in)


def _fourier_kernel(u_ref, ccs_ref, cn_ref, sn_ref, o_ref, ucs_scr):
    @pl.when(pl.program_id(1) == 0)
    def _():
        ucs_scr[...] = _dot(u_ref[0].astype(bf16), ccs_ref[...]).astype(bf16)

    y = _dot(cn_ref[...], ucs_scr[:, :FN_WIDTH]) + _dot(sn_ref[...], ucs_scr[:, FN_WIDTH:])
    o_ref[0] = y.astype(bf16)


def _fourier(proj, ccs, cn, sn_neg, tr):
    b_sz, n, _ = proj.shape
    return pl.pallas_call(
        _fourier_kernel,
        grid=(b_sz, n // tr),
        in_specs=[
            pl.BlockSpec((1, n, FN_WIDTH), lambda b, i: (b, 0, 0)),
            pl.BlockSpec((FN_WIDTH, 2 * FN_WIDTH), lambda b, i: (0, 0)),
            pl.BlockSpec((tr, n), lambda b, i: (i, 0)),
            pl.BlockSpec((tr, n), lambda b, i: (i, 0)),
        ],
        out_specs=pl.BlockSpec((1, tr, FN_WIDTH), lambda b, i: (b, i, 0)),
        out_shape=jax.ShapeDtypeStruct((b_sz, n, FN_WIDTH), bf16),
        scratch_shapes=[pltpu.VMEM((n, 2 * FN_WIDTH), bf16)],
        compiler_params=_params(("parallel", "arbitrary")),
        name="fourier",
    )(proj, ccs, cn, sn_neg)


def _dft_tables(n):
    j = jnp.arange(n, dtype=jnp.int32)
    ang = ((j[:, None] * j[None, :]) % n).astype(f32) * (2.0 * np.pi / n)
    scale_n = 1.0 / np.sqrt(n)
    cn = (jnp.cos(ang) * scale_n).astype(bf16)
    sn_neg = (-jnp.sin(ang) * scale_n).astype(bf16)
    return cn, sn_neg


def _channel_tables():
    k = np.arange(FN_DH)
    ang = 2.0 * np.pi * ((k[:, None] * k[None, :]) % FN_DH) / FN_DH
    eye = np.eye(FN_HEADS)
    cc = np.kron(eye, np.cos(ang)) / np.sqrt(FN_DH)
    sc = np.kron(eye, np.sin(ang)) / np.sqrt(FN_DH)
    return jnp.asarray(np.concatenate([cc, sc], axis=1), dtype=bf16)


def _hgrn_tables():
    c = HG_CHUNK
    idx = np.arange(c)
    blocks = [(idx[None, :] <= idx[:, None])]
    masks = []
    for h in HG_LEVELS:
        mid = (idx // (2 * h)) * (2 * h) + h
        upper = idx >= mid
        if h < SUBLANES:
            row_up = (idx[None, :] >= mid[:, None]) & (idx[None, :] <= idx[:, None])
            row_lo = (idx[None, :] > idx[:, None]) & (idx[None, :] < mid[:, None])
            blocks.append(np.where(upper[:, None], row_up, row_lo))
        same = (idx[:, None] // (2 * h)) == (idx[None, :] // (2 * h))
        masks.append(same & upper[:, None] & ~upper[None, :])
    masks.append(np.eye(c, dtype=bool))
    a_f = np.concatenate([b.astype(np.float32) for b in blocks], axis=0)
    a_b = np.concatenate([b[::-1, ::-1].astype(np.float32) for b in blocks], axis=0)
    m_f = np.stack([m.astype(np.float32) for m in masks])
    m_b = np.stack([m[::-1, ::-1].astype(np.float32) for m in masks])
    rep = lambda a: jnp.asarray(np.concatenate([a] * HG_SPLIT, axis=1), dtype=bf16)
    return rep(a_f), rep(a_b), jnp.asarray(m_f), jnp.asarray(m_b)


def _block_row(x, h, row):
    c, w = x.shape
    xb = x.reshape(c // (2 * h), 2 * h, w)
    return jnp.broadcast_to(xb[:, row:row + 1, :], xb.shape).reshape(c, w)


def _hgrn_direction(q, v, pre, lb, a_ref, m_ref, states, backward):
    c = HG_CHUNK
    nl = len(HG_LEVELS)
    f = lb + (1.0 - lb) * jax.nn.sigmoid(pre)
    lf = jnp.log(jnp.maximum(f, F_FLOOR))
    kk = 1.0 - f
    pieces = []
    rem = lf
    for _ in range(HG_SPLIT):
        piece = rem.astype(bf16)
        pieces.append(piece)
        rem = rem - piece.astype(f32)
    sums = _dot(a_ref[...], jnp.concatenate(pieces, axis=0))
    cum = sums[0:c]
    total = cum[0:1] if backward else cum[c - 1:c]
    e_cum = jnp.exp(cum)
    e_rest = jnp.exp(jnp.minimum(total - cum, 0.0))
    e_total = jnp.exp(total)
    e_levels = []
    fine = 0
    for h in HG_LEVELS:
        if h >= SUBLANES:
            ref = _block_row(cum, h, h if backward else h - 1)
            e_levels.append(jnp.exp(-jnp.abs(cum - ref)))
        else:
            fine += 1
            e_levels.append(jnp.exp(sums[fine * c:(fine + 1) * c]))
    outs, new_states = [], []
    for h in range(HG_HEADS):
        sl = slice(h * HG_DK, (h + 1) * HG_DK)
        qh = q[:, sl]
        kh = kk[:, sl]
        vh = v[:, sl].astype(bf16)
        sc = m_ref[nl] * _dot_nt(qh.astype(bf16), kh.astype(bf16))
        for li in range(nl):
            el = e_levels[li][:, sl]
            sc = sc + m_ref[li] * _dot_nt((qh * el).astype(bf16), (kh * el).astype(bf16))
        st = states[h]
        outs.append(_dot(sc.astype(bf16), vh) + _dot_nt((qh * e_cum[:, sl]).astype(bf16), st.astype(bf16)))
        new_states.append(e_total[:, sl] * st + _dot_tn(vh, (kh * e_rest[:, sl]).astype(bf16)))
    return outs, new_states


def _hgrn_kernel(layer, has_s0, has_sout, *refs):
    (qf_ref, vf_ref, ff_ref, qb_ref, vb_ref, fb_ref, lb_ref, af_ref, ab_ref, mf_ref, mb_ref) = refs[:11]
    pos = 11
    s0_ref = None
    if has_s0:
        s0_ref = refs[pos]
        pos += 1
    of_ref, ob_ref = refs[pos], refs[pos + 1]
    pos += 2
    sout_ref = None
    if has_sout:
        sout_ref = refs[pos]
        pos += 1
    st_scr = refs[pos]
    ci = pl.program_id(1)

    @pl.when(ci == 0)
    def _():
        for d in range(2):
            for h in range(HG_HEADS):
                if has_s0:
                    st_scr[d, h] = s0_ref[0, 0, d, h].T
                else:
                    st_scr[d, h] = jnp.zeros((HG_DV, HG_DK), f32)

    raw = lb_ref[...]
    mx = raw[0]
    for i in range(1, DEPTH):
        mx = jnp.maximum(mx, raw[i])
    ex = [jnp.exp(raw[i] - mx) for i in range(DEPTH)]
    den = ex[0]
    for i in range(1, DEPTH):
        den = den + ex[i]
    lb = jnp.zeros_like(den)
    for i in range(1, layer + 1):
        lb = lb + ex[i] / den

    c = HG_CHUNK
    st_f = [st_scr[0, h] for h in range(HG_HEADS)]
    st_b = [st_scr[1, h] for h in range(HG_HEADS)]
    o_f, st_f = _hgrn_direction(qf_ref[0], vf_ref[0], ff_ref[0], lb[0:1], af_ref, mf_ref, st_f, False)
    o_b, st_b = _hgrn_direction(qb_ref[0], vb_ref[0], fb_ref[0], lb[1:2], ab_ref, mb_ref, st_b, True)
    of_ref[0] = jnp.concatenate(o_f, axis=1)
    ob_ref[0] = jnp.concatenate(o_b, axis=1)
    for h in range(HG_HEADS):
        st_scr[0, h] = st_f[h]
        st_scr[1, h] = st_b[h]

    if has_sout:
        @pl.when(ci == pl.num_programs(1) - 1)
        def _():
            for d in range(2):
                for h in range(HG_HEADS):
                    sout_ref[0, d, h] = st_scr[d, h].T


def _hgrn(proj, hg_lb, tables, layer, s0, want_state):
    b_sz, n, _ = proj.shape
    nc = n // HG_CHUNK
    a_f, a_b, m_f, m_b = tables
    c = HG_CHUNK
    fwd = lambda col: pl.BlockSpec((1, c, HG_QK), lambda b, i: (b, i, col))
    bwd = lambda col: pl.BlockSpec((1, c, HG_QK), lambda b, i: (b, nc - 1 - i, col))
    const = lambda shape: pl.BlockSpec(shape, lambda b, i: (0,) * len(shape))
    in_specs = [fwd(COL_HQ), fwd(COL_HV), fwd(COL_HFF), bwd(COL_HQ), bwd(COL_HV), bwd(COL_HFB),
                const(hg_lb.shape), const(a_f.shape), const(a_b.shape), const(m_f.shape), const(m_b.shape)]
    args = [proj, proj, proj, proj, proj, proj, hg_lb, a_f, a_b, m_f, m_b]
    state_spec = pl.BlockSpec((1, 2, HG_HEADS, HG_DK, HG_DV), lambda b, i: (b, 0, 0, 0, 0))
    if s0 is not None:
        in_specs.append(pl.BlockSpec((1, 1, 2, HG_HEADS, HG_DK, HG_DV), lambda b, i: (b, layer, 0, 0, 0, 0)))
        args.append(s0)
    out_specs = [pl.BlockSpec((1, c, HG_WIDTH), lambda b, i: (b, i, 0)),
                 pl.BlockSpec((1, c, HG_WIDTH), lambda b, i: (b, nc - 1 - i, 0))]
    out_shape = [jax.ShapeDtypeStruct((b_sz, n, HG_WIDTH), f32)] * 2
    if want_state:
        out_specs.append(state_spec)
        out_shape.append(jax.ShapeDtypeStruct((b_sz, 2, HG_HEADS, HG_DK, HG_DV), f32))
    return pl.pallas_call(
        functools.partial(_hgrn_kernel, layer, s0 is not None, want_state),
        grid=(b_sz, nc),
        in_specs=in_specs,
        out_specs=out_specs,
        out_shape=out_shape,
        scratch_shapes=[pltpu.VMEM((2, HG_HEADS, HG_DV, HG_DK), f32)],
        compiler_params=_params(("parallel", "arbitrary")),
        name="hgrn",
    )(*args)


def _rope_rotate(x, cos, sin):
    lane = lax.broadcasted_iota(jnp.int32, x.shape, 1)
    first_half = (lane % (MLA_ROPE // 2)) < (MLA_ROPE // 4)
    quarter = MLA_ROPE // 4
    rot = jnp.where(first_half, -pltpu.roll(x, LANES - quarter, 1), pltpu.roll(x, quarter, 1))
    return x * cos + rot * sin


def _mla_kernel(n_ctx, use_rope, want_cache, tq, *refs):
    qa_ref, kvr_ref, wq_ref, wkv_ref, qn_ref, kvn_ref = refs[:6]
    pos = 6
    if n_ctx:
        cckv_ref, ckr_ref = refs[pos], refs[pos + 1]
        pos += 2
    if use_rope:
        cosq_ref, sinq_ref, cosk_ref, sink_ref = refs[pos:pos + 4]
        pos += 4
    y_ref = refs[pos]
    pos += 1
    if want_cache:
        ockv_ref, okr_ref = refs[pos], refs[pos + 1]
        pos += 2
    kcat_scr, v_scr, o_scr = refs[pos:pos + 3]
    n = kvr_ref.shape[1]
    rows = 512 if n % 512 == 0 else n

    def expand(r0, nrows, c_kv, krp):
        kv = _dot(c_kv, wkv_ref[0])
        ones = jnp.ones((nrows, MLA_V), bf16)
        for h in range(MLA_HEADS):
            base = h * (MLA_NOPE + MLA_V)
            kcat_scr[h, r0:r0 + nrows, 0:MLA_NOPE] = kv[:, base:base + MLA_NOPE].astype(bf16)
            kcat_scr[h, r0:r0 + nrows, MLA_NOPE:QHEAD_PAD] = krp
            v_scr[h, r0:r0 + nrows, 0:MLA_V] = kv[:, base + MLA_NOPE:base + MLA_NOPE + MLA_V].astype(bf16)
            v_scr[h, r0:r0 + nrows, MLA_V:2 * MLA_V] = ones

    @pl.when(pl.program_id(1) == 0)
    def _():
        if n_ctx:
            zeros = jnp.zeros((n_ctx, LANES - MLA_ROPE), f32)
            krp_ctx = jnp.concatenate([ckr_ref[0, 0], zeros], axis=1).astype(bf16)
            expand(0, n_ctx, cckv_ref[0, 0].astype(bf16), krp_ctx)
        for r0 in range(0, n, rows):
            blk = kvr_ref[0, r0:r0 + rows]
            c_kv = _rms(blk[:, :KV_LORA]) * kvn_ref[...]
            krp = blk[:, KV_LORA:]
            if want_cache:
                ockv_ref[0, r0:r0 + rows] = c_kv
                okr_ref[0, r0:r0 + rows] = krp[:, :MLA_ROPE]
            if use_rope:
                krp = _rope_rotate(krp, cosk_ref[r0:r0 + rows], sink_ref[r0:r0 + rows])
            expand(n_ctx + r0, rows, c_kv.astype(bf16), krp.astype(bf16))

    scale = float(MLA_NOPE + MLA_ROPE) ** -0.5
    qn = (_rms(qa_ref[0]) * qn_ref[...]).astype(bf16)
    sk = kcat_scr.shape[1]
    n_kc = 2 if sk > MLA_KEY_CHUNK else 1
    kc = sk // n_kc

    def head(h):
        q = _dot(qn, wq_ref[0, h]) * scale
        q_rope = q[:, MLA_NOPE:]
        if use_rope:
            q_rope = _rope_rotate(q_rope, cosq_ref[...], sinq_ref[...])
        q_cat = jnp.concatenate([q[:, :MLA_NOPE], q_rope], axis=1).astype(bf16)
        m_run = acc = None
        for c in range(n_kc):
            s = _dot_nt(q_cat, kcat_scr[h, c * kc:(c + 1) * kc])
            m_c = jnp.max(s, axis=-1, keepdims=True)
            m_new = m_c if c == 0 else jnp.maximum(m_run, m_c)
            p = jnp.exp((s - m_new).astype(bf16))
            pv = _dot(p, v_scr[h, c * kc:(c + 1) * kc])
            acc = pv if c == 0 else jnp.exp(m_run - m_new) * acc + pv
            m_run = m_new
        return (acc[:, :MLA_V] / acc[:, MLA_V:]).astype(bf16)

    if n_kc == 1:
        for h in range(MLA_HEADS):
            y_ref[0, :, h * MLA_V:(h + 1) * MLA_V] = head(h)
    else:
        def body(h, carry):
            o_scr[h] = head(h)
            return carry

        lax.fori_loop(0, MLA_HEADS, body, 0, unroll=4)
        for h in range(MLA_HEADS):
            y_ref[0, :, h * MLA_V:(h + 1) * MLA_V] = o_scr[h]


def _mla(proj, wq, wkv, q_norm, kv_norm, ctx, layer, rope, want_cache, tq):
    b_sz, n, _ = proj.shape
    n_ctx = 0 if ctx is None else ctx[0].shape[2]
    sk = n_ctx + n
    single = pl.Buffered(1)
    const = lambda shape: pl.BlockSpec(shape, lambda b, i: (0,) * len(shape), pipeline_mode=single)
    in_specs = [
        pl.BlockSpec((1, tq, Q_LORA), lambda b, i: (b, i, COL_QA)),
        pl.BlockSpec((1, n, KVR_WIDTH), lambda b, i: (b, 0, COL_KVR), pipeline_mode=single),
        pl.BlockSpec((1,) + wq.shape[1:], lambda b, i: (layer, 0, 0, 0), pipeline_mode=single),
        pl.BlockSpec((1,) + wkv.shape[1:], lambda b, i: (layer, 0, 0), pipeline_mode=single),
        const(q_norm.shape), const(kv_norm.shape),
    ]
    args = [proj, proj, wq, wkv, q_norm, kv_norm]
    if ctx is not None:
        in_specs += [
            pl.BlockSpec((1, 1, n_ctx, KV_LORA), lambda b, i: (b, layer, 0, 0), pipeline_mode=single),
            pl.BlockSpec((1, 1, n_ctx, MLA_ROPE), lambda b, i: (b, layer, 0, 0), pipeline_mode=single)]
        args += [ctx[0], ctx[1]]
    if rope is not None:
        cos, sin = rope
        in_specs += [pl.BlockSpec((tq, LANES), lambda b, i: (i, 0)),
                     pl.BlockSpec((tq, LANES), lambda b, i: (i, 0)),
                     const(cos.shape), const(sin.shape)]
        args += [cos, sin, cos, sin]
    out_specs = [pl.BlockSpec((1, tq, MLA_WIDTH), lambda b, i: (b, i, 0))]
    out_shape = [jax.ShapeDtypeStruct((b_sz, n, MLA_WIDTH), bf16)]
    if want_cache:
        out_specs += [pl.BlockSpec((1, n, KV_LORA), lambda b, i: (b, 0, 0)),
                      pl.BlockSpec((1, n, MLA_ROPE), lambda b, i: (b, 0, 0))]
        out_shape += [jax.ShapeDtypeStruct((b_sz, n, KV_LORA), f32),
                      jax.ShapeDtypeStruct((b_sz, n, MLA_ROPE), f32)]
    return pl.pallas_call(
        functools.partial(_mla_kernel, n_ctx, rope is not None, want_cache, tq),
        grid=(b_sz, n // tq),
        in_specs=in_specs,
        out_specs=out_specs,
        out_shape=out_shape,
        scratch_shapes=[pltpu.VMEM((MLA_HEADS, sk, QHEAD_PAD), bf16), pltpu.VMEM((MLA_HEADS, sk, 2 * MLA_V), bf16),
                        pltpu.VMEM((MLA_HEADS, tq, MLA_V), bf16)],
        compiler_params=_params(("parallel", "arbitrary")),
        name="mla",
    )(*args)


def _rope_tables(n):
    t = jnp.arange(n)
    r = (t // GRID_W).astype(f32)
    col = (t % GRID_W).astype(f32)
    nf = MLA_ROPE // 4
    inv = ROPE_BASE ** (-jnp.arange(nf, dtype=f32) / nf)
    ar = r[:, None] * inv
    ac = col[:, None] * inv
    ang = jnp.concatenate([ar, ar, ac, ac], axis=-1)
    pad = LANES - MLA_ROPE
    cos = jnp.concatenate([jnp.cos(ang), jnp.ones((n, pad), f32)], axis=-1)
    sin = jnp.concatenate([jnp.sin(ang), jnp.zeros((n, pad), f32)], axis=-1)
    return cos, sin


def _outproj_kernel(x_ref, yfn_ref, of_ref, ob_ref, gate_ref, ymla_ref, w_ref, hgg_ref, gpost_ref,
                    mod_ref, o_ref):
    tm = x_ref.shape[1]
    half = tm // 2
    for r0 in range(0, tm, half):
        rows = slice(r0, r0 + half)
        o_hg = of_ref[0, rows] + ob_ref[0, rows]
        gate = gate_ref[0, rows]
        gate = gate * jax.nn.sigmoid(gate)
        parts = [yfn_ref[0, rows]]
        for h in range(HG_HEADS):
            sl = slice(h * HG_DV, (h + 1) * HG_DV)
            parts.append((_rms(o_hg[:, sl]) * hgg_ref[:, sl] * gate[:, sl]).astype(bf16))
        parts.append(ymla_ref[0, rows])
        y = _dot(jnp.concatenate(parts, axis=1), w_ref[0])
        o_ref[0, rows] = x_ref[0, rows] + mod_ref[0, 2:3, :] * (_rms(y) * gpost_ref[...])


def _outproj(x, y_fn, o_f, o_b, proj, y_mla, w_out, hg_gain, g_post, mod, layer, tm):
    b_sz, n, _ = x.shape
    tok = lambda width, col=0: pl.BlockSpec((1, tm, width), lambda b, i: (b, i, col))
    return pl.pallas_call(
        _outproj_kernel,
        grid=(b_sz, n // tm),
        in_specs=[
            tok(D_MODEL), tok(FN_WIDTH), tok(HG_WIDTH), tok(HG_WIDTH), tok(HG_WIDTH, COL_HGATE),
            tok(MLA_WIDTH),
            pl.BlockSpec((1, D_MODEL, D_MODEL), lambda b, i: (layer, 0, 0)),
            pl.BlockSpec((1, HG_WIDTH), lambda b, i: (0, 0)),
            pl.BlockSpec((1, D_MODEL), lambda b, i: (0, 0)),
            pl.BlockSpec((1, 6, D_MODEL), lambda b, i: (b, 0, 0)),
        ],
        out_specs=tok(D_MODEL),
        out_shape=jax.ShapeDtypeStruct((b_sz, n, D_MODEL), f32),
        compiler_params=_params(("parallel", "parallel")),
        name="outproj",
    )(x, y_fn, o_f, o_b, proj, y_mla, w_out, hg_gain, g_post, mod)


def _ffn_kernel(x_ref, mod_ref, gpre_ref, gpost_ref, w1_ref, w2_ref, o_ref, h_scr):
    k = pl.program_id(2)

    @pl.when(k == 0)
    def _():
        y = _rms(x_ref[0]) * gpre_ref[...]
        h_scr[...] = (y * (1.0 + mod_ref[0, 4:5, :]) + mod_ref[0, 3:4, :]).astype(bf16)
        o_ref[...] = jnp.zeros_like(o_ref)

    a = jnp.maximum(_dot(h_scr[...], w1_ref[0]), 0.0)
    o_ref[0] += _dot((a * a).astype(bf16), w2_ref[0])

    @pl.when(k == pl.num_programs(2) - 1)
    def _():
        o_ref[0] = x_ref[0] + mod_ref[0, 5:6, :] * (_rms(o_ref[0]) * gpost_ref[...])


def _ffn(x, mod, g_pre, g_post, w1, w2, layer, tm, tf):
    b_sz, n, _ = x.shape
    return pl.pallas_call(
        _ffn_kernel,
        grid=(b_sz, n // tm, D_FF // tf),
        in_specs=[
            pl.BlockSpec((1, tm, D_MODEL), lambda b, i, k: (b, i, 0)),
            pl.BlockSpec((1, 6, D_MODEL), lambda b, i, k: (b, 0, 0)),
            pl.BlockSpec((1, D_MODEL), lambda b, i, k: (0, 0)),
            pl.BlockSpec((1, D_MODEL), lambda b, i, k: (0, 0)),
            pl.BlockSpec((1, D_MODEL, tf), lambda b, i, k: (layer, 0, k)),
            pl.BlockSpec((1, tf, D_MODEL), lambda b, i, k: (layer, k, 0)),
        ],
        out_specs=pl.BlockSpec((1, tm, D_MODEL), lambda b, i, k: (b, i, 0)),
        out_shape=jax.ShapeDtypeStruct((b_sz, n, D_MODEL), f32),
        scratch_shapes=[pltpu.VMEM((tm, D_MODEL), bf16)],
        compiler_params=_params(("parallel", "parallel", "arbitrary")),
        name="ffn",
    )(x, mod, g_pre, g_post, w1, w2)


def _layer(x, mod, wl, hg_tables, dft, layer, n_seq, ctx, s0, rope, is_context, tm, tq):
    b_sz, n, _ = x.shape
    n_seqs = b_sz * n // n_seq
    proj = _inproj(x, mod, wl["g_pre_mix"], wl["w_in"], layer, tm).reshape(n_seqs, n_seq, IN_COLS_PAD)
    y_fn = _fourier(proj, dft[0], dft[1], dft[2], min(n_seq, 512))
    hg = _hgrn(proj, wl["hg_lb"], hg_tables, layer, s0, is_context)
    mla = _mla(proj, wl["wq"], wl["wkv"], wl["q_norm"], wl["kv_norm"], ctx, layer, rope, is_context, tq)
    flat = lambda a: a.reshape(b_sz, n, a.shape[-1])
    x = _outproj(x, flat(y_fn), flat(hg[0]), flat(hg[1]), flat(proj), flat(mla[0]), wl["w_out"],
                 wl["hg_gain"], wl["g_post_mix"], mod, layer, tm)
    x = _ffn(x, mod, wl["g_pre_ff"], wl["g_post_ff"], wl["w_ff1"], wl["w_ff2"], layer, FFN_TM, FFN_TF)
    extras = (mla[1], mla[2], hg[2]) if is_context else None
    return x, extras


def kernel(x_prompt, x_sample, c, cache_ckv, cache_krope, state_hgrn, c_ctx, w_ada, b_ada, g_pre_mix,
           g_post_mix, g_pre_ff, g_post_ff, w_in, hg_lb, hg_gain, mla_q_norm, mla_kv_norm, w_q_b, w_kv_b,
           w_out, w_ff1, w_ff2):
    batch, seq, _ = x_prompt.shape
    dec_batch, dec_seq, _ = x_sample.shape

    rows = 16
    cond = jnp.concatenate([c_ctx[None, :], c, jnp.zeros((rows - 1 - dec_batch, D_MODEL), f32)], axis=0)
    mod = _ada(cond, w_ada, b_ada).reshape(DEPTH, rows, 6, D_MODEL)

    hg_tables = _hgrn_tables()
    ccs = _channel_tables()
    dft_p = (ccs,) + _dft_tables(seq)
    dft_s = (ccs,) + _dft_tables(dec_seq)
    rope = _rope_tables(dec_seq)

    yp = x_prompt.reshape(1, batch * seq, D_MODEL)
    ys = x_sample
    ckv_list, kr_list, st_list = [], [], []
    wq = w_q_b.reshape(DEPTH, Q_LORA, MLA_HEADS, MLA_NOPE + MLA_ROPE).astype(bf16)
    wq = jnp.pad(wq, ((0, 0), (0, 0), (0, 0), (0, QHEAD_PAD - MLA_NOPE - MLA_ROPE)))
    stacked = {
        "w_in": jnp.pad(w_in.astype(bf16), ((0, 0), (0, 0), (0, IN_COLS_PAD - IN_COLS))),
        "wq": wq.transpose(0, 2, 1, 3),
        "wkv": w_kv_b.astype(bf16),
        "w_out": w_out.astype(bf16),
        "w_ff1": w_ff1.astype(bf16),
        "w_ff2": w_ff2.astype(bf16),
    }
    for l in range(DEPTH):
        wl = {
            **stacked,
            "g_pre_mix": g_pre_mix[l][None, :], "g_post_mix": g_post_mix[l][None, :],
            "g_pre_ff": g_pre_ff[l][None, :], "g_post_ff": g_post_ff[l][None, :],
            "hg_lb": hg_lb, "hg_gain": hg_gain[l][None, :],
            "q_norm": mla_q_norm[l][None, :], "kv_norm": mla_kv_norm[l][None, :],
        }
        yp, (ckv_l, kr_l, st_l) = _layer(yp, mod[l, 0:1], wl, hg_tables, dft_p, l, seq, None, None, None,
                                         True, 512, seq)
        ckv_list.append(ckv_l)
        kr_list.append(kr_l)
        st_list.append(st_l)
        ys, _ = _layer(ys, mod[l, 1:1 + dec_batch], wl, hg_tables, dft_s, l, dec_seq,
                       (cache_ckv, cache_krope), state_hgrn, rope, False, 512, 512)

    return (yp.reshape(batch, seq, D_MODEL), ys, jnp.stack(ckv_list, axis=1), jnp.stack(kr_list, axis=1),
            jnp.stack(st_list, axis=1))
```

```python
import functools

import numpy as np
import jax
import jax.numpy as jnp
from jax import lax
from jax.experimental import pallas as pl
from jax.experimental.pallas import tpu as pltpu

f32 = jnp.float32
bf16 = jnp.bfloat16

D_MODEL = 2048
DEPTH = 2
GRID_W = 64
FN_HEADS = 4
FN_DH = 128
FN_WIDTH = FN_HEADS * FN_DH
HG_HEADS = 4
HG_DK = 128
HG_DV = 128
HG_QK = HG_HEADS * HG_DK
HG_WIDTH = HG_HEADS * HG_DV
MLA_HEADS = 8
MLA_NOPE = 128
MLA_ROPE = 64
MLA_V = 128
Q_LORA = 768
KV_LORA = 512
MLA_WIDTH = MLA_HEADS * MLA_V
D_FF = 4 * D_MODEL
ROPE_BASE = 10000.0
EPS = 1e-6
F_FLOOR = 1e-30

LANES = 128
SUBLANES = 8
ROW_CHUNK = 2 * SUBLANES
MXU_DIM = 256
IN_COLS = 4416
IN_COLS_PAD = 4608
IN_TILE = 768
COL_HQ, COL_HV, COL_HFF, COL_HFB, COL_HGATE = 1, 2, 3, 4, 5
COL_QA = 4
KVR_WIDTH = 640
COL_KVR = 6
QHEAD_PAD = 256
HG_CHUNK = 128
HG_LEVELS = (64, 32, 16, 8, 4, 2, 1)
HG_SPLIT = 2
HG_SEQS = 2
INPROJ_TM = 1024
OUTPROJ_TM = 512
MLA_KEY_CHUNK = 1280
FFN_TM = 1024
FFN_TF = 512
VMEM_LIMIT = 56 * 1024 * 1024


def _params(semantics):
    return pltpu.CompilerParams(dimension_semantics=semantics, vmem_limit_bytes=VMEM_LIMIT)


def _dot(a, b):
    return jnp.dot(a, b, preferred_element_type=f32)


def _dot_nt(a, b):
    return lax.dot_general(a, b, (((1,), (1,)), ((), ())), preferred_element_type=f32)


def _dot_tn(a, b):
    return lax.dot_general(a, b, (((0,), (0,)), ((), ())), preferred_element_type=f32)


def _rms(x):
    return x * lax.rsqrt(jnp.mean(x * x, axis=-1, keepdims=True) + EPS)


def _row_loop(n_rows, fn):
    def body(i, carry):
        fn(pl.ds(pl.multiple_of(i * ROW_CHUNK, ROW_CHUNK), ROW_CHUNK))
        return carry

    lax.fori_loop(0, n_rows // ROW_CHUNK, body, 0, unroll=4)


def _modulated_norm(x_ref, g_ref, scale_row, shift_row, h_scr):
    gain = g_ref[...] * (1.0 + scale_row)

    def rows(r):
        h_scr[r, :] = (_rms(x_ref[0, r, :]) * gain + shift_row).astype(bf16)

    _row_loop(h_scr.shape[0], rows)


def _ada_kernel(cond_ref, w_ref, b_ref, o_ref):
    cnd = cond_ref[...]
    act = cnd * jax.nn.sigmoid(cnd)
    o_ref[0] = _dot(act.astype(bf16), w_ref[0].astype(bf16)) + b_ref[0]


def _ada(cond, w_ada, b_ada):
    rows = cond.shape[0]
    tn = 1024
    return pl.pallas_call(
        _ada_kernel,
        grid=(DEPTH, 6 * D_MODEL // tn),
        in_specs=[
            pl.BlockSpec((rows, D_MODEL), lambda l, j: (0, 0)),
            pl.BlockSpec((1, D_MODEL, tn), lambda l, j: (l, 0, j)),
            pl.BlockSpec((1, 1, tn), lambda l, j: (l, 0, j)),
        ],
        out_specs=pl.BlockSpec((1, rows, tn), lambda l, j: (l, 0, j)),
        out_shape=jax.ShapeDtypeStruct((DEPTH, rows, 6 * D_MODEL), f32),
        compiler_params=_params(("parallel", "parallel")),
        name="ada_mod",
    )(cond, w_ada, b_ada.reshape(DEPTH, 1, 6 * D_MODEL))


def _inproj_kernel(x_ref, mod_ref, g_ref, w_ref, o_ref, h_scr):
    @pl.when(pl.program_id(2) == 0)
    def _():
        _modulated_norm(x_ref, g_ref, mod_ref[0, 1:2, :], mod_ref[0, 0:1, :], h_scr)

    o_ref[0] = _dot(h_scr[...], w_ref[0])


def _inproj(x, mod, g, w_in, layer, tm):
    b_sz, n, _ = x.shape
    return pl.pallas_call(
        _inproj_kernel,
        grid=(b_sz, n // tm, IN_COLS_PAD // IN_TILE),
        in_specs=[
            pl.BlockSpec((1, tm, D_MODEL), lambda b, i, j: (b, i, 0)),
            pl.BlockSpec((1, 6, D_MODEL), lambda b, i, j: (b, 0, 0)),
            pl.BlockSpec((1, D_MODEL), lambda b, i, j: (0, 0)),
            pl.BlockSpec((1, D_MODEL, IN_TILE), lambda b, i, j: (layer, 0, j)),
        ],
        out_specs=pl.BlockSpec((1, tm, IN_TILE), lambda b, i, j: (b, i, j)),
        out_shape=jax.ShapeDtypeStruct((b_sz, n, IN_COLS_PAD), f32),
        scratch_shapes=[pltpu.VMEM((tm, D_MODEL), bf16)],
        compiler_params=_params(("parallel", "parallel", "arbitrary")),
        name="inproj",
    )(x, mod, g, w_in)


def _fourier_kernel(u_ref, ccs_ref, cn_ref, sn_ref, o_ref, ucs_scr):
    @pl.when(pl.program_id(1) == 0)
    def _():
        ucs_scr[...] = _dot(u_ref[0].astype(bf16), ccs_ref[...]).astype(bf16)

    y = _dot(cn_ref[...], ucs_scr[:, :FN_WIDTH]) + _dot(sn_ref[...], ucs_scr[:, FN_WIDTH:])
    o_ref[0] = y.astype(bf16)


def _fourier(proj, ccs, cn, sn_neg, tr):
    b_sz, n, _ = proj.shape
    return pl.pallas_call(
        _fourier_kernel,
        grid=(b_sz, n // tr),
        in_specs=[
            pl.BlockSpec((1, n, FN_WIDTH), lambda b, i: (b, 0, 0)),
            pl.BlockSpec((FN_WIDTH, 2 * FN_WIDTH), lambda b, i: (0, 0)),
            pl.BlockSpec((tr, n), lambda b, i: (i, 0)),
            pl.BlockSpec((tr, n), lambda b, i: (i, 0)),
        ],
        out_specs=pl.BlockSpec((1, tr, FN_WIDTH), lambda b, i: (b, i, 0)),
        out_shape=jax.ShapeDtypeStruct((b_sz, n, FN_WIDTH), bf16),
        scratch_shapes=[pltpu.VMEM((n, 2 * FN_WIDTH), bf16)],
        compiler_params=_params(("parallel", "arbitrary")),
        name="fourier",
    )(proj, ccs, cn, sn_neg)


def _dft_tables(n):
    j = jnp.arange(n, dtype=jnp.int32)
    ang = ((j[:, None] * j[None, :]) % n).astype(f32) * (2.0 * np.pi / n)
    scale_n = 1.0 / np.sqrt(n)
    cn = (jnp.cos(ang) * scale_n).astype(bf16)
    sn_neg = (-jnp.sin(ang) * scale_n).astype(bf16)
    return cn, sn_neg


def _channel_tables():
    k = np.arange(FN_DH)
    ang = 2.0 * np.pi * ((k[:, None] * k[None, :]) % FN_DH) / FN_DH
    eye = np.eye(FN_HEADS)
    cc = np.kron(eye, np.cos(ang)) / np.sqrt(FN_DH)
    sc = np.kron(eye, np.sin(ang)) / np.sqrt(FN_DH)
    return jnp.asarray(np.concatenate([cc, sc], axis=1), dtype=bf16)


def _hgrn_tables():
    c = HG_CHUNK
    idx = np.arange(c)
    blocks = [(idx[None, :] <= idx[:, None])]
    masks = []
    for h in HG_LEVELS:
        mid = (idx // (2 * h)) * (2 * h) + h
        upper = idx >= mid
        if h < SUBLANES:
            row_up = (idx[None, :] >= mid[:, None]) & (idx[None, :] <= idx[:, None])
            row_lo = (idx[None, :] > idx[:, None]) & (idx[None, :] < mid[:, None])
            blocks.append(np.where(upper[:, None], row_up, row_lo))
        same = (idx[:, None] // (2 * h)) == (idx[None, :] // (2 * h))
        masks.append(same & upper[:, None] & ~upper[None, :])
    masks.append(np.eye(c, dtype=bool))
    a_f = np.concatenate([b.astype(np.float32) for b in blocks], axis=0)
    a_b = np.concatenate([b[::-1, ::-1].astype(np.float32) for b in blocks], axis=0)
    m_f = np.stack([m.astype(np.float32) for m in masks])
    m_b = np.stack([m[::-1, ::-1].astype(np.float32) for m in masks])
    rep = lambda a: jnp.asarray(np.concatenate([a] * HG_SPLIT, axis=1), dtype=bf16)
    return rep(a_f), rep(a_b), jnp.asarray(m_f), jnp.asarray(m_b)


def _block_row(x, h, row):
    c, w = x.shape
    xb = x.reshape(c // (2 * h), 2 * h, w)
    return jnp.broadcast_to(xb[:, row:row + 1, :], xb.shape).reshape(c, w)


def _hgrn_direction(q, v, pre, lb, a_ref, m_ref, states, backward):
    c = HG_CHUNK
    nl = len(HG_LEVELS)
    f = lb + (1.0 - lb) * jax.nn.sigmoid(pre)
    lf = jnp.log(jnp.maximum(f, F_FLOOR))
    kk = 1.0 - f
    pieces = []
    rem = lf
    for _ in range(HG_SPLIT):
        piece = rem.astype(bf16)
        pieces.append(piece)
        rem = rem - piece.astype(f32)
    sums = _dot(a_ref[...], jnp.concatenate(pieces, axis=0))
    cum = sums[0:c]
    total = cum[0:1] if backward else cum[c - 1:c]
    e_cum = jnp.exp(cum)
    e_rest = jnp.exp(jnp.minimum(total - cum, 0.0))
    e_total = jnp.exp(total)
    e_levels = []
    fine = 0
    for h in HG_LEVELS:
        if h >= SUBLANES:
            ref = _block_row(cum, h, h if backward else h - 1)
            e_levels.append(jnp.exp(-jnp.abs(cum - ref)))
        else:
            fine += 1
            e_levels.append(jnp.exp(sums[fine * c:(fine + 1) * c]))
    outs, new_states = [], []
    for h in range(HG_HEADS):
        sl = slice(h * HG_DK, (h + 1) * HG_DK)
        qh = q[:, sl]
        kh = kk[:, sl]
        vh = v[:, sl].astype(bf16)
        sc = m_ref[nl] * _dot_nt(qh.astype(bf16), kh.astype(bf16))
        for li in range(nl):
            el = e_levels[li][:, sl]
            sc = sc + m_ref[li] * _dot_nt((qh * el).astype(bf16), (kh * el).astype(bf16))
        st = states[h]
        outs.append(_dot(sc.astype(bf16), vh) + _dot_nt((qh * e_cum[:, sl]).astype(bf16), st.astype(bf16)))
        new_states.append(e_total[:, sl] * st + _dot_tn(vh, (kh * e_rest[:, sl]).astype(bf16)))
    return outs, new_states


def _hgrn_kernel(layer, has_s0, has_sout, *refs):
    (qf_ref, vf_ref, ff_ref, qb_ref, vb_ref, fb_ref, lb_ref, af_ref, ab_ref, mf_ref, mb_ref) = refs[:11]
    pos = 11
    s0_ref = None
    if has_s0:
        s0_ref = refs[pos]
        pos += 1
    of_ref, ob_ref = refs[pos], refs[pos + 1]
    pos += 2
    sout_ref = None
    if has_sout:
        sout_ref = refs[pos]
        pos += 1
    st_scr = refs[pos]
    ci = pl.program_id(1)

    n_seq = qf_ref.shape[0]

    @pl.when(ci == 0)
    def _():
        for s in range(n_seq):
            for d in range(2):
                for h in range(HG_HEADS):
                    if has_s0:
                        st_scr[s, d, h] = s0_ref[s, 0, d, h].T
                    else:
                        st_scr[s, d, h] = jnp.zeros((HG_DV, HG_DK), f32)

    raw = lb_ref[...]
    mx = raw[0]
    for i in range(1, DEPTH):
        mx = jnp.maximum(mx, raw[i])
    ex = [jnp.exp(raw[i] - mx) for i in range(DEPTH)]
    den = ex[0]
    for i in range(1, DEPTH):
        den = den + ex[i]
    lb = jnp.zeros_like(den)
    for i in range(1, layer + 1):
        lb = lb + ex[i] / den

    loaded = [[[st_scr[s, d, h] for h in range(HG_HEADS)] for d in range(2)] for s in range(n_seq)]
    results = []
    for s in range(n_seq):
        o_f, st_f = _hgrn_direction(qf_ref[s], vf_ref[s], ff_ref[s], lb[0:1], af_ref, mf_ref, loaded[s][0], False)
        o_b, st_b = _hgrn_direction(qb_ref[s], vb_ref[s], fb_ref[s], lb[1:2], ab_ref, mb_ref, loaded[s][1], True)
        of_ref[s] = jnp.concatenate(o_f, axis=1)
        ob_ref[s] = jnp.concatenate(o_b, axis=1)
        results.append((st_f, st_b))
    for s in range(n_seq):
        for h in range(HG_HEADS):
            st_scr[s, 0, h] = results[s][0][h]
            st_scr[s, 1, h] = results[s][1][h]

    if has_sout:
        @pl.when(ci == pl.num_programs(1) - 1)
        def _():
            for s in range(n_seq):
                for d in range(2):
                    for h in range(HG_HEADS):
                        sout_ref[s, d, h] = st_scr[s, d, h].T


def _hgrn(proj, hg_lb, tables, layer, s0, want_state):
    b_sz, n, _ = proj.shape
    nc = n // HG_CHUNK
    a_f, a_b, m_f, m_b = tables
    c = HG_CHUNK
    sb = HG_SEQS
    fwd = lambda col: pl.BlockSpec((sb, c, HG_QK), lambda b, i: (b, i, col))
    bwd = lambda col: pl.BlockSpec((sb, c, HG_QK), lambda b, i: (b, nc - 1 - i, col))
    const = lambda shape: pl.BlockSpec(shape, lambda b, i: (0,) * len(shape))
    in_specs = [fwd(COL_HQ), fwd(COL_HV), fwd(COL_HFF), bwd(COL_HQ), bwd(COL_HV), bwd(COL_HFB),
                const(hg_lb.shape), const(a_f.shape), const(a_b.shape), const(m_f.shape), const(m_b.shape)]
    args = [proj, proj, proj, proj, proj, proj, hg_lb, a_f, a_b, m_f, m_b]
    state_spec = pl.BlockSpec((sb, 2, HG_HEADS, HG_DK, HG_DV), lambda b, i: (b, 0, 0, 0, 0))
    if s0 is not None:
        in_specs.append(pl.BlockSpec((sb, 1, 2, HG_HEADS, HG_DK, HG_DV), lambda b, i: (b, layer, 0, 0, 0, 0)))
        args.append(s0)
    out_specs = [pl.BlockSpec((sb, c, HG_WIDTH), lambda b, i: (b, i, 0)),
                 pl.BlockSpec((sb, c, HG_WIDTH), lambda b, i: (b, nc - 1 - i, 0))]
    out_shape = [jax.ShapeDtypeStruct((b_sz, n, HG_WIDTH), f32)] * 2
    if want_state:
        out_specs.append(state_spec)
        out_shape.append(jax.ShapeDtypeStruct((b_sz, 2, HG_HEADS, HG_DK, HG_DV), f32))
    return pl.pallas_call(
        functools.partial(_hgrn_kernel, layer, s0 is not None, want_state),
        grid=(b_sz // sb, nc),
        in_specs=in_specs,
        out_specs=out_specs,
        out_shape=out_shape,
        scratch_shapes=[pltpu.VMEM((sb, 2, HG_HEADS, HG_DV, HG_DK), f32)],
        compiler_params=_params(("parallel", "arbitrary")),
        name="hgrn",
    )(*args)


def _rope_rotate(x, cos, sin):
    lane = lax.broadcasted_iota(jnp.int32, x.shape, 1)
    first_half = (lane % (MLA_ROPE // 2)) < (MLA_ROPE // 4)
    quarter = MLA_ROPE // 4
    rot = jnp.where(first_half, -pltpu.roll(x, LANES - quarter, 1), pltpu.roll(x, quarter, 1))
    return x * cos + rot * sin


def _mla_kernel(n_ctx, use_rope, want_cache, tq, *refs):
    qa_ref, kvr_ref, wq_ref, wkv_ref, qn_ref, kvn_ref = refs[:6]
    pos = 6
    if n_ctx:
        cckv_ref, ckr_ref = refs[pos], refs[pos + 1]
        pos += 2
    if use_rope:
        cosq_ref, sinq_ref, cosk_ref, sink_ref = refs[pos:pos + 4]
        pos += 4
    y_ref = refs[pos]
    pos += 1
    if want_cache:
        ockv_ref, okr_ref = refs[pos], refs[pos + 1]
        pos += 2
    kcat_scr, v_scr, o_scr = refs[pos:pos + 3]
    n = kvr_ref.shape[1]
    rows = 512 if n % 512 == 0 else n

    def expand(r0, nrows, c_kv, krp):
        kv = _dot(c_kv, wkv_ref[0])
        ones = jnp.ones((nrows, MLA_V), bf16)
        for h in range(MLA_HEADS):
            base = h * (MLA_NOPE + MLA_V)
            kcat_scr[h, r0:r0 + nrows, 0:MLA_NOPE] = kv[:, base:base + MLA_NOPE].astype(bf16)
            kcat_scr[h, r0:r0 + nrows, MLA_NOPE:QHEAD_PAD] = krp
            v_scr[h, r0:r0 + nrows, 0:MLA_V] = kv[:, base + MLA_NOPE:base + MLA_NOPE + MLA_V].astype(bf16)
            v_scr[h, r0:r0 + nrows, MLA_V:2 * MLA_V] = ones

    @pl.when(pl.program_id(1) == 0)
    def _():
        if n_ctx:
            zeros = jnp.zeros((n_ctx, LANES - MLA_ROPE), f32)
            krp_ctx = jnp.concatenate([ckr_ref[0, 0], zeros], axis=1).astype(bf16)
            expand(0, n_ctx, cckv_ref[0, 0].astype(bf16), krp_ctx)
        for r0 in range(0, n, rows):
            blk = kvr_ref[0, r0:r0 + rows]
            c_kv = _rms(blk[:, :KV_LORA]) * kvn_ref[...]
            krp = blk[:, KV_LORA:]
            if want_cache:
                ockv_ref[0, r0:r0 + rows] = c_kv
                okr_ref[0, r0:r0 + rows] = krp[:, :MLA_ROPE]
            if use_rope:
                krp = _rope_rotate(krp, cosk_ref[r0:r0 + rows], sink_ref[r0:r0 + rows])
            expand(n_ctx + r0, rows, c_kv.astype(bf16), krp.astype(bf16))

    scale = float(MLA_NOPE + MLA_ROPE) ** -0.5
    qn = (_rms(qa_ref[0]) * qn_ref[...]).astype(bf16)
    sk = kcat_scr.shape[1]
    n_kc = 2 if sk > MLA_KEY_CHUNK else 1
    kc = sk // n_kc

    def head(h):
        q = _dot(qn, wq_ref[0, h]) * scale
        q_rope = q[:, MLA_NOPE:]
        if use_rope:
            q_rope = _rope_rotate(q_rope, cosq_ref[...], sinq_ref[...])
        q_cat = jnp.concatenate([q[:, :MLA_NOPE], q_rope], axis=1).astype(bf16)
        m_run = acc = None
        for c in range(n_kc):
            s = _dot_nt(q_cat, kcat_scr[h, c * kc:(c + 1) * kc])
            m_c = jnp.max(s, axis=-1, keepdims=True)
            m_new = m_c if c == 0 else jnp.maximum(m_run, m_c)
            p = jnp.exp((s - m_new).astype(bf16))
            pv = _dot(p, v_scr[h, c * kc:(c + 1) * kc])
            acc = pv if c == 0 else jnp.exp(m_run - m_new) * acc + pv
            m_run = m_new
        return (acc[:, :MLA_V] / acc[:, MLA_V:]).astype(bf16)

    if n_kc == 1:
        for h in range(MLA_HEADS):
            y_ref[0, :, h * MLA_V:(h + 1) * MLA_V] = head(h)
    else:
        def body(h, carry):
            o_scr[h] = head(h)
            return carry

        lax.fori_loop(0, MLA_HEADS, body, 0, unroll=4)
        for h in range(MLA_HEADS):
            y_ref[0, :, h * MLA_V:(h + 1) * MLA_V] = o_scr[h]


def _mla(proj, wq, wkv, q_norm, kv_norm, ctx, layer, rope, want_cache, tq):
    b_sz, n, _ = proj.shape
    n_ctx = 0 if ctx is None else ctx[0].shape[2]
    sk = n_ctx + n
    single = pl.Buffered(1)
    const = lambda shape: pl.BlockSpec(shape, lambda b, i: (0,) * len(shape), pipeline_mode=single)
    in_specs = [
        pl.BlockSpec((1, tq, Q_LORA), lambda b, i: (b, i, COL_QA)),
        pl.BlockSpec((1, n, KVR_WIDTH), lambda b, i: (b, 0, COL_KVR), pipeline_mode=single),
        pl.BlockSpec((1,) + wq.shape[1:], lambda b, i: (layer, 0, 0, 0), pipeline_mode=single),
        pl.BlockSpec((1,) + wkv.shape[1:], lambda b, i: (layer, 0, 0), pipeline_mode=single),
        const(q_norm.shape), const(kv_norm.shape),
    ]
    args = [proj, proj, wq, wkv, q_norm, kv_norm]
    if ctx is not None:
        in_specs += [
            pl.BlockSpec((1, 1, n_ctx, KV_LORA), lambda b, i: (b, layer, 0, 0), pipeline_mode=single),
            pl.BlockSpec((1, 1, n_ctx, MLA_ROPE), lambda b, i: (b, layer, 0, 0), pipeline_mode=single)]
        args += [ctx[0], ctx[1]]
    if rope is not None:
        cos, sin = rope
        in_specs += [pl.BlockSpec((tq, LANES), lambda b, i: (i, 0)),
                     pl.BlockSpec((tq, LANES), lambda b, i: (i, 0)),
                     const(cos.shape), const(sin.shape)]
        args += [cos, sin, cos, sin]
    out_specs = [pl.BlockSpec((1, tq, MLA_WIDTH), lambda b, i: (b, i, 0))]
    out_shape = [jax.ShapeDtypeStruct((b_sz, n, MLA_WIDTH), bf16)]
    if want_cache:
        out_specs += [pl.BlockSpec((1, n, KV_LORA), lambda b, i: (b, 0, 0)),
                      pl.BlockSpec((1, n, MLA_ROPE), lambda b, i: (b, 0, 0))]
        out_shape += [jax.ShapeDtypeStruct((b_sz, n, KV_LORA), f32),
                      jax.ShapeDtypeStruct((b_sz, n, MLA_ROPE), f32)]
    return pl.pallas_call(
        functools.partial(_mla_kernel, n_ctx, rope is not None, want_cache, tq),
        grid=(b_sz, n // tq),
        in_specs=in_specs,
        out_specs=out_specs,
        out_shape=out_shape,
        scratch_shapes=[pltpu.VMEM((MLA_HEADS, sk, QHEAD_PAD), bf16), pltpu.VMEM((MLA_HEADS, sk, 2 * MLA_V), bf16),
                        pltpu.VMEM((MLA_HEADS, tq, MLA_V), bf16)],
        compiler_params=_params(("parallel", "arbitrary")),
        name="mla",
    )(*args)


def _rope_tables(n):
    t = jnp.arange(n)
    r = (t // GRID_W).astype(f32)
    col = (t % GRID_W).astype(f32)
    nf = MLA_ROPE // 4
    inv = ROPE_BASE ** (-jnp.arange(nf, dtype=f32) / nf)
    ar = r[:, None] * inv
    ac = col[:, None] * inv
    ang = jnp.concatenate([ar, ar, ac, ac], axis=-1)
    pad = LANES - MLA_ROPE
    cos = jnp.concatenate([jnp.cos(ang), jnp.ones((n, pad), f32)], axis=-1)
    sin = jnp.concatenate([jnp.sin(ang), jnp.zeros((n, pad), f32)], axis=-1)
    return cos, sin


def _outproj_kernel(x_ref, yfn_ref, of_ref, ob_ref, gate_ref, ymla_ref, w_ref, hgg_ref, gpost_ref,
                    mod_ref, o_ref):
    tm = x_ref.shape[1]
    half = tm // 2
    for r0 in range(0, tm, half):
        rows = slice(r0, r0 + half)
        o_hg = of_ref[0, rows] + ob_ref[0, rows]
        gate = gate_ref[0, rows]
        gate = gate * jax.nn.sigmoid(gate)
        parts = [yfn_ref[0, rows]]
        for h in range(HG_HEADS):
            sl = slice(h * HG_DV, (h + 1) * HG_DV)
            parts.append((_rms(o_hg[:, sl]) * hgg_ref[:, sl] * gate[:, sl]).astype(bf16))
        parts.append(ymla_ref[0, rows])
        y = _dot(jnp.concatenate(parts, axis=1), w_ref[0])
        o_ref[0, rows] = x_ref[0, rows] + mod_ref[0, 2:3, :] * (_rms(y) * gpost_ref[...])


def _outproj(x, y_fn, o_f, o_b, proj, y_mla, w_out, hg_gain, g_post, mod, layer, tm):
    b_sz, n, _ = x.shape
    tok = lambda width, col=0: pl.BlockSpec((1, tm, width), lambda b, i: (b, i, col))
    return pl.pallas_call(
        _outproj_kernel,
        grid=(b_sz, n // tm),
        in_specs=[
            tok(D_MODEL), tok(FN_WIDTH), tok(HG_WIDTH), tok(HG_WIDTH), tok(HG_WIDTH, COL_HGATE),
            tok(MLA_WIDTH),
            pl.BlockSpec((1, D_MODEL, D_MODEL), lambda b, i: (layer, 0, 0)),
            pl.BlockSpec((1, HG_WIDTH), lambda b, i: (0, 0)),
            pl.BlockSpec((1, D_MODEL), lambda b, i: (0, 0)),
            pl.BlockSpec((1, 6, D_MODEL), lambda b, i: (b, 0, 0)),
        ],
        out_specs=tok(D_MODEL),
        out_shape=jax.ShapeDtypeStruct((b_sz, n, D_MODEL), f32),
        compiler_params=_params(("parallel", "parallel")),
        name="outproj",
    )(x, y_fn, o_f, o_b, proj, y_mla, w_out, hg_gain, g_post, mod)


def _ffn_kernel(x_ref, mod_ref, gpre_ref, gpost_ref, w1_ref, w2_ref, o_ref, h_scr):
    k = pl.program_id(2)

    @pl.when(k == 0)
    def _():
        _modulated_norm(x_ref, gpre_ref, mod_ref[0, 4:5, :], mod_ref[0, 3:4, :], h_scr)
        o_ref[...] = jnp.zeros_like(o_ref)

    a = jnp.maximum(_dot(h_scr[...], w1_ref[0]), 0.0)
    o_ref[0] += _dot((a * a).astype(bf16), w2_ref[0])

    @pl.when(k == pl.num_programs(2) - 1)
    def _():
        o_ref[0] = x_ref[0] + mod_ref[0, 5:6, :] * (_rms(o_ref[0]) * gpost_ref[...])


def _ffn(x, mod, g_pre, g_post, w1, w2, layer, tm, tf):
    b_sz, n, _ = x.shape
    return pl.pallas_call(
        _ffn_kernel,
        grid=(b_sz, n // tm, D_FF // tf),
        in_specs=[
            pl.BlockSpec((1, tm, D_MODEL), lambda b, i, k: (b, i, 0)),
            pl.BlockSpec((1, 6, D_MODEL), lambda b, i, k: (b, 0, 0)),
            pl.BlockSpec((1, D_MODEL), lambda b, i, k: (0, 0)),
            pl.BlockSpec((1, D_MODEL), lambda b, i, k: (0, 0)),
            pl.BlockSpec((1, D_MODEL, tf), lambda b, i, k: (layer, 0, k)),
            pl.BlockSpec((1, tf, D_MODEL), lambda b, i, k: (layer, k, 0)),
        ],
        out_specs=pl.BlockSpec((1, tm, D_MODEL), lambda b, i, k: (b, i, 0)),
        out_shape=jax.ShapeDtypeStruct((b_sz, n, D_MODEL), f32),
        scratch_shapes=[pltpu.VMEM((tm, D_MODEL), bf16)],
        compiler_params=_params(("parallel", "parallel", "arbitrary")),
        name="ffn",
    )(x, mod, g_pre, g_post, w1, w2)


def _layer(x, mod, wl, hg_tables, dft, layer, n_seq, ctx, s0, rope, is_context, tq):
    b_sz, n, _ = x.shape
    n_seqs = b_sz * n // n_seq
    proj = _inproj(x, mod, wl["g_pre_mix"], wl["w_in"], layer, INPROJ_TM).reshape(n_seqs, n_seq, IN_COLS_PAD)
    y_fn = _fourier(proj, dft[0], dft[1], dft[2], min(n_seq, 512))
    hg = _hgrn(proj, wl["hg_lb"], hg_tables, layer, s0, is_context)
    mla = _mla(proj, wl["wq"], wl["wkv"], wl["q_norm"], wl["kv_norm"], ctx, layer, rope, is_context, tq)
    flat = lambda a: a.reshape(b_sz, n, a.shape[-1])
    x = _outproj(x, flat(y_fn), flat(hg[0]), flat(hg[1]), flat(proj), flat(mla[0]), wl["w_out"],
                 wl["hg_gain"], wl["g_post_mix"], mod, layer, OUTPROJ_TM)
    x = _ffn(x, mod, wl["g_pre_ff"], wl["g_post_ff"], wl["w_ff1"], wl["w_ff2"], layer, FFN_TM, FFN_TF)
    extras = (mla[1], mla[2], hg[2]) if is_context else None
    return x, extras


def kernel(x_prompt, x_sample, c, cache_ckv, cache_krope, state_hgrn, c_ctx, w_ada, b_ada, g_pre_mix,
           g_post_mix, g_pre_ff, g_post_ff, w_in, hg_lb, hg_gain, mla_q_norm, mla_kv_norm, w_q_b, w_kv_b,
           w_out, w_ff1, w_ff2):
    batch, seq, _ = x_prompt.shape
    dec_batch, dec_seq, _ = x_sample.shape

    rows = 16
    cond = jnp.concatenate([c_ctx[None, :], c, jnp.zeros((rows - 1 - dec_batch, D_MODEL), f32)], axis=0)
    mod = _ada(cond, w_ada, b_ada).reshape(DEPTH, rows, 6, D_MODEL)

    hg_tables = _hgrn_tables()
    ccs = _channel_tables()
    dft_p = (ccs,) + _dft_tables(seq)
    dft_s = (ccs,) + _dft_tables(dec_seq)
    rope = _rope_tables(dec_seq)

    yp = x_prompt.reshape(1, batch * seq, D_MODEL)
    ys = x_sample
    ckv_list, kr_list, st_list = [], [], []
    wq = w_q_b.reshape(DEPTH, Q_LORA, MLA_HEADS, MLA_NOPE + MLA_ROPE).astype(bf16)
    wq = jnp.pad(wq, ((0, 0), (0, 0), (0, 0), (0, QHEAD_PAD - MLA_NOPE - MLA_ROPE)))
    stacked = {
        "w_in": jnp.pad(w_in.astype(bf16), ((0, 0), (0, 0), (0, IN_COLS_PAD - IN_COLS))),
        "wq": wq.transpose(0, 2, 1, 3),
        "wkv": w_kv_b.astype(bf16),
        "w_out": w_out.astype(bf16),
        "w_ff1": w_ff1.astype(bf16),
        "w_ff2": w_ff2.astype(bf16),
    }
    for l in range(DEPTH):
        wl = {
            **stacked,
            "g_pre_mix": g_pre_mix[l][None, :], "g_post_mix": g_post_mix[l][None, :],
            "g_pre_ff": g_pre_ff[l][None, :], "g_post_ff": g_post_ff[l][None, :],
            "hg_lb": hg_lb, "hg_gain": hg_gain[l][None, :],
            "q_norm": mla_q_norm[l][None, :], "kv_norm": mla_kv_norm[l][None, :],
        }
        yp, (ckv_l, kr_l, st_l) = _layer(yp, mod[l, 0:1], wl, hg_tables, dft_p, l, seq, None, None, None,
                                         True, seq)
        ckv_list.append(ckv_l)
        kr_list.append(kr_l)
        st_list.append(st_l)
        ys, _ = _layer(ys, mod[l, 1:1 + dec_batch], wl, hg_tables, dft_s, l, dec_seq,
                       (cache_ckv, cache_krope), state_hgrn, rope, False, 512)

    return (yp.reshape(batch, seq, D_MODEL), ys, jnp.stack(ckv_list, axis=1), jnp.stack(kr_list, axis=1),
            jnp.stack(st_list, axis=1))
```

```python
import functools

import numpy as np
import jax
import jax.numpy as jnp
from jax import lax
from jax.experimental import pallas as pl
from jax.experimental.pallas import tpu as pltpu

f32 = jnp.float32
bf16 = jnp.bfloat16

D_MODEL = 2048
DEPTH = 2
GRID_W = 64
FN_HEADS = 4
FN_DH = 128
FN_WIDTH = FN_HEADS * FN_DH
HG_HEADS = 4
HG_DK = 128
HG_DV = 128
HG_QK = HG_HEADS * HG_DK
HG_WIDTH = HG_HEADS * HG_DV
MLA_HEADS = 8
MLA_NOPE = 128
MLA_ROPE = 64
MLA_V = 128
Q_LORA = 768
KV_LORA = 512
MLA_WIDTH = MLA_HEADS * MLA_V
D_FF = 4 * D_MODEL
ROPE_BASE = 10000.0
EPS = 1e-6
F_FLOOR = 1e-30

LANES = 128
SUBLANES = 8
ROW_CHUNK = 2 * SUBLANES
MXU_DIM = 256
IN_COLS = 4416
IN_COLS_PAD = 4608
IN_TILE = 768
COL_HQ, COL_HV, COL_HFF, COL_HFB, COL_HGATE = 1, 2, 3, 4, 5
COL_QA = 4
KVR_WIDTH = 640
COL_KVR = 6
QHEAD_PAD = 256
HG_CHUNK = 128
HG_LEVELS = (64, 32, 16, 8, 4, 2, 1)
HG_SPLIT = 2
HG_SEQS = 4
INPROJ_TM = 1024
OUTPROJ_TM = 512
MLA_KEY_CHUNK = 1280
FFN_TM = 1024
FFN_TF = 512
VMEM_LIMIT = 56 * 1024 * 1024


def _params(semantics):
    return pltpu.CompilerParams(dimension_semantics=semantics, vmem_limit_bytes=VMEM_LIMIT)


def _dot(a, b):
    return jnp.dot(a, b, preferred_element_type=f32)


def _dot_nt(a, b):
    return lax.dot_general(a, b, (((1,), (1,)), ((), ())), preferred_element_type=f32)


def _dot_tn(a, b):
    return lax.dot_general(a, b, (((0,), (0,)), ((), ())), preferred_element_type=f32)


def _rms(x):
    return x * lax.rsqrt(jnp.mean(x * x, axis=-1, keepdims=True) + EPS)


def _row_loop(n_rows, fn):
    def body(i, carry):
        fn(pl.ds(pl.multiple_of(i * ROW_CHUNK, ROW_CHUNK), ROW_CHUNK))
        return carry

    lax.fori_loop(0, n_rows // ROW_CHUNK, body, 0, unroll=4)


def _modulated_norm(x_ref, g_ref, scale_row, shift_row, h_scr):
    gain = g_ref[...] * (1.0 + scale_row)

    def rows(r):
        h_scr[r, :] = (_rms(x_ref[0, r, :]) * gain + shift_row).astype(bf16)

    _row_loop(h_scr.shape[0], rows)


def _ada_kernel(cond_ref, w_ref, b_ref, o_ref):
    cnd = cond_ref[...]
    act = cnd * jax.nn.sigmoid(cnd)
    o_ref[0] = _dot(act.astype(bf16), w_ref[0].astype(bf16)) + b_ref[0]


def _ada(cond, w_ada, b_ada):
    rows = cond.shape[0]
    tn = 1024
    return pl.pallas_call(
        _ada_kernel,
        grid=(DEPTH, 6 * D_MODEL // tn),
        in_specs=[
            pl.BlockSpec((rows, D_MODEL), lambda l, j: (0, 0)),
            pl.BlockSpec((1, D_MODEL, tn), lambda l, j: (l, 0, j)),
            pl.BlockSpec((1, 1, tn), lambda l, j: (l, 0, j)),
        ],
        out_specs=pl.BlockSpec((1, rows, tn), lambda l, j: (l, 0, j)),
        out_shape=jax.ShapeDtypeStruct((DEPTH, rows, 6 * D_MODEL), f32),
        compiler_params=_params(("parallel", "parallel")),
        name="ada_mod",
    )(cond, w_ada, b_ada.reshape(DEPTH, 1, 6 * D_MODEL))


def _inproj_kernel(x_ref, mod_ref, g_ref, w_ref, o_ref, h_scr):
    @pl.when(pl.program_id(2) == 0)
    def _():
        _modulated_norm(x_ref, g_ref, mod_ref[0, 1:2, :], mod_ref[0, 0:1, :], h_scr)

    o_ref[0] = _dot(h_scr[...], w_ref[0])


def _inproj(x, mod, g, w_in, layer, tm):
    b_sz, n, _ = x.shape
    return pl.pallas_call(
        _inproj_kernel,
        grid=(b_sz, n // tm, IN_COLS_PAD // IN_TILE),
        in_specs=[
            pl.BlockSpec((1, tm, D_MODEL), lambda b, i, j: (b, i, 0)),
            pl.BlockSpec((1, 6, D_MODEL), lambda b, i, j: (b, 0, 0)),
            pl.BlockSpec((1, D_MODEL), lambda b, i, j: (0, 0)),
            pl.BlockSpec((1, D_MODEL, IN_TILE), lambda b, i, j: (layer, 0, j)),
        ],
        out_specs=pl.BlockSpec((1, tm, IN_TILE), lambda b, i, j: (b, i, j)),
        out_shape=jax.ShapeDtypeStruct((b_sz, n, IN_COLS_PAD), f32),
        scratch_shapes=[pltpu.VMEM((tm, D_MODEL), bf16)],
        compiler_params=_params(("parallel", "parallel", "arbitrary")),
        name="inproj",
    )(x, mod, g, w_in)


def _fourier_kernel(u_ref, ccs_ref, cn_ref, sn_ref, o_ref, ucs_scr):
    @pl.when(pl.program_id(1) == 0)
    def _():
        ucs_scr[...] = _dot(u_ref[0].astype(bf16), ccs_ref[...]).astype(bf16)

    y = _dot(cn_ref[...], ucs_scr[:, :FN_WIDTH]) + _dot(sn_ref[...], ucs_scr[:, FN_WIDTH:])
    o_ref[0] = y.astype(bf16)


def _fourier(proj, ccs, cn, sn_neg, tr):
    b_sz, n, _ = proj.shape
    return pl.pallas_call(
        _fourier_kernel,
        grid=(b_sz, n // tr),
        in_specs=[
            pl.BlockSpec((1, n, FN_WIDTH), lambda b, i: (b, 0, 0)),
            pl.BlockSpec((FN_WIDTH, 2 * FN_WIDTH), lambda b, i: (0, 0)),
            pl.BlockSpec((tr, n), lambda b, i: (i, 0)),
            pl.BlockSpec((tr, n), lambda b, i: (i, 0)),
        ],
        out_specs=pl.BlockSpec((1, tr, FN_WIDTH), lambda b, i: (b, i, 0)),
        out_shape=jax.ShapeDtypeStruct((b_sz, n, FN_WIDTH), bf16),
        scratch_shapes=[pltpu.VMEM((n, 2 * FN_WIDTH), bf16)],
        compiler_params=_params(("parallel", "arbitrary")),
        name="fourier",
    )(proj, ccs, cn, sn_neg)


def _dft_tables(n):
    j = np.arange(n, dtype=np.int64)
    ang = ((j[:, None] * j[None, :]) % n).astype(np.float64) * (2.0 * np.pi / n)
    scale_n = 1.0 / np.sqrt(n)
    cn = jnp.asarray(np.cos(ang) * scale_n, dtype=bf16)
    sn_neg = jnp.asarray(-np.sin(ang) * scale_n, dtype=bf16)
    return cn, sn_neg


def _channel_tables():
    k = np.arange(FN_DH)
    ang = 2.0 * np.pi * ((k[:, None] * k[None, :]) % FN_DH) / FN_DH
    eye = np.eye(FN_HEADS)
    cc = np.kron(eye, np.cos(ang)) / np.sqrt(FN_DH)
    sc = np.kron(eye, np.sin(ang)) / np.sqrt(FN_DH)
    return jnp.asarray(np.concatenate([cc, sc], axis=1), dtype=bf16)


def _hgrn_tables():
    c = HG_CHUNK
    idx = np.arange(c)
    blocks = [(idx[None, :] <= idx[:, None])]
    masks = []
    for h in HG_LEVELS:
        mid = (idx // (2 * h)) * (2 * h) + h
        upper = idx >= mid
        if h < SUBLANES:
            row_up = (idx[None, :] >= mid[:, None]) & (idx[None, :] <= idx[:, None])
            row_lo = (idx[None, :] > idx[:, None]) & (idx[None, :] < mid[:, None])
            blocks.append(np.where(upper[:, None], row_up, row_lo))
        same = (idx[:, None] // (2 * h)) == (idx[None, :] // (2 * h))
        masks.append(same & upper[:, None] & ~upper[None, :])
    masks.append(np.eye(c, dtype=bool))
    a_f = np.concatenate([b.astype(np.float32) for b in blocks], axis=0)
    a_b = np.concatenate([b[::-1, ::-1].astype(np.float32) for b in blocks], axis=0)
    m_f = np.stack([m.astype(np.float32) for m in masks])
    m_b = np.stack([m[::-1, ::-1].astype(np.float32) for m in masks])
    rep = lambda a: jnp.asarray(np.concatenate([a] * HG_SPLIT, axis=1), dtype=bf16)
    return rep(a_f), rep(a_b), jnp.asarray(m_f), jnp.asarray(m_b)


def _block_row(x, h, row):
    c, w = x.shape
    xb = x.reshape(c // (2 * h), 2 * h, w)
    return jnp.broadcast_to(xb[:, row:row + 1, :], xb.shape).reshape(c, w)


def _hgrn_direction(q, v, pre, lb, a_ref, m_ref, states, backward):
    c = HG_CHUNK
    nl = len(HG_LEVELS)
    f = lb + (1.0 - lb) * jax.nn.sigmoid(pre)
    lf = jnp.log(jnp.maximum(f, F_FLOOR))
    kk = 1.0 - f
    pieces = []
    rem = lf
    for _ in range(HG_SPLIT):
        piece = rem.astype(bf16)
        pieces.append(piece)
        rem = rem - piece.astype(f32)
    sums = _dot(a_ref[...], jnp.concatenate(pieces, axis=0))
    cum = sums[0:c]
    total = cum[0:1] if backward else cum[c - 1:c]
    q16 = q.astype(bf16)
    k16 = kk.astype(bf16)
    v16 = v.astype(bf16)
    decay16 = lambda log_decay: jnp.exp(log_decay.astype(bf16))
    q_inter = q16 * decay16(cum)
    k_state = k16 * decay16(jnp.minimum(total - cum, 0.0))
    e_total = jnp.exp(total)
    q_levels, k_levels = [q16], [k16]
    fine = 0
    for h in HG_LEVELS:
        if h >= SUBLANES:
            ref = _block_row(cum, h, h if backward else h - 1)
            el = decay16(-jnp.abs(cum - ref))
        else:
            fine += 1
            el = decay16(sums[fine * c:(fine + 1) * c])
        q_levels.append(q16 * el)
        k_levels.append(k16 * el)
    outs, new_states = [], []
    for h in range(HG_HEADS):
        sl = slice(h * HG_DK, (h + 1) * HG_DK)
        vh = v16[:, sl]
        sc = m_ref[nl] * _dot_nt(q_levels[0][:, sl], k_levels[0][:, sl])
        for li in range(nl):
            sc = sc + m_ref[li] * _dot_nt(q_levels[li + 1][:, sl], k_levels[li + 1][:, sl])
        st = states[h]
        outs.append(_dot(sc.astype(bf16), vh) + _dot_nt(q_inter[:, sl], st.astype(bf16)))
        new_states.append(e_total[:, sl] * st + _dot_tn(vh, k_state[:, sl]))
    return outs, new_states


def _hgrn_kernel(layer, has_s0, has_sout, *refs):
    (qf_ref, vf_ref, ff_ref, qb_ref, vb_ref, fb_ref, lb_ref, af_ref, ab_ref, mf_ref, mb_ref) = refs[:11]
    pos = 11
    s0_ref = None
    if has_s0:
        s0_ref = refs[pos]
        pos += 1
    of_ref, ob_ref = refs[pos], refs[pos + 1]
    pos += 2
    sout_ref = None
    if has_sout:
        sout_ref = refs[pos]
        pos += 1
    st_scr = refs[pos]
    ci = pl.program_id(1)

    n_seq = qf_ref.shape[0]

    @pl.when(ci == 0)
    def _():
        for s in range(n_seq):
            for d in range(2):
                for h in range(HG_HEADS):
                    if has_s0:
                        st_scr[s, d, h] = s0_ref[s, 0, d, h].T
                    else:
                        st_scr[s, d, h] = jnp.zeros((HG_DV, HG_DK), f32)

    raw = lb_ref[...]
    mx = raw[0]
    for i in range(1, DEPTH):
        mx = jnp.maximum(mx, raw[i])
    ex = [jnp.exp(raw[i] - mx) for i in range(DEPTH)]
    den = ex[0]
    for i in range(1, DEPTH):
        den = den + ex[i]
    lb = jnp.zeros_like(den)
    for i in range(1, layer + 1):
        lb = lb + ex[i] / den

    loaded = [[[st_scr[s, d, h] for h in range(HG_HEADS)] for d in range(2)] for s in range(n_seq)]
    results = []
    for s in range(n_seq):
        o_f, st_f = _hgrn_direction(qf_ref[s], vf_ref[s], ff_ref[s], lb[0:1], af_ref, mf_ref, loaded[s][0], False)
        o_b, st_b = _hgrn_direction(qb_ref[s], vb_ref[s], fb_ref[s], lb[1:2], ab_ref, mb_ref, loaded[s][1], True)
        of_ref[s] = jnp.concatenate(o_f, axis=1)
        ob_ref[s] = jnp.concatenate(o_b, axis=1)
        results.append((st_f, st_b))
    for s in range(n_seq):
        for h in range(HG_HEADS):
            st_scr[s, 0, h] = results[s][0][h]
            st_scr[s, 1, h] = results[s][1][h]

    if has_sout:
        @pl.when(ci == pl.num_programs(1) - 1)
        def _():
            for s in range(n_seq):
                for d in range(2):
                    for h in range(HG_HEADS):
                        sout_ref[s, d, h] = st_scr[s, d, h].T


def _hgrn(proj, hg_lb, tables, layer, s0, want_state):
    b_sz, n, _ = proj.shape
    nc = n // HG_CHUNK
    a_f, a_b, m_f, m_b = tables
    c = HG_CHUNK
    sb = HG_SEQS
    fwd = lambda col: pl.BlockSpec((sb, c, HG_QK), lambda b, i: (b, i, col))
    bwd = lambda col: pl.BlockSpec((sb, c, HG_QK), lambda b, i: (b, nc - 1 - i, col))
    const = lambda shape: pl.BlockSpec(shape, lambda b, i: (0,) * len(shape))
    in_specs = [fwd(COL_HQ), fwd(COL_HV), fwd(COL_HFF), bwd(COL_HQ), bwd(COL_HV), bwd(COL_HFB),
                const(hg_lb.shape), const(a_f.shape), const(a_b.shape), const(m_f.shape), const(m_b.shape)]
    args = [proj, proj, proj, proj, proj, proj, hg_lb, a_f, a_b, m_f, m_b]
    state_spec = pl.BlockSpec((sb, 2, HG_HEADS, HG_DK, HG_DV), lambda b, i: (b, 0, 0, 0, 0))
    if s0 is not None:
        in_specs.append(pl.BlockSpec((sb, 1, 2, HG_HEADS, HG_DK, HG_DV), lambda b, i: (b, layer, 0, 0, 0, 0)))
        args.append(s0)
    out_specs = [pl.BlockSpec((sb, c, HG_WIDTH), lambda b, i: (b, i, 0)),
                 pl.BlockSpec((sb, c, HG_WIDTH), lambda b, i: (b, nc - 1 - i, 0))]
    out_shape = [jax.ShapeDtypeStruct((b_sz, n, HG_WIDTH), f32)] * 2
    if want_state:
        out_specs.append(state_spec)
        out_shape.append(jax.ShapeDtypeStruct((b_sz, 2, HG_HEADS, HG_DK, HG_DV), f32))
    return pl.pallas_call(
        functools.partial(_hgrn_kernel, layer, s0 is not None, want_state),
        grid=(b_sz // sb, nc),
        in_specs=in_specs,
        out_specs=out_specs,
        out_shape=out_shape,
        scratch_shapes=[pltpu.VMEM((sb, 2, HG_HEADS, HG_DV, HG_DK), f32)],
        compiler_params=_params(("parallel", "arbitrary")),
        name="hgrn",
    )(*args)


def _rope_rotate(x, cos, sin):
    lane = lax.broadcasted_iota(jnp.int32, x.shape, 1)
    first_half = (lane % (MLA_ROPE // 2)) < (MLA_ROPE // 4)
    quarter = MLA_ROPE // 4
    rot = jnp.where(first_half, -pltpu.roll(x, LANES - quarter, 1), pltpu.roll(x, quarter, 1))
    return x * cos + rot * sin


def _mla_kernel(n_ctx, use_rope, want_cache, tq, *refs):
    qa_ref, kvr_ref, wq_ref, wkt_ref, wv_ref, qn_ref, kvn_ref = refs[:7]
    pos = 7
    if n_ctx:
        cckv_ref, ckr_ref = refs[pos], refs[pos + 1]
        pos += 2
    if use_rope:
        cosq_ref, sinq_ref, cosk_ref, sink_ref = refs[pos:pos + 4]
        pos += 4
    y_ref = refs[pos]
    pos += 1
    if want_cache:
        ockv_ref, okr_ref = refs[pos], refs[pos + 1]
        pos += 2
    kcat_scr, v_scr, o_scr = refs[pos:pos + 3]
    n = kvr_ref.shape[1]
    rows = 512 if n % 512 == 0 else n

    def expand(r0, nrows, c_kv, krp):
        k_t = _dot_nt(wkt_ref[0], c_kv)
        eye = (lax.broadcasted_iota(jnp.int32, (LANES, LANES), 0)
               == lax.broadcasted_iota(jnp.int32, (LANES, LANES), 1)).astype(bf16)
        krp_t = _dot_nt(eye, krp).astype(bf16)
        vals = _dot(c_kv, wv_ref[0])
        ones = jnp.ones((nrows, MLA_V), bf16)
        for h in range(MLA_HEADS):
            kcat_scr[h, 0:MLA_NOPE, r0:r0 + nrows] = k_t[h * MLA_NOPE:(h + 1) * MLA_NOPE].astype(bf16)
            kcat_scr[h, MLA_NOPE:QHEAD_PAD, r0:r0 + nrows] = krp_t
            v_scr[h, r0:r0 + nrows, 0:MLA_V] = vals[:, h * MLA_V:(h + 1) * MLA_V].astype(bf16)
            v_scr[h, r0:r0 + nrows, MLA_V:2 * MLA_V] = ones

    @pl.when(pl.program_id(1) == 0)
    def _():
        if n_ctx:
            zeros = jnp.zeros((n_ctx, LANES - MLA_ROPE), f32)
            krp_ctx = jnp.concatenate([ckr_ref[0, 0], zeros], axis=1).astype(bf16)
            expand(0, n_ctx, cckv_ref[0, 0].astype(bf16), krp_ctx)
        for r0 in range(0, n, rows):
            blk = kvr_ref[0, r0:r0 + rows]
            c_kv = _rms(blk[:, :KV_LORA]) * kvn_ref[...]
            krp = blk[:, KV_LORA:]
            if want_cache:
                ockv_ref[0, r0:r0 + rows] = c_kv
                okr_ref[0, r0:r0 + rows] = krp[:, :MLA_ROPE]
            if use_rope:
                krp = _rope_rotate(krp, cosk_ref[r0:r0 + rows], sink_ref[r0:r0 + rows])
            expand(n_ctx + r0, rows, c_kv.astype(bf16), krp.astype(bf16))

    scale = float(MLA_NOPE + MLA_ROPE) ** -0.5
    qn = (_rms(qa_ref[0]) * qn_ref[...]).astype(bf16)
    sk = kcat_scr.shape[2]
    n_kc = 2 if sk > MLA_KEY_CHUNK else 1
    kc = sk // n_kc

    def head(h):
        q = _dot(qn, wq_ref[0, h]) * scale
        q_rope = q[:, MLA_NOPE:]
        if use_rope:
            q_rope = _rope_rotate(q_rope, cosq_ref[...], sinq_ref[...])
        q_cat = jnp.concatenate([q[:, :MLA_NOPE], q_rope], axis=1).astype(bf16)
        m_run = acc = None
        for c in range(n_kc):
            s = _dot(q_cat, kcat_scr[h, :, c * kc:(c + 1) * kc])
            m_c = jnp.max(s, axis=-1, keepdims=True)
            m_new = m_c if c == 0 else jnp.maximum(m_run, m_c)
            p = jnp.exp((s - m_new).astype(bf16))
            pv = _dot(p, v_scr[h, c * kc:(c + 1) * kc])
            acc = pv if c == 0 else jnp.exp(m_run - m_new) * acc + pv
            m_run = m_new
        return (acc[:, :MLA_V] / acc[:, MLA_V:]).astype(bf16)

    if n_kc == 1:
        for h in range(MLA_HEADS):
            y_ref[0, :, h * MLA_V:(h + 1) * MLA_V] = head(h)
    else:
        def body(h, carry):
            o_scr[h] = head(h)
            return carry

        lax.fori_loop(0, MLA_HEADS, body, 0, unroll=4)
        for h in range(MLA_HEADS):
            y_ref[0, :, h * MLA_V:(h + 1) * MLA_V] = o_scr[h]


def _mla(proj, wq, wkt, wv, q_norm, kv_norm, ctx, layer, rope, want_cache, tq):
    b_sz, n, _ = proj.shape
    n_ctx = 0 if ctx is None else ctx[0].shape[2]
    sk = n_ctx + n
    single = pl.Buffered(1)
    const = lambda shape: pl.BlockSpec(shape, lambda b, i: (0,) * len(shape), pipeline_mode=single)
    in_specs = [
        pl.BlockSpec((1, tq, Q_LORA), lambda b, i: (b, i, COL_QA)),
        pl.BlockSpec((1, n, KVR_WIDTH), lambda b, i: (b, 0, COL_KVR), pipeline_mode=single),
        pl.BlockSpec((1,) + wq.shape[1:], lambda b, i: (layer, 0, 0, 0), pipeline_mode=single),
        pl.BlockSpec((1,) + wkt.shape[1:], lambda b, i: (layer, 0, 0), pipeline_mode=single),
        pl.BlockSpec((1,) + wv.shape[1:], lambda b, i: (layer, 0, 0), pipeline_mode=single),
        const(q_norm.shape), const(kv_norm.shape),
    ]
    args = [proj, proj, wq, wkt, wv, q_norm, kv_norm]
    if ctx is not None:
        in_specs += [
            pl.BlockSpec((1, 1, n_ctx, KV_LORA), lambda b, i: (b, layer, 0, 0), pipeline_mode=single),
            pl.BlockSpec((1, 1, n_ctx, MLA_ROPE), lambda b, i: (b, layer, 0, 0), pipeline_mode=single)]
        args += [ctx[0], ctx[1]]
    if rope is not None:
        cos, sin = rope
        in_specs += [pl.BlockSpec((tq, LANES), lambda b, i: (i, 0)),
                     pl.BlockSpec((tq, LANES), lambda b, i: (i, 0)),
                     const(cos.shape), const(sin.shape)]
        args += [cos, sin, cos, sin]
    out_specs = [pl.BlockSpec((1, tq, MLA_WIDTH), lambda b, i: (b, i, 0))]
    out_shape = [jax.ShapeDtypeStruct((b_sz, n, MLA_WIDTH), bf16)]
    if want_cache:
        out_specs += [pl.BlockSpec((1, n, KV_LORA), lambda b, i: (b, 0, 0)),
                      pl.BlockSpec((1, n, MLA_ROPE), lambda b, i: (b, 0, 0))]
        out_shape += [jax.ShapeDtypeStruct((b_sz, n, KV_LORA), f32),
                      jax.ShapeDtypeStruct((b_sz, n, MLA_ROPE), f32)]
    return pl.pallas_call(
        functools.partial(_mla_kernel, n_ctx, rope is not None, want_cache, tq),
        grid=(b_sz, n // tq),
        in_specs=in_specs,
        out_specs=out_specs,
        out_shape=out_shape,
        scratch_shapes=[pltpu.VMEM((MLA_HEADS, QHEAD_PAD, sk), bf16), pltpu.VMEM((MLA_HEADS, sk, 2 * MLA_V), bf16),
                        pltpu.VMEM((MLA_HEADS, tq, MLA_V), bf16)],
        compiler_params=_params(("parallel", "arbitrary")),
        name="mla",
    )(*args)


def _rope_tables(n):
    t = np.arange(n)
    r = (t // GRID_W).astype(np.float32)
    col = (t % GRID_W).astype(np.float32)
    nf = MLA_ROPE // 4
    inv = np.float32(ROPE_BASE) ** (-np.arange(nf, dtype=np.float32) / np.float32(nf))
    ar = r[:, None] * inv
    ac = col[:, None] * inv
    ang = np.concatenate([ar, ar, ac, ac], axis=-1).astype(np.float32)
    pad = LANES - MLA_ROPE
    cos = np.concatenate([np.cos(ang), np.ones((n, pad), np.float32)], axis=-1)
    sin = np.concatenate([np.sin(ang), np.zeros((n, pad), np.float32)], axis=-1)
    return jnp.asarray(cos, dtype=f32), jnp.asarray(sin, dtype=f32)


def _outproj_kernel(x_ref, yfn_ref, of_ref, ob_ref, gate_ref, ymla_ref, w_ref, hgg_ref, gpost_ref,
                    mod_ref, o_ref):
    tm = x_ref.shape[1]
    half = tm // 2
    for r0 in range(0, tm, half):
        rows = slice(r0, r0 + half)
        o_hg = of_ref[0, rows] + ob_ref[0, rows]
        gate = gate_ref[0, rows]
        gate = gate * jax.nn.sigmoid(gate)
        parts = [yfn_ref[0, rows]]
        for h in range(HG_HEADS):
            sl = slice(h * HG_DV, (h + 1) * HG_DV)
            parts.append((_rms(o_hg[:, sl]) * hgg_ref[:, sl] * gate[:, sl]).astype(bf16))
        parts.append(ymla_ref[0, rows])
        y = _dot(jnp.concatenate(parts, axis=1), w_ref[0])
        o_ref[0, rows] = x_ref[0, rows] + mod_ref[0, 2:3, :] * (_rms(y) * gpost_ref[...])


def _outproj(x, y_fn, o_f, o_b, proj, y_mla, w_out, hg_gain, g_post, mod, layer, tm):
    b_sz, n, _ = x.shape
    tok = lambda width, col=0: pl.BlockSpec((1, tm, width), lambda b, i: (b, i, col))
    return pl.pallas_call(
        _outproj_kernel,
        grid=(b_sz, n // tm),
        in_specs=[
            tok(D_MODEL), tok(FN_WIDTH), tok(HG_WIDTH), tok(HG_WIDTH), tok(HG_WIDTH, COL_HGATE),
            tok(MLA_WIDTH),
            pl.BlockSpec((1, D_MODEL, D_MODEL), lambda b, i: (layer, 0, 0)),
            pl.BlockSpec((1, HG_WIDTH), lambda b, i: (0, 0)),
            pl.BlockSpec((1, D_MODEL), lambda b, i: (0, 0)),
            pl.BlockSpec((1, 6, D_MODEL), lambda b, i: (b, 0, 0)),
        ],
        out_specs=tok(D_MODEL),
        out_shape=jax.ShapeDtypeStruct((b_sz, n, D_MODEL), f32),
        compiler_params=_params(("parallel", "parallel")),
        name="outproj",
    )(x, y_fn, o_f, o_b, proj, y_mla, w_out, hg_gain, g_post, mod)


def _ffn_kernel(x_ref, mod_ref, gpre_ref, gpost_ref, w1_ref, w2_ref, o_ref, h_scr):
    k = pl.program_id(2)

    @pl.when(k == 0)
    def _():
        _modulated_norm(x_ref, gpre_ref, mod_ref[0, 4:5, :], mod_ref[0, 3:4, :], h_scr)
        o_ref[...] = jnp.zeros_like(o_ref)

    a = jnp.maximum(_dot(h_scr[...], w1_ref[0]), 0.0)
    o_ref[0] += _dot((a * a).astype(bf16), w2_ref[0])

    @pl.when(k == pl.num_programs(2) - 1)
    def _():
        o_ref[0] = x_ref[0] + mod_ref[0, 5:6, :] * (_rms(o_ref[0]) * gpost_ref[...])


def _ffn(x, mod, g_pre, g_post, w1, w2, layer, tm, tf):
    b_sz, n, _ = x.shape
    return pl.pallas_call(
        _ffn_kernel,
        grid=(b_sz, n // tm, D_FF // tf),
        in_specs=[
            pl.BlockSpec((1, tm, D_MODEL), lambda b, i, k: (b, i, 0)),
            pl.BlockSpec((1, 6, D_MODEL), lambda b, i, k: (b, 0, 0)),
            pl.BlockSpec((1, D_MODEL), lambda b, i, k: (0, 0)),
            pl.BlockSpec((1, D_MODEL), lambda b, i, k: (0, 0)),
            pl.BlockSpec((1, D_MODEL, tf), lambda b, i, k: (layer, 0, k)),
            pl.BlockSpec((1, tf, D_MODEL), lambda b, i, k: (layer, k, 0)),
        ],
        out_specs=pl.BlockSpec((1, tm, D_MODEL), lambda b, i, k: (b, i, 0)),
        out_shape=jax.ShapeDtypeStruct((b_sz, n, D_MODEL), f32),
        scratch_shapes=[pltpu.VMEM((tm, D_MODEL), bf16)],
        compiler_params=_params(("parallel", "parallel", "arbitrary")),
        name="ffn",
    )(x, mod, g_pre, g_post, w1, w2)


def _layer(x, mod, wl, hg_tables, dft, layer, n_seq, ctx, s0, rope, is_context, tq):
    b_sz, n, _ = x.shape
    n_seqs = b_sz * n // n_seq
    proj = _inproj(x, mod, wl["g_pre_mix"], wl["w_in"], layer, INPROJ_TM).reshape(n_seqs, n_seq, IN_COLS_PAD)
    y_fn = _fourier(proj, dft[0], dft[1], dft[2], min(n_seq, 512))
    hg = _hgrn(proj, wl["hg_lb"], hg_tables, layer, s0, is_context)
    mla = _mla(proj, wl["wq"], wl["wkt"], wl["wv"], wl["q_norm"], wl["kv_norm"], ctx, layer, rope, is_context,
               tq)
    flat = lambda a: a.reshape(b_sz, n, a.shape[-1])
    x = _outproj(x, flat(y_fn), flat(hg[0]), flat(hg[1]), flat(proj), flat(mla[0]), wl["w_out"],
                 wl["hg_gain"], wl["g_post_mix"], mod, layer, OUTPROJ_TM)
    x = _ffn(x, mod, wl["g_pre_ff"], wl["g_post_ff"], wl["w_ff1"], wl["w_ff2"], layer, FFN_TM, FFN_TF)
    extras = (mla[1], mla[2], hg[2]) if is_context else None
    return x, extras


def kernel(x_prompt, x_sample, c, cache_ckv, cache_krope, state_hgrn, c_ctx, w_ada, b_ada, g_pre_mix,
           g_post_mix, g_pre_ff, g_post_ff, w_in, hg_lb, hg_gain, mla_q_norm, mla_kv_norm, w_q_b, w_kv_b,
           w_out, w_ff1, w_ff2):
    batch, seq, _ = x_prompt.shape
    dec_batch, dec_seq, _ = x_sample.shape

    rows = 16
    cond = jnp.concatenate([c_ctx[None, :], c, jnp.zeros((rows - 1 - dec_batch, D_MODEL), f32)], axis=0)
    mod = _ada(cond, w_ada, b_ada).reshape(DEPTH, rows, 6, D_MODEL)

    hg_tables = _hgrn_tables()
    ccs = _channel_tables()
    dft_p = (ccs,) + _dft_tables(seq)
    dft_s = (ccs,) + _dft_tables(dec_seq)
    rope = _rope_tables(dec_seq)

    yp = x_prompt.reshape(1, batch * seq, D_MODEL)
    ys = x_sample
    ckv_list, kr_list, st_list = [], [], []
    wq = w_q_b.reshape(DEPTH, Q_LORA, MLA_HEADS, MLA_NOPE + MLA_ROPE).astype(bf16)
    wq = jnp.pad(wq, ((0, 0), (0, 0), (0, 0), (0, QHEAD_PAD - MLA_NOPE - MLA_ROPE)))
    wkv = w_kv_b.astype(bf16).reshape(DEPTH, KV_LORA, MLA_HEADS, MLA_NOPE + MLA_V)
    stacked = {
        "w_in": jnp.pad(w_in.astype(bf16), ((0, 0), (0, 0), (0, IN_COLS_PAD - IN_COLS))),
        "wq": wq.transpose(0, 2, 1, 3),
        "wkt": wkv[:, :, :, :MLA_NOPE].reshape(DEPTH, KV_LORA, MLA_HEADS * MLA_NOPE).transpose(0, 2, 1),
        "wv": wkv[:, :, :, MLA_NOPE:].reshape(DEPTH, KV_LORA, MLA_HEADS * MLA_V),
        "w_out": w_out.astype(bf16),
        "w_ff1": w_ff1.astype(bf16),
        "w_ff2": w_ff2.astype(bf16),
    }
    for l in range(DEPTH):
        wl = {
            **stacked,
            "g_pre_mix": g_pre_mix[l][None, :], "g_post_mix": g_post_mix[l][None, :],
            "g_pre_ff": g_pre_ff[l][None, :], "g_post_ff": g_post_ff[l][None, :],
            "hg_lb": hg_lb, "hg_gain": hg_gain[l][None, :],
            "q_norm": mla_q_norm[l][None, :], "kv_norm": mla_kv_norm[l][None, :],
        }
        yp, (ckv_l, kr_l, st_l) = _layer(yp, mod[l, 0:1], wl, hg_tables, dft_p, l, seq, None, None, None,
                                         True, seq)
        ckv_list.append(ckv_l)
        kr_list.append(kr_l)
        st_list.append(st_l)
        ys, _ = _layer(ys, mod[l, 1:1 + dec_batch], wl, hg_tables, dft_s, l, dec_seq,
                       (cache_ckv, cache_krope), state_hgrn, rope, False, 512)

    return (yp.reshape(batch, seq, D_MODEL), ys, jnp.stack(ckv_list, axis=1), jnp.stack(kr_list, axis=1),
            jnp.stack(st_list, axis=1))
```

```python
import functools

import numpy as np
import jax
import jax.numpy as jnp
from jax import lax
from jax.experimental import pallas as pl
from jax.experimental.pallas import tpu as pltpu

f32 = jnp.float32
bf16 = jnp.bfloat16

D_MODEL = 2048
DEPTH = 2
GRID_W = 64
FN_HEADS = 4
FN_DH = 128
FN_WIDTH = FN_HEADS * FN_DH
HG_HEADS = 4
HG_DK = 128
HG_DV = 128
HG_QK = HG_HEADS * HG_DK
HG_WIDTH = HG_HEADS * HG_DV
MLA_HEADS = 8
MLA_NOPE = 128
MLA_ROPE = 64
MLA_V = 128
Q_LORA = 768
KV_LORA = 512
MLA_WIDTH = MLA_HEADS * MLA_V
D_FF = 4 * D_MODEL
ROPE_BASE = 10000.0
EPS = 1e-6
F_FLOOR = 1e-30

LANES = 128
SUBLANES = 8
ROW_CHUNK = 2 * SUBLANES
MXU_DIM = 256
IN_COLS = 4416
IN_COLS_PAD = 4608
IN_TILE = 768
COL_HQ, COL_HV, COL_HFF, COL_HFB, COL_HGATE = 1, 2, 3, 4, 5
COL_QA = 4
KVR_WIDTH = 640
COL_KVR = 6
QHEAD_PAD = 256
HG_CHUNK = 128
HG_LEVELS = (64, 32, 16, 8, 4, 2, 1)
HG_SPLIT = 2
HG_SEQS = 4
INPROJ_TM = 1024
OUTPROJ_TM = 512
MLA_KEY_CHUNK = 1280
FFN_TM = 1024
FFN_TF = 512
VMEM_LIMIT = 56 * 1024 * 1024


def _params(semantics):
    return pltpu.CompilerParams(dimension_semantics=semantics, vmem_limit_bytes=VMEM_LIMIT)


def _dot(a, b):
    return jnp.dot(a, b, preferred_element_type=f32)


def _dot_nt(a, b):
    return lax.dot_general(a, b, (((1,), (1,)), ((), ())), preferred_element_type=f32)


def _dot_tn(a, b):
    return lax.dot_general(a, b, (((0,), (0,)), ((), ())), preferred_element_type=f32)


def _rms(x):
    return x * lax.rsqrt(jnp.mean(x * x, axis=-1, keepdims=True) + EPS)


def _row_loop(n_rows, fn):
    def body(i, carry):
        fn(pl.ds(pl.multiple_of(i * ROW_CHUNK, ROW_CHUNK), ROW_CHUNK))
        return carry

    lax.fori_loop(0, n_rows // ROW_CHUNK, body, 0, unroll=4)


def _modulated_norm(x_ref, g_ref, scale_row, shift_row, h_scr):
    gain = g_ref[...] * (1.0 + scale_row)

    def rows(r):
        h_scr[r, :] = (_rms(x_ref[0, r, :]) * gain + shift_row).astype(bf16)

    _row_loop(h_scr.shape[0], rows)


def _ada_kernel(cond_ref, w_ref, b_ref, o_ref):
    cnd = cond_ref[...]
    act = cnd * jax.nn.sigmoid(cnd)
    o_ref[0] = _dot(act.astype(bf16), w_ref[0].astype(bf16)) + b_ref[0]


def _ada(cond, w_ada, b_ada):
    rows = cond.shape[0]
    tn = 1024
    return pl.pallas_call(
        _ada_kernel,
        grid=(DEPTH, 6 * D_MODEL // tn),
        in_specs=[
            pl.BlockSpec((rows, D_MODEL), lambda l, j: (0, 0)),
            pl.BlockSpec((1, D_MODEL, tn), lambda l, j: (l, 0, j)),
            pl.BlockSpec((1, 1, tn), lambda l, j: (l, 0, j)),
        ],
        out_specs=pl.BlockSpec((1, rows, tn), lambda l, j: (l, 0, j)),
        out_shape=jax.ShapeDtypeStruct((DEPTH, rows, 6 * D_MODEL), f32),
        compiler_params=_params(("parallel", "parallel")),
        name="ada_mod",
    )(cond, w_ada, b_ada.reshape(DEPTH, 1, 6 * D_MODEL))


def _inproj_kernel(x_ref, mod_ref, g_ref, w_ref, o_ref, h_scr):
    @pl.when(pl.program_id(2) == 0)
    def _():
        _modulated_norm(x_ref, g_ref, mod_ref[0, 1:2, :], mod_ref[0, 0:1, :], h_scr)

    o_ref[0] = _dot(h_scr[...], w_ref[0])


def _inproj(x, mod, g, w_in, layer, tm):
    b_sz, n, _ = x.shape
    return pl.pallas_call(
        _inproj_kernel,
        grid=(b_sz, n // tm, IN_COLS_PAD // IN_TILE),
        in_specs=[
            pl.BlockSpec((1, tm, D_MODEL), lambda b, i, j: (b, i, 0)),
            pl.BlockSpec((1, 6, D_MODEL), lambda b, i, j: (b, 0, 0)),
            pl.BlockSpec((1, D_MODEL), lambda b, i, j: (0, 0)),
            pl.BlockSpec((1, D_MODEL, IN_TILE), lambda b, i, j: (layer, 0, j)),
        ],
        out_specs=pl.BlockSpec((1, tm, IN_TILE), lambda b, i, j: (b, i, j)),
        out_shape=jax.ShapeDtypeStruct((b_sz, n, IN_COLS_PAD), f32),
        scratch_shapes=[pltpu.VMEM((tm, D_MODEL), bf16)],
        compiler_params=_params(("parallel", "parallel", "arbitrary")),
        name="inproj",
    )(x, mod, g, w_in)


def _fourier_kernel(u_ref, ccs_ref, cn_ref, sn_ref, o_ref, ucs_scr):
    @pl.when(pl.program_id(1) == 0)
    def _():
        ucs_scr[...] = _dot(u_ref[0].astype(bf16), ccs_ref[...]).astype(bf16)

    y = _dot(cn_ref[...], ucs_scr[:, :FN_WIDTH]) + _dot(sn_ref[...], ucs_scr[:, FN_WIDTH:])
    o_ref[0] = y.astype(bf16)


def _fourier(proj, ccs, cn, sn_neg, tr):
    b_sz, n, _ = proj.shape
    return pl.pallas_call(
        _fourier_kernel,
        grid=(b_sz, n // tr),
        in_specs=[
            pl.BlockSpec((1, n, FN_WIDTH), lambda b, i: (b, 0, 0)),
            pl.BlockSpec((FN_WIDTH, 2 * FN_WIDTH), lambda b, i: (0, 0)),
            pl.BlockSpec((tr, n), lambda b, i: (i, 0)),
            pl.BlockSpec((tr, n), lambda b, i: (i, 0)),
        ],
        out_specs=pl.BlockSpec((1, tr, FN_WIDTH), lambda b, i: (b, i, 0)),
        out_shape=jax.ShapeDtypeStruct((b_sz, n, FN_WIDTH), bf16),
        scratch_shapes=[pltpu.VMEM((n, 2 * FN_WIDTH), bf16)],
        compiler_params=_params(("parallel", "arbitrary")),
        name="fourier",
    )(proj, ccs, cn, sn_neg)


def _dft_tables(n):
    j = np.arange(n, dtype=np.int64)
    ang = ((j[:, None] * j[None, :]) % n).astype(np.float64) * (2.0 * np.pi / n)
    scale_n = 1.0 / np.sqrt(n)
    cn = jnp.asarray(np.cos(ang) * scale_n, dtype=bf16)
    sn_neg = jnp.asarray(-np.sin(ang) * scale_n, dtype=bf16)
    return cn, sn_neg


def _channel_tables():
    k = np.arange(FN_DH)
    ang = 2.0 * np.pi * ((k[:, None] * k[None, :]) % FN_DH) / FN_DH
    eye = np.eye(FN_HEADS)
    cc = np.kron(eye, np.cos(ang)) / np.sqrt(FN_DH)
    sc = np.kron(eye, np.sin(ang)) / np.sqrt(FN_DH)
    return jnp.asarray(np.concatenate([cc, sc], axis=1), dtype=bf16)


def _hgrn_tables():
    c = HG_CHUNK
    idx = np.arange(c)
    blocks = [(idx[None, :] <= idx[:, None])]
    masks = []
    for h in HG_LEVELS:
        mid = (idx // (2 * h)) * (2 * h) + h
        upper = idx >= mid
        if h < SUBLANES:
            row_up = (idx[None, :] >= mid[:, None]) & (idx[None, :] <= idx[:, None])
            row_lo = (idx[None, :] > idx[:, None]) & (idx[None, :] < mid[:, None])
            blocks.append(np.where(upper[:, None], row_up, row_lo))
        same = (idx[:, None] // (2 * h)) == (idx[None, :] // (2 * h))
        masks.append(same & upper[:, None] & ~upper[None, :])
    masks.append(np.eye(c, dtype=bool))
    a_f = np.concatenate([b.astype(np.float32) for b in blocks], axis=0)
    a_b = np.concatenate([b[::-1, ::-1].astype(np.float32) for b in blocks], axis=0)
    m_f = np.stack([m.astype(np.float32) for m in masks])
    m_b = np.stack([m[::-1, ::-1].astype(np.float32) for m in masks])
    rep = lambda a: jnp.asarray(np.concatenate([a] * HG_SPLIT, axis=1), dtype=bf16)
    return rep(a_f), rep(a_b), jnp.asarray(m_f), jnp.asarray(m_b)


def _block_row(x, h, row):
    c, w = x.shape
    xb = x.reshape(c // (2 * h), 2 * h, w)
    return jnp.broadcast_to(xb[:, row:row + 1, :], xb.shape).reshape(c, w)


def _hgrn_direction(q, v, pre, lb, a_ref, m_ref, states, backward):
    c = HG_CHUNK
    nl = len(HG_LEVELS)
    f = lb + (1.0 - lb) * jax.nn.sigmoid(pre)
    lf = jnp.log(jnp.maximum(f, F_FLOOR))
    kk = 1.0 - f
    pieces = []
    rem = lf
    for _ in range(HG_SPLIT):
        piece = rem.astype(bf16)
        pieces.append(piece)
        rem = rem - piece.astype(f32)
    sums = _dot(a_ref[...], jnp.concatenate(pieces, axis=0))
    cum = sums[0:c]
    total = cum[0:1] if backward else cum[c - 1:c]
    q16 = q.astype(bf16)
    k16 = kk.astype(bf16)
    v16 = v.astype(bf16)
    decay16 = lambda log_decay: jnp.exp(log_decay.astype(bf16))
    q_inter = q16 * decay16(cum)
    k_state = k16 * decay16(jnp.minimum(total - cum, 0.0))
    e_total = jnp.exp(total)
    q_levels, k_levels = [q16], [k16]
    fine = 0
    for h in HG_LEVELS:
        if h >= SUBLANES:
            ref = _block_row(cum, h, h if backward else h - 1)
            el = decay16(-jnp.abs(cum - ref))
        else:
            fine += 1
            el = decay16(sums[fine * c:(fine + 1) * c])
        q_levels.append(q16 * el)
        k_levels.append(k16 * el)
    outs, new_states = [], []
    for h in range(HG_HEADS):
        sl = slice(h * HG_DK, (h + 1) * HG_DK)
        vh = v16[:, sl]
        sc = m_ref[nl] * _dot_nt(q_levels[0][:, sl], k_levels[0][:, sl])
        for li in range(nl):
            sc = sc + m_ref[li] * _dot_nt(q_levels[li + 1][:, sl], k_levels[li + 1][:, sl])
        st = states[h]
        outs.append(_dot(sc.astype(bf16), vh) + _dot_nt(q_inter[:, sl], st.astype(bf16)))
        new_states.append(e_total[:, sl] * st + _dot_tn(vh, k_state[:, sl]))
    return outs, new_states


def _hgrn_kernel(layer, has_s0, has_sout, *refs):
    (qf_ref, vf_ref, ff_ref, qb_ref, vb_ref, fb_ref, lb_ref, af_ref, ab_ref, mf_ref, mb_ref) = refs[:11]
    pos = 11
    s0_ref = None
    if has_s0:
        s0_ref = refs[pos]
        pos += 1
    of_ref, ob_ref = refs[pos], refs[pos + 1]
    pos += 2
    sout_ref = None
    if has_sout:
        sout_ref = refs[pos]
        pos += 1
    st_scr = refs[pos]
    ci = pl.program_id(1)

    n_seq = qf_ref.shape[0]

    @pl.when(ci == 0)
    def _():
        for s in range(n_seq):
            for d in range(2):
                for h in range(HG_HEADS):
                    if has_s0:
                        st_scr[s, d, h] = s0_ref[s, 0, d, h].T
                    else:
                        st_scr[s, d, h] = jnp.zeros((HG_DV, HG_DK), f32)

    raw = lb_ref[...]
    mx = raw[0]
    for i in range(1, DEPTH):
        mx = jnp.maximum(mx, raw[i])
    ex = [jnp.exp(raw[i] - mx) for i in range(DEPTH)]
    den = ex[0]
    for i in range(1, DEPTH):
        den = den + ex[i]
    lb = jnp.zeros_like(den)
    for i in range(1, layer + 1):
        lb = lb + ex[i] / den

    loaded = [[[st_scr[s, d, h] for h in range(HG_HEADS)] for d in range(2)] for s in range(n_seq)]
    results = []
    for s in range(n_seq):
        o_f, st_f = _hgrn_direction(qf_ref[s], vf_ref[s], ff_ref[s], lb[0:1], af_ref, mf_ref, loaded[s][0], False)
        o_b, st_b = _hgrn_direction(qb_ref[s], vb_ref[s], fb_ref[s], lb[1:2], ab_ref, mb_ref, loaded[s][1], True)
        of_ref[s] = jnp.concatenate(o_f, axis=1)
        ob_ref[s] = jnp.concatenate(o_b, axis=1)
        results.append((st_f, st_b))
    for s in range(n_seq):
        for h in range(HG_HEADS):
            st_scr[s, 0, h] = results[s][0][h]
            st_scr[s, 1, h] = results[s][1][h]

    if has_sout:
        @pl.when(ci == pl.num_programs(1) - 1)
        def _():
            for s in range(n_seq):
                for d in range(2):
                    for h in range(HG_HEADS):
                        sout_ref[s, d, h] = st_scr[s, d, h].T


def _hgrn(proj, hg_lb, tables, layer, s0, want_state):
    b_sz, n, _ = proj.shape
    nc = n // HG_CHUNK
    a_f, a_b, m_f, m_b = tables
    c = HG_CHUNK
    sb = HG_SEQS
    fwd = lambda col: pl.BlockSpec((sb, c, HG_QK), lambda b, i: (b, i, col))
    bwd = lambda col: pl.BlockSpec((sb, c, HG_QK), lambda b, i: (b, nc - 1 - i, col))
    const = lambda shape: pl.BlockSpec(shape, lambda b, i: (0,) * len(shape))
    in_specs = [fwd(COL_HQ), fwd(COL_HV), fwd(COL_HFF), bwd(COL_HQ), bwd(COL_HV), bwd(COL_HFB),
                const(hg_lb.shape), const(a_f.shape), const(a_b.shape), const(m_f.shape), const(m_b.shape)]
    args = [proj, proj, proj, proj, proj, proj, hg_lb, a_f, a_b, m_f, m_b]
    state_spec = pl.BlockSpec((sb, 2, HG_HEADS, HG_DK, HG_DV), lambda b, i: (b, 0, 0, 0, 0))
    if s0 is not None:
        in_specs.append(pl.BlockSpec((sb, 1, 2, HG_HEADS, HG_DK, HG_DV), lambda b, i: (b, layer, 0, 0, 0, 0)))
        args.append(s0)
    out_specs = [pl.BlockSpec((sb, c, HG_WIDTH), lambda b, i: (b, i, 0)),
                 pl.BlockSpec((sb, c, HG_WIDTH), lambda b, i: (b, nc - 1 - i, 0))]
    out_shape = [jax.ShapeDtypeStruct((b_sz, n, HG_WIDTH), f32)] * 2
    if want_state:
        out_specs.append(state_spec)
        out_shape.append(jax.ShapeDtypeStruct((b_sz, 2, HG_HEADS, HG_DK, HG_DV), f32))
    return pl.pallas_call(
        functools.partial(_hgrn_kernel, layer, s0 is not None, want_state),
        grid=(b_sz // sb, nc),
        in_specs=in_specs,
        out_specs=out_specs,
        out_shape=out_shape,
        scratch_shapes=[pltpu.VMEM((sb, 2, HG_HEADS, HG_DV, HG_DK), f32)],
        compiler_params=_params(("parallel", "arbitrary")),
        name="hgrn",
    )(*args)


def _rope_rotate(x, cos, sin):
    lane = lax.broadcasted_iota(jnp.int32, x.shape, 1)
    first_half = (lane % (MLA_ROPE // 2)) < (MLA_ROPE // 4)
    quarter = MLA_ROPE // 4
    rot = jnp.where(first_half, -pltpu.roll(x, LANES - quarter, 1), pltpu.roll(x, quarter, 1))
    return x * cos + rot * sin


def _mla_kernel(n_ctx, use_rope, want_cache, tq, *refs):
    qa_ref, kvr_ref, wq_ref, wkt_ref, wv_ref, qn_ref, kvn_ref = refs[:7]
    pos = 7
    if n_ctx:
        cckv_ref, ckr_ref = refs[pos], refs[pos + 1]
        pos += 2
    if use_rope:
        cosq_ref, sinq_ref, cosk_ref, sink_ref = refs[pos:pos + 4]
        pos += 4
    y_ref = refs[pos]
    pos += 1
    if want_cache:
        ockv_ref, okr_ref = refs[pos], refs[pos + 1]
        pos += 2
    kcat_scr, v_scr, o_scr = refs[pos:pos + 3]
    n = kvr_ref.shape[1]
    rows = 512 if n % 512 == 0 else n

    def expand(r0, nrows, c_kv, krp):
        k_t = _dot_nt(wkt_ref[0], c_kv)
        eye = (lax.broadcasted_iota(jnp.int32, (LANES, LANES), 0)
               == lax.broadcasted_iota(jnp.int32, (LANES, LANES), 1)).astype(bf16)
        krp_t = _dot_nt(eye, krp).astype(bf16)
        vals = _dot(c_kv, wv_ref[0])
        ones = jnp.ones((nrows, MLA_V), bf16)
        for h in range(MLA_HEADS):
            kcat_scr[h, 0:MLA_NOPE, r0:r0 + nrows] = k_t[h * MLA_NOPE:(h + 1) * MLA_NOPE].astype(bf16)
            kcat_scr[h, MLA_NOPE:QHEAD_PAD, r0:r0 + nrows] = krp_t
            v_scr[h, r0:r0 + nrows, 0:MLA_V] = vals[:, h * MLA_V:(h + 1) * MLA_V].astype(bf16)
            v_scr[h, r0:r0 + nrows, MLA_V:2 * MLA_V] = ones

    @pl.when(pl.program_id(1) == 0)
    def _():
        if n_ctx:
            zeros = jnp.zeros((n_ctx, LANES - MLA_ROPE), f32)
            krp_ctx = jnp.concatenate([ckr_ref[0, 0], zeros], axis=1).astype(bf16)
            expand(0, n_ctx, cckv_ref[0, 0].astype(bf16), krp_ctx)
        for r0 in range(0, n, rows):
            blk = kvr_ref[0, r0:r0 + rows]
            c_kv = _rms(blk[:, :KV_LORA]) * kvn_ref[...]
            krp = blk[:, KV_LORA:]
            if want_cache:
                ockv_ref[0, r0:r0 + rows] = c_kv
                okr_ref[0, r0:r0 + rows] = krp[:, :MLA_ROPE]
            if use_rope:
                krp = _rope_rotate(krp, cosk_ref[r0:r0 + rows], sink_ref[r0:r0 + rows])
            expand(n_ctx + r0, rows, c_kv.astype(bf16), krp.astype(bf16))

    scale = float(MLA_NOPE + MLA_ROPE) ** -0.5
    qn = (_rms(qa_ref[0]) * qn_ref[...]).astype(bf16)
    sk = kcat_scr.shape[2]
    n_kc = 1 if sk <= MLA_KEY_CHUNK else max(2, (sk * tq) // (MLA_KEY_CHUNK * 512))
    kc = sk // n_kc

    def head(h):
        q = _dot(qn, wq_ref[0, h]) * scale
        q_rope = q[:, MLA_NOPE:]
        if use_rope:
            q_rope = _rope_rotate(q_rope, cosq_ref[...], sinq_ref[...])
        q_cat = jnp.concatenate([q[:, :MLA_NOPE], q_rope], axis=1).astype(bf16)
        m_run = acc = None
        for c in range(n_kc):
            s = _dot(q_cat, kcat_scr[h, :, c * kc:(c + 1) * kc])
            m_c = jnp.max(s, axis=-1, keepdims=True)
            m_new = m_c if c == 0 else jnp.maximum(m_run, m_c)
            p = jnp.exp((s - m_new).astype(bf16))
            pv = _dot(p, v_scr[h, c * kc:(c + 1) * kc])
            acc = pv if c == 0 else jnp.exp(m_run - m_new) * acc + pv
            m_run = m_new
        return (acc[:, :MLA_V] / acc[:, MLA_V:]).astype(bf16)

    if n_kc == 1:
        for h in range(MLA_HEADS):
            y_ref[0, :, h * MLA_V:(h + 1) * MLA_V] = head(h)
    else:
        def body(h, carry):
            o_scr[h] = head(h)
            return carry

        lax.fori_loop(0, MLA_HEADS, body, 0, unroll=4)
        for h in range(MLA_HEADS):
            y_ref[0, :, h * MLA_V:(h + 1) * MLA_V] = o_scr[h]


def _mla(proj, wq, wkt, wv, q_norm, kv_norm, ctx, layer, rope, want_cache, tq):
    b_sz, n, _ = proj.shape
    n_ctx = 0 if ctx is None else ctx[0].shape[2]
    sk = n_ctx + n
    single = pl.Buffered(1)
    const = lambda shape: pl.BlockSpec(shape, lambda b, i: (0,) * len(shape), pipeline_mode=single)
    in_specs = [
        pl.BlockSpec((1, tq, Q_LORA), lambda b, i: (b, i, COL_QA)),
        pl.BlockSpec((1, n, KVR_WIDTH), lambda b, i: (b, 0, COL_KVR), pipeline_mode=single),
        pl.BlockSpec((1,) + wq.shape[1:], lambda b, i: (layer, 0, 0, 0), pipeline_mode=single),
        pl.BlockSpec((1,) + wkt.shape[1:], lambda b, i: (layer, 0, 0), pipeline_mode=single),
        pl.BlockSpec((1,) + wv.shape[1:], lambda b, i: (layer, 0, 0), pipeline_mode=single),
        const(q_norm.shape), const(kv_norm.shape),
    ]
    args = [proj, proj, wq, wkt, wv, q_norm, kv_norm]
    if ctx is not None:
        in_specs += [
            pl.BlockSpec((1, 1, n_ctx, KV_LORA), lambda b, i: (b, layer, 0, 0), pipeline_mode=single),
            pl.BlockSpec((1, 1, n_ctx, MLA_ROPE), lambda b, i: (b, layer, 0, 0), pipeline_mode=single)]
        args += [ctx[0], ctx[1]]
    if rope is not None:
        cos, sin = rope
        in_specs += [pl.BlockSpec((tq, LANES), lambda b, i: (i, 0)),
                     pl.BlockSpec((tq, LANES), lambda b, i: (i, 0)),
                     const(cos.shape), const(sin.shape)]
        args += [cos, sin, cos, sin]
    out_specs = [pl.BlockSpec((1, tq, MLA_WIDTH), lambda b, i: (b, i, 0))]
    out_shape = [jax.ShapeDtypeStruct((b_sz, n, MLA_WIDTH), bf16)]
    if want_cache:
        out_specs += [pl.BlockSpec((1, n, KV_LORA), lambda b, i: (b, 0, 0)),
                      pl.BlockSpec((1, n, MLA_ROPE), lambda b, i: (b, 0, 0))]
        out_shape += [jax.ShapeDtypeStruct((b_sz, n, KV_LORA), f32),
                      jax.ShapeDtypeStruct((b_sz, n, MLA_ROPE), f32)]
    return pl.pallas_call(
        functools.partial(_mla_kernel, n_ctx, rope is not None, want_cache, tq),
        grid=(b_sz, n // tq),
        in_specs=in_specs,
        out_specs=out_specs,
        out_shape=out_shape,
        scratch_shapes=[pltpu.VMEM((MLA_HEADS, QHEAD_PAD, sk), bf16), pltpu.VMEM((MLA_HEADS, sk, 2 * MLA_V), bf16),
                        pltpu.VMEM((MLA_HEADS, tq, MLA_V), bf16)],
        compiler_params=_params(("parallel", "arbitrary")),
        name="mla",
    )(*args)


def _rope_tables(n):
    t = np.arange(n)
    r = (t // GRID_W).astype(np.float32)
    col = (t % GRID_W).astype(np.float32)
    nf = MLA_ROPE // 4
    inv = np.float32(ROPE_BASE) ** (-np.arange(nf, dtype=np.float32) / np.float32(nf))
    ar = r[:, None] * inv
    ac = col[:, None] * inv
    ang = np.concatenate([ar, ar, ac, ac], axis=-1).astype(np.float32)
    pad = LANES - MLA_ROPE
    cos = np.concatenate([np.cos(ang), np.ones((n, pad), np.float32)], axis=-1)
    sin = np.concatenate([np.sin(ang), np.zeros((n, pad), np.float32)], axis=-1)
    return jnp.asarray(cos, dtype=f32), jnp.asarray(sin, dtype=f32)


def _outproj_kernel(x_ref, yfn_ref, of_ref, ob_ref, gate_ref, ymla_ref, w_ref, hgg_ref, gpost_ref,
                    mod_ref, o_ref):
    tm = x_ref.shape[1]
    half = tm // 2
    for r0 in range(0, tm, half):
        rows = slice(r0, r0 + half)
        o_hg = of_ref[0, rows] + ob_ref[0, rows]
        gate = gate_ref[0, rows]
        gate = gate * jax.nn.sigmoid(gate)
        parts = [yfn_ref[0, rows]]
        for h in range(HG_HEADS):
            sl = slice(h * HG_DV, (h + 1) * HG_DV)
            parts.append((_rms(o_hg[:, sl]) * hgg_ref[:, sl] * gate[:, sl]).astype(bf16))
        parts.append(ymla_ref[0, rows])
        y = _dot(jnp.concatenate(parts, axis=1), w_ref[0])
        o_ref[0, rows] = x_ref[0, rows] + mod_ref[0, 2:3, :] * (_rms(y) * gpost_ref[...])


def _outproj(x, y_fn, o_f, o_b, proj, y_mla, w_out, hg_gain, g_post, mod, layer, tm):
    b_sz, n, _ = x.shape
    tok = lambda width, col=0: pl.BlockSpec((1, tm, width), lambda b, i: (b, i, col))
    return pl.pallas_call(
        _outproj_kernel,
        grid=(b_sz, n // tm),
        in_specs=[
            tok(D_MODEL), tok(FN_WIDTH), tok(HG_WIDTH), tok(HG_WIDTH), tok(HG_WIDTH, COL_HGATE),
            tok(MLA_WIDTH),
            pl.BlockSpec((1, D_MODEL, D_MODEL), lambda b, i: (layer, 0, 0)),
            pl.BlockSpec((1, HG_WIDTH), lambda b, i: (0, 0)),
            pl.BlockSpec((1, D_MODEL), lambda b, i: (0, 0)),
            pl.BlockSpec((1, 6, D_MODEL), lambda b, i: (b, 0, 0)),
        ],
        out_specs=tok(D_MODEL),
        out_shape=jax.ShapeDtypeStruct((b_sz, n, D_MODEL), f32),
        compiler_params=_params(("parallel", "parallel")),
        name="outproj",
    )(x, y_fn, o_f, o_b, proj, y_mla, w_out, hg_gain, g_post, mod)


def _ffn_kernel(x_ref, mod_ref, gpre_ref, gpost_ref, w1_ref, w2_ref, o_ref, h_scr, inv_scr):
    k = pl.program_id(2)

    def hidden_slice():
        a = jnp.maximum(_dot(h_scr[...], w1_ref[0]), 0.0)
        return _dot((a * a).astype(bf16), w2_ref[0])

    @pl.when(k == 0)
    def _():
        _modulated_norm(x_ref, gpre_ref, mod_ref[0, 4:5, :], mod_ref[0, 3:4, :], h_scr)
        o_ref[0] = hidden_slice()

    @pl.when(k > 0)
    def _():
        o_ref[0] += hidden_slice()

    @pl.when(k == pl.num_programs(2) - 1)
    def _():
        f = o_ref[0]
        inv_scr[...] = lax.rsqrt(jnp.mean(f * f, axis=-1, keepdims=True) + EPS)
        gain = mod_ref[0, 5:6, :] * gpost_ref[...]
        o_ref[0] = x_ref[0] + o_ref[0] * inv_scr[...] * gain


def _ffn(x, mod, g_pre, g_post, w1, w2, layer, tm, tf):
    b_sz, n, _ = x.shape
    return pl.pallas_call(
        _ffn_kernel,
        grid=(b_sz, n // tm, D_FF // tf),
        in_specs=[
            pl.BlockSpec((1, tm, D_MODEL), lambda b, i, k: (b, i, 0)),
            pl.BlockSpec((1, 6, D_MODEL), lambda b, i, k: (b, 0, 0)),
            pl.BlockSpec((1, D_MODEL), lambda b, i, k: (0, 0)),
            pl.BlockSpec((1, D_MODEL), lambda b, i, k: (0, 0)),
            pl.BlockSpec((1, D_MODEL, tf), lambda b, i, k: (layer, 0, k)),
            pl.BlockSpec((1, tf, D_MODEL), lambda b, i, k: (layer, k, 0)),
        ],
        out_specs=pl.BlockSpec((1, tm, D_MODEL), lambda b, i, k: (b, i, 0)),
        out_shape=jax.ShapeDtypeStruct((b_sz, n, D_MODEL), f32),
        scratch_shapes=[pltpu.VMEM((tm, D_MODEL), bf16), pltpu.VMEM((tm, 1), f32)],
        compiler_params=_params(("parallel", "parallel", "arbitrary")),
        name="ffn",
    )(x, mod, g_pre, g_post, w1, w2)


def _layer(x, mod, wl, hg_tables, dft, layer, n_seq, ctx, s0, rope, is_context, tq):
    b_sz, n, _ = x.shape
    n_seqs = b_sz * n // n_seq
    proj = _inproj(x, mod, wl["g_pre_mix"], wl["w_in"], layer, INPROJ_TM).reshape(n_seqs, n_seq, IN_COLS_PAD)
    y_fn = _fourier(proj, dft[0], dft[1], dft[2], min(n_seq, 512))
    hg = _hgrn(proj, wl["hg_lb"], hg_tables, layer, s0, is_context)
    mla = _mla(proj, wl["wq"], wl["wkt"], wl["wv"], wl["q_norm"], wl["kv_norm"], ctx, layer, rope, is_context,
               tq)
    flat = lambda a: a.reshape(b_sz, n, a.shape[-1])
    x = _outproj(x, flat(y_fn), flat(hg[0]), flat(hg[1]), flat(proj), flat(mla[0]), wl["w_out"],
                 wl["hg_gain"], wl["g_post_mix"], mod, layer, OUTPROJ_TM)
    x = _ffn(x, mod, wl["g_pre_ff"], wl["g_post_ff"], wl["w_ff1"], wl["w_ff2"], layer, FFN_TM, FFN_TF)
    extras = (mla[1], mla[2], hg[2]) if is_context else None
    return x, extras


def kernel(x_prompt, x_sample, c, cache_ckv, cache_krope, state_hgrn, c_ctx, w_ada, b_ada, g_pre_mix,
           g_post_mix, g_pre_ff, g_post_ff, w_in, hg_lb, hg_gain, mla_q_norm, mla_kv_norm, w_q_b, w_kv_b,
           w_out, w_ff1, w_ff2):
    batch, seq, _ = x_prompt.shape
    dec_batch, dec_seq, _ = x_sample.shape

    rows = 16
    cond = jnp.concatenate([c_ctx[None, :], c, jnp.zeros((rows - 1 - dec_batch, D_MODEL), f32)], axis=0)
    mod = _ada(cond, w_ada, b_ada).reshape(DEPTH, rows, 6, D_MODEL)

    hg_tables = _hgrn_tables()
    ccs = _channel_tables()
    dft_p = (ccs,) + _dft_tables(seq)
    dft_s = (ccs,) + _dft_tables(dec_seq)
    rope = _rope_tables(dec_seq)

    yp = x_prompt.reshape(1, batch * seq, D_MODEL)
    ys = x_sample
    ckv_list, kr_list, st_list = [], [], []
    wq = w_q_b.reshape(DEPTH, Q_LORA, MLA_HEADS, MLA_NOPE + MLA_ROPE).astype(bf16)
    wq = jnp.pad(wq, ((0, 0), (0, 0), (0, 0), (0, QHEAD_PAD - MLA_NOPE - MLA_ROPE)))
    wkv = w_kv_b.astype(bf16).reshape(DEPTH, KV_LORA, MLA_HEADS, MLA_NOPE + MLA_V)
    stacked = {
        "w_in": jnp.pad(w_in.astype(bf16), ((0, 0), (0, 0), (0, IN_COLS_PAD - IN_COLS))),
        "wq": wq.transpose(0, 2, 1, 3),
        "wkt": wkv[:, :, :, :MLA_NOPE].reshape(DEPTH, KV_LORA, MLA_HEADS * MLA_NOPE).transpose(0, 2, 1),
        "wv": wkv[:, :, :, MLA_NOPE:].reshape(DEPTH, KV_LORA, MLA_HEADS * MLA_V),
        "w_out": w_out.astype(bf16),
        "w_ff1": w_ff1.astype(bf16),
        "w_ff2": w_ff2.astype(bf16),
    }
    for l in range(DEPTH):
        wl = {
            **stacked,
            "g_pre_mix": g_pre_mix[l][None, :], "g_post_mix": g_post_mix[l][None, :],
            "g_pre_ff": g_pre_ff[l][None, :], "g_post_ff": g_post_ff[l][None, :],
            "hg_lb": hg_lb, "hg_gain": hg_gain[l][None, :],
            "q_norm": mla_q_norm[l][None, :], "kv_norm": mla_kv_norm[l][None, :],
        }
        yp, (ckv_l, kr_l, st_l) = _layer(yp, mod[l, 0:1], wl, hg_tables, dft_p, l, seq, None, None, None,
                                         True, seq)
        ckv_list.append(ckv_l)
        kr_list.append(kr_l)
        st_list.append(st_l)
        ys, _ = _layer(ys, mod[l, 1:1 + dec_batch], wl, hg_tables, dft_s, l, dec_seq,
                       (cache_ckv, cache_krope), state_hgrn, rope, False, 1024)

    return (yp.reshape(batch, seq, D_MODEL), ys, jnp.stack(ckv_list, axis=1), jnp.stack(kr_list, axis=1),
            jnp.stack(st_list, axis=1))
```

```python
import functools

import numpy as np
import jax
import jax.numpy as jnp
from jax import lax
from jax.experimental import pallas as pl
from jax.experimental.pallas import tpu as pltpu

f32 = jnp.float32
bf16 = jnp.bfloat16

D_MODEL = 2048
DEPTH = 2
GRID_W = 64
FN_HEADS = 4
FN_DH = 128
FN_WIDTH = FN_HEADS * FN_DH
HG_HEADS = 4
HG_DK = 128
HG_DV = 128
HG_QK = HG_HEADS * HG_DK
HG_WIDTH = HG_HEADS * HG_DV
MLA_HEADS = 8
MLA_NOPE = 128
MLA_ROPE = 64
MLA_V = 128
Q_LORA = 768
KV_LORA = 512
MLA_WIDTH = MLA_HEADS * MLA_V
D_FF = 4 * D_MODEL
ROPE_BASE = 10000.0
EPS = 1e-6
F_FLOOR = 1e-30

LANES = 128
SUBLANES = 8
MXU_DIM = 256
IN_COLS = 4416
IN_COLS_PAD = 4608
IN_TILE = 768
COL_HQ, COL_HV, COL_HFF, COL_HFB, COL_HGATE = 1, 2, 3, 4, 5
COL_QA = 4
KVR_WIDTH = 640
COL_KVR = 6
QHEAD_PAD = 256
HG_CHUNK = 128
HG_LEVELS = (64, 32, 16, 8, 4, 2, 1)
HG_SPLIT = 2
HG_SEQS = 4
INPROJ_TM = 1024
OUTPROJ_TM = 512
MLA_KEY_CHUNK = 1280
FFN_TM = 1024
FFN_TF = 512
VMEM_LIMIT = 56 * 1024 * 1024


def _params(semantics):
    return pltpu.CompilerParams(dimension_semantics=semantics, vmem_limit_bytes=VMEM_LIMIT)


def _dot(a, b):
    return jnp.dot(a, b, preferred_element_type=f32)


def _dot_nt(a, b):
    return lax.dot_general(a, b, (((1,), (1,)), ((), ())), preferred_element_type=f32)


def _dot_tn(a, b):
    return lax.dot_general(a, b, (((0,), (0,)), ((), ())), preferred_element_type=f32)


def _rms(x):
    return x * lax.rsqrt(jnp.mean(x * x, axis=-1, keepdims=True) + EPS)


def _modulated_norm(x_ref, g_ref, scale_row, shift_row, h_scr, inv_scr):
    x = x_ref[0]
    inv_scr[...] = lax.rsqrt(jnp.mean(x * x, axis=-1, keepdims=True) + EPS)
    gain = g_ref[...] * (1.0 + scale_row)
    h_scr[...] = (x_ref[0] * inv_scr[...] * gain + shift_row).astype(bf16)


def _ada_kernel(cond_ref, w_ref, b_ref, o_ref):
    cnd = cond_ref[...]
    act = cnd * jax.nn.sigmoid(cnd)
    o_ref[0] = _dot(act.astype(bf16), w_ref[0].astype(bf16)) + b_ref[0]


def _ada(cond, w_ada, b_ada):
    rows = cond.shape[0]
    tn = 1024
    return pl.pallas_call(
        _ada_kernel,
        grid=(DEPTH, 6 * D_MODEL // tn),
        in_specs=[
            pl.BlockSpec((rows, D_MODEL), lambda l, j: (0, 0)),
            pl.BlockSpec((1, D_MODEL, tn), lambda l, j: (l, 0, j)),
            pl.BlockSpec((1, 1, tn), lambda l, j: (l, 0, j)),
        ],
        out_specs=pl.BlockSpec((1, rows, tn), lambda l, j: (l, 0, j)),
        out_shape=jax.ShapeDtypeStruct((DEPTH, rows, 6 * D_MODEL), f32),
        compiler_params=_params(("parallel", "parallel")),
        name="ada_mod",
    )(cond, w_ada, b_ada.reshape(DEPTH, 1, 6 * D_MODEL))


def _inproj_kernel(x_ref, mod_ref, g_ref, w_ref, o_ref, h_scr, inv_scr):
    @pl.when(pl.program_id(2) == 0)
    def _():
        _modulated_norm(x_ref, g_ref, mod_ref[0, 1:2, :], mod_ref[0, 0:1, :], h_scr, inv_scr)
        o_ref[0] = _dot(h_scr[...], w_ref[0])

    @pl.when(pl.program_id(2) > 0)
    def _():
        o_ref[0] = _dot(h_scr[...], w_ref[0])


def _inproj(x, mod, g, w_in, layer, tm):
    b_sz, n, _ = x.shape
    return pl.pallas_call(
        _inproj_kernel,
        grid=(b_sz, n // tm, IN_COLS_PAD // IN_TILE),
        in_specs=[
            pl.BlockSpec((1, tm, D_MODEL), lambda b, i, j: (b, i, 0)),
            pl.BlockSpec((1, 6, D_MODEL), lambda b, i, j: (b, 0, 0)),
            pl.BlockSpec((1, D_MODEL), lambda b, i, j: (0, 0)),
            pl.BlockSpec((1, D_MODEL, IN_TILE), lambda b, i, j: (layer, 0, j)),
        ],
        out_specs=pl.BlockSpec((1, tm, IN_TILE), lambda b, i, j: (b, i, j)),
        out_shape=jax.ShapeDtypeStruct((b_sz, n, IN_COLS_PAD), f32),
        scratch_shapes=[pltpu.VMEM((tm, D_MODEL), bf16), pltpu.VMEM((tm, 1), f32)],
        compiler_params=_params(("parallel", "parallel", "arbitrary")),
        name="inproj",
    )(x, mod, g, w_in)


def _fourier_kernel(u_ref, ccs_ref, cn_ref, sn_ref, o_ref, ucs_scr):
    @pl.when(pl.program_id(1) == 0)
    def _():
        ucs_scr[...] = _dot(u_ref[0].astype(bf16), ccs_ref[...]).astype(bf16)

    y = _dot(cn_ref[...], ucs_scr[:, :FN_WIDTH]) + _dot(sn_ref[...], ucs_scr[:, FN_WIDTH:])
    o_ref[0] = y.astype(bf16)


def _fourier(proj, ccs, cn, sn_neg, tr):
    b_sz, n, _ = proj.shape
    return pl.pallas_call(
        _fourier_kernel,
        grid=(b_sz, n // tr),
        in_specs=[
            pl.BlockSpec((1, n, FN_WIDTH), lambda b, i: (b, 0, 0)),
            pl.BlockSpec((FN_WIDTH, 2 * FN_WIDTH), lambda b, i: (0, 0)),
            pl.BlockSpec((tr, n), lambda b, i: (i, 0)),
            pl.BlockSpec((tr, n), lambda b, i: (i, 0)),
        ],
        out_specs=pl.BlockSpec((1, tr, FN_WIDTH), lambda b, i: (b, i, 0)),
        out_shape=jax.ShapeDtypeStruct((b_sz, n, FN_WIDTH), bf16),
        scratch_shapes=[pltpu.VMEM((n, 2 * FN_WIDTH), bf16)],
        compiler_params=_params(("parallel", "arbitrary")),
        name="fourier",
    )(proj, ccs, cn, sn_neg)


def _dft_tables(n):
    j = np.arange(n, dtype=np.int64)
    ang = ((j[:, None] * j[None, :]) % n).astype(np.float64) * (2.0 * np.pi / n)
    scale_n = 1.0 / np.sqrt(n)
    cn = jnp.asarray(np.cos(ang) * scale_n, dtype=bf16)
    sn_neg = jnp.asarray(-np.sin(ang) * scale_n, dtype=bf16)
    return cn, sn_neg


def _channel_tables():
    k = np.arange(FN_DH)
    ang = 2.0 * np.pi * ((k[:, None] * k[None, :]) % FN_DH) / FN_DH
    eye = np.eye(FN_HEADS)
    cc = np.kron(eye, np.cos(ang)) / np.sqrt(FN_DH)
    sc = np.kron(eye, np.sin(ang)) / np.sqrt(FN_DH)
    return jnp.asarray(np.concatenate([cc, sc], axis=1), dtype=bf16)


def _hgrn_tables():
    c = HG_CHUNK
    idx = np.arange(c)
    blocks = [(idx[None, :] <= idx[:, None])]
    masks = []
    for h in HG_LEVELS:
        mid = (idx // (2 * h)) * (2 * h) + h
        upper = idx >= mid
        if h < SUBLANES:
            row_up = (idx[None, :] >= mid[:, None]) & (idx[None, :] <= idx[:, None])
            row_lo = (idx[None, :] > idx[:, None]) & (idx[None, :] < mid[:, None])
            blocks.append(np.where(upper[:, None], row_up, row_lo))
        same = (idx[:, None] // (2 * h)) == (idx[None, :] // (2 * h))
        masks.append(same & upper[:, None] & ~upper[None, :])
    masks.append(np.eye(c, dtype=bool))
    a_f = np.concatenate([b.astype(np.float32) for b in blocks], axis=0)
    a_b = np.concatenate([b[::-1, ::-1].astype(np.float32) for b in blocks], axis=0)
    m_f = np.stack([m.astype(np.float32) for m in masks])
    m_b = np.stack([m[::-1, ::-1].astype(np.float32) for m in masks])
    rep = lambda a: jnp.asarray(np.concatenate([a] * HG_SPLIT, axis=1), dtype=bf16)
    return rep(a_f), rep(a_b), jnp.asarray(m_f), jnp.asarray(m_b)


def _block_row(x, h, row):
    c, w = x.shape
    xb = x.reshape(c // (2 * h), 2 * h, w)
    return jnp.broadcast_to(xb[:, row:row + 1, :], xb.shape).reshape(c, w)


def _hgrn_direction(q, v, pre, lb, a_ref, m_ref, states, backward):
    c = HG_CHUNK
    nl = len(HG_LEVELS)
    f = lb + (1.0 - lb) * jax.nn.sigmoid(pre)
    lf = jnp.log(jnp.maximum(f, F_FLOOR))
    kk = 1.0 - f
    pieces = []
    rem = lf
    for _ in range(HG_SPLIT):
        piece = rem.astype(bf16)
        pieces.append(piece)
        rem = rem - piece.astype(f32)
    sums = _dot(a_ref[...], jnp.concatenate(pieces, axis=0))
    cum = sums[0:c]
    total = cum[0:1] if backward else cum[c - 1:c]
    q16 = q.astype(bf16)
    k16 = kk.astype(bf16)
    v16 = v.astype(bf16)
    decay16 = lambda log_decay: jnp.exp(log_decay.astype(bf16))
    q_inter = q16 * decay16(cum)
    k_state = k16 * decay16(jnp.minimum(total - cum, 0.0))
    e_total = jnp.exp(total)
    q_levels, k_levels = [q16], [k16]
    fine = 0
    for h in HG_LEVELS:
        if h >= SUBLANES:
            ref = _block_row(cum, h, h if backward else h - 1)
            el = decay16(-jnp.abs(cum - ref))
        else:
            fine += 1
            el = decay16(sums[fine * c:(fine + 1) * c])
        q_levels.append(q16 * el)
        k_levels.append(k16 * el)
    outs, new_states = [], []
    for h in range(HG_HEADS):
        sl = slice(h * HG_DK, (h + 1) * HG_DK)
        vh = v16[:, sl]
        sc = m_ref[nl] * _dot_nt(q_levels[0][:, sl], k_levels[0][:, sl])
        for li in range(nl):
            sc = sc + m_ref[li] * _dot_nt(q_levels[li + 1][:, sl], k_levels[li + 1][:, sl])
        st = states[h]
        outs.append(_dot(sc.astype(bf16), vh) + _dot_nt(q_inter[:, sl], st.astype(bf16)))
        new_states.append(e_total[:, sl] * st + _dot_tn(vh, k_state[:, sl]))
    return outs, new_states


def _hgrn_kernel(layer, has_s0, has_sout, *refs):
    (qf_ref, vf_ref, ff_ref, qb_ref, vb_ref, fb_ref, lb_ref, af_ref, ab_ref, mf_ref, mb_ref) = refs[:11]
    pos = 11
    s0_ref = None
    if has_s0:
        s0_ref = refs[pos]
        pos += 1
    of_ref, ob_ref = refs[pos], refs[pos + 1]
    pos += 2
    sout_ref = None
    if has_sout:
        sout_ref = refs[pos]
        pos += 1
    st_scr = refs[pos]
    ci = pl.program_id(1)

    n_seq = qf_ref.shape[0]

    @pl.when(ci == 0)
    def _():
        for s in range(n_seq):
            for d in range(2):
                for h in range(HG_HEADS):
                    if has_s0:
                        st_scr[s, d, h] = s0_ref[s, 0, d, h].T
                    else:
                        st_scr[s, d, h] = jnp.zeros((HG_DV, HG_DK), f32)

    raw = lb_ref[...]
    mx = raw[0]
    for i in range(1, DEPTH):
        mx = jnp.maximum(mx, raw[i])
    ex = [jnp.exp(raw[i] - mx) for i in range(DEPTH)]
    den = ex[0]
    for i in range(1, DEPTH):
        den = den + ex[i]
    lb = jnp.zeros_like(den)
    for i in range(1, layer + 1):
        lb = lb + ex[i] / den

    loaded = [[[st_scr[s, d, h] for h in range(HG_HEADS)] for d in range(2)] for s in range(n_seq)]
    results = []
    for s in range(n_seq):
        o_f, st_f = _hgrn_direction(qf_ref[s], vf_ref[s], ff_ref[s], lb[0:1], af_ref, mf_ref, loaded[s][0], False)
        o_b, st_b = _hgrn_direction(qb_ref[s], vb_ref[s], fb_ref[s], lb[1:2], ab_ref, mb_ref, loaded[s][1], True)
        of_ref[s] = jnp.concatenate(o_f, axis=1)
        ob_ref[s] = jnp.concatenate(o_b, axis=1)
        results.append((st_f, st_b))
    for s in range(n_seq):
        for h in range(HG_HEADS):
            st_scr[s, 0, h] = results[s][0][h]
            st_scr[s, 1, h] = results[s][1][h]

    if has_sout:
        @pl.when(ci == pl.num_programs(1) - 1)
        def _():
            for s in range(n_seq):
                for d in range(2):
                    for h in range(HG_HEADS):
                        sout_ref[s, d, h] = st_scr[s, d, h].T


def _hgrn(proj, hg_lb, tables, layer, s0, want_state):
    b_sz, n, _ = proj.shape
    nc = n // HG_CHUNK
    a_f, a_b, m_f, m_b = tables
    c = HG_CHUNK
    sb = HG_SEQS
    fwd = lambda col: pl.BlockSpec((sb, c, HG_QK), lambda b, i: (b, i, col))
    bwd = lambda col: pl.BlockSpec((sb, c, HG_QK), lambda b, i: (b, nc - 1 - i, col))
    const = lambda shape: pl.BlockSpec(shape, lambda b, i: (0,) * len(shape))
    in_specs = [fwd(COL_HQ), fwd(COL_HV), fwd(COL_HFF), bwd(COL_HQ), bwd(COL_HV), bwd(COL_HFB),
                const(hg_lb.shape), const(a_f.shape), const(a_b.shape), const(m_f.shape), const(m_b.shape)]
    args = [proj, proj, proj, proj, proj, proj, hg_lb, a_f, a_b, m_f, m_b]
    state_spec = pl.BlockSpec((sb, 2, HG_HEADS, HG_DK, HG_DV), lambda b, i: (b, 0, 0, 0, 0))
    if s0 is not None:
        in_specs.append(pl.BlockSpec((sb, 1, 2, HG_HEADS, HG_DK, HG_DV), lambda b, i: (b, layer, 0, 0, 0, 0)))
        args.append(s0)
    out_specs = [pl.BlockSpec((sb, c, HG_WIDTH), lambda b, i: (b, i, 0)),
                 pl.BlockSpec((sb, c, HG_WIDTH), lambda b, i: (b, nc - 1 - i, 0))]
    out_shape = [jax.ShapeDtypeStruct((b_sz, n, HG_WIDTH), f32)] * 2
    if want_state:
        out_specs.append(state_spec)
        out_shape.append(jax.ShapeDtypeStruct((b_sz, 2, HG_HEADS, HG_DK, HG_DV), f32))
    return pl.pallas_call(
        functools.partial(_hgrn_kernel, layer, s0 is not None, want_state),
        grid=(b_sz // sb, nc),
        in_specs=in_specs,
        out_specs=out_specs,
        out_shape=out_shape,
        scratch_shapes=[pltpu.VMEM((sb, 2, HG_HEADS, HG_DV, HG_DK), f32)],
        compiler_params=_params(("parallel", "arbitrary")),
        name="hgrn",
    )(*args)


def _rope_rotate(x, cos, sin):
    lane = lax.broadcasted_iota(jnp.int32, x.shape, 1)
    first_half = (lane % (MLA_ROPE // 2)) < (MLA_ROPE // 4)
    quarter = MLA_ROPE // 4
    rot = jnp.where(first_half, -pltpu.roll(x, LANES - quarter, 1), pltpu.roll(x, quarter, 1))
    return x * cos + rot * sin


def _mla_kernel(n_ctx, use_rope, want_cache, tq, *refs):
    qa_ref, kvr_ref, wq_ref, wkt_ref, wv_ref, qn_ref, kvn_ref = refs[:7]
    pos = 7
    if n_ctx:
        cckv_ref, ckr_ref = refs[pos], refs[pos + 1]
        pos += 2
    if use_rope:
        cosq_ref, sinq_ref, cosk_ref, sink_ref = refs[pos:pos + 4]
        pos += 4
    y_ref = refs[pos]
    pos += 1
    if want_cache:
        ockv_ref, okr_ref = refs[pos], refs[pos + 1]
        pos += 2
    kcat_scr, v_scr, o_scr = refs[pos:pos + 3]
    n = kvr_ref.shape[1]
    rows = 512 if n % 512 == 0 else n

    def expand(r0, nrows, c_kv, krp):
        k_t = _dot_nt(wkt_ref[0], c_kv)
        eye = (lax.broadcasted_iota(jnp.int32, (LANES, LANES), 0)
               == lax.broadcasted_iota(jnp.int32, (LANES, LANES), 1)).astype(bf16)
        krp_t = _dot_nt(eye, krp).astype(bf16)
        vals = _dot(c_kv, wv_ref[0])
        ones = jnp.ones((nrows, MLA_V), bf16)
        for h in range(MLA_HEADS):
            kcat_scr[h, 0:MLA_NOPE, r0:r0 + nrows] = k_t[h * MLA_NOPE:(h + 1) * MLA_NOPE].astype(bf16)
            kcat_scr[h, MLA_NOPE:QHEAD_PAD, r0:r0 + nrows] = krp_t
            v_scr[h, r0:r0 + nrows, 0:MLA_V] = vals[:, h * MLA_V:(h + 1) * MLA_V].astype(bf16)
            v_scr[h, r0:r0 + nrows, MLA_V:2 * MLA_V] = ones

    @pl.when(pl.program_id(1) == 0)
    def _():
        if n_ctx:
            zeros = jnp.zeros((n_ctx, LANES - MLA_ROPE), f32)
            krp_ctx = jnp.concatenate([ckr_ref[0, 0], zeros], axis=1).astype(bf16)
            expand(0, n_ctx, cckv_ref[0, 0].astype(bf16), krp_ctx)
        for r0 in range(0, n, rows):
            blk = kvr_ref[0, r0:r0 + rows]
            c_kv = _rms(blk[:, :KV_LORA]) * kvn_ref[...]
            krp = blk[:, KV_LORA:]
            if want_cache:
                ockv_ref[0, r0:r0 + rows] = c_kv
                okr_ref[0, r0:r0 + rows] = krp[:, :MLA_ROPE]
            if use_rope:
                krp = _rope_rotate(krp, cosk_ref[r0:r0 + rows], sink_ref[r0:r0 + rows])
            expand(n_ctx + r0, rows, c_kv.astype(bf16), krp.astype(bf16))

    scale = float(MLA_NOPE + MLA_ROPE) ** -0.5
    qn = (_rms(qa_ref[0]) * qn_ref[...]).astype(bf16)
    sk = kcat_scr.shape[2]
    n_kc = 1 if sk <= MLA_KEY_CHUNK else sk // (2 * MXU_DIM)
    kc = sk // n_kc

    def head(h):
        q = _dot(qn, wq_ref[0, h]) * scale
        q_rope = q[:, MLA_NOPE:]
        if use_rope:
            q_rope = _rope_rotate(q_rope, cosq_ref[...], sinq_ref[...])
        q_cat = jnp.concatenate([q[:, :MLA_NOPE], q_rope], axis=1).astype(bf16)
        m_run = acc = None
        for c in range(n_kc):
            s = _dot(q_cat, kcat_scr[h, :, c * kc:(c + 1) * kc])
            m_c = jnp.max(s, axis=-1, keepdims=True)
            m_new = m_c if c == 0 else jnp.maximum(m_run, m_c)
            p = jnp.exp((s - m_new).astype(bf16))
            pv = _dot(p, v_scr[h, c * kc:(c + 1) * kc])
            acc = pv if c == 0 else jnp.exp(m_run - m_new) * acc + pv
            m_run = m_new
        return (acc[:, :MLA_V] / acc[:, MLA_V:]).astype(bf16)

    if n_kc == 1:
        for h in range(MLA_HEADS):
            y_ref[0, :, h * MLA_V:(h + 1) * MLA_V] = head(h)
    else:
        def body(h, carry):
            o_scr[h] = head(h)
            return carry

        lax.fori_loop(0, MLA_HEADS, body, 0, unroll=4)
        for h in range(MLA_HEADS):
            y_ref[0, :, h * MLA_V:(h + 1) * MLA_V] = o_scr[h]


def _mla(proj, wq, wkt, wv, q_norm, kv_norm, ctx, layer, rope, want_cache, tq):
    b_sz, n, _ = proj.shape
    n_ctx = 0 if ctx is None else ctx[0].shape[2]
    sk = n_ctx + n
    single = pl.Buffered(1)
    const = lambda shape: pl.BlockSpec(shape, lambda b, i: (0,) * len(shape), pipeline_mode=single)
    in_specs = [
        pl.BlockSpec((1, tq, Q_LORA), lambda b, i: (b, i, COL_QA)),
        pl.BlockSpec((1, n, KVR_WIDTH), lambda b, i: (b, 0, COL_KVR), pipeline_mode=single),
        pl.BlockSpec((1,) + wq.shape[1:], lambda b, i: (layer, 0, 0, 0), pipeline_mode=single),
        pl.BlockSpec((1,) + wkt.shape[1:], lambda b, i: (layer, 0, 0), pipeline_mode=single),
        pl.BlockSpec((1,) + wv.shape[1:], lambda b, i: (layer, 0, 0), pipeline_mode=single),
        const(q_norm.shape), const(kv_norm.shape),
    ]
    args = [proj, proj, wq, wkt, wv, q_norm, kv_norm]
    if ctx is not None:
        in_specs += [
            pl.BlockSpec((1, 1, n_ctx, KV_LORA), lambda b, i: (b, layer, 0, 0), pipeline_mode=single),
            pl.BlockSpec((1, 1, n_ctx, MLA_ROPE), lambda b, i: (b, layer, 0, 0), pipeline_mode=single)]
        args += [ctx[0], ctx[1]]
    if rope is not None:
        cos, sin = rope
        in_specs += [pl.BlockSpec((tq, LANES), lambda b, i: (i, 0)),
                     pl.BlockSpec((tq, LANES), lambda b, i: (i, 0)),
                     const(cos.shape), const(sin.shape)]
        args += [cos, sin, cos, sin]
    out_specs = [pl.BlockSpec((1, tq, MLA_WIDTH), lambda b, i: (b, i, 0))]
    out_shape = [jax.ShapeDtypeStruct((b_sz, n, MLA_WIDTH), bf16)]
    if want_cache:
        out_specs += [pl.BlockSpec((1, n, KV_LORA), lambda b, i: (b, 0, 0)),
                      pl.BlockSpec((1, n, MLA_ROPE), lambda b, i: (b, 0, 0))]
        out_shape += [jax.ShapeDtypeStruct((b_sz, n, KV_LORA), f32),
                      jax.ShapeDtypeStruct((b_sz, n, MLA_ROPE), f32)]
    return pl.pallas_call(
        functools.partial(_mla_kernel, n_ctx, rope is not None, want_cache, tq),
        grid=(b_sz, n // tq),
        in_specs=in_specs,
        out_specs=out_specs,
        out_shape=out_shape,
        scratch_shapes=[pltpu.VMEM((MLA_HEADS, QHEAD_PAD, sk), bf16), pltpu.VMEM((MLA_HEADS, sk, 2 * MLA_V), bf16),
                        pltpu.VMEM((MLA_HEADS, tq, MLA_V), bf16)],
        compiler_params=_params(("parallel", "arbitrary")),
        name="mla",
    )(*args)


def _rope_tables(n):
    t = np.arange(n)
    r = (t // GRID_W).astype(np.float32)
    col = (t % GRID_W).astype(np.float32)
    nf = MLA_ROPE // 4
    inv = np.float32(ROPE_BASE) ** (-np.arange(nf, dtype=np.float32) / np.float32(nf))
    ar = r[:, None] * inv
    ac = col[:, None] * inv
    ang = np.concatenate([ar, ar, ac, ac], axis=-1).astype(np.float32)
    pad = LANES - MLA_ROPE
    cos = np.concatenate([np.cos(ang), np.ones((n, pad), np.float32)], axis=-1)
    sin = np.concatenate([np.sin(ang), np.zeros((n, pad), np.float32)], axis=-1)
    return jnp.asarray(cos, dtype=f32), jnp.asarray(sin, dtype=f32)


def _outproj_kernel(x_ref, yfn_ref, of_ref, ob_ref, gate_ref, ymla_ref, w_ref, hgg_ref, gpost_ref,
                    mod_ref, o_ref):
    tm = x_ref.shape[1]
    half = tm // 2
    for r0 in range(0, tm, half):
        rows = slice(r0, r0 + half)
        o_hg = of_ref[0, rows] + ob_ref[0, rows]
        gate = gate_ref[0, rows]
        gate = gate * jax.nn.sigmoid(gate)
        parts = [yfn_ref[0, rows]]
        for h in range(HG_HEADS):
            sl = slice(h * HG_DV, (h + 1) * HG_DV)
            parts.append((_rms(o_hg[:, sl]) * hgg_ref[:, sl] * gate[:, sl]).astype(bf16))
        parts.append(ymla_ref[0, rows])
        y = _dot(jnp.concatenate(parts, axis=1), w_ref[0])
        o_ref[0, rows] = x_ref[0, rows] + mod_ref[0, 2:3, :] * (_rms(y) * gpost_ref[...])


def _outproj(x, y_fn, o_f, o_b, proj, y_mla, w_out, hg_gain, g_post, mod, layer, tm):
    b_sz, n, _ = x.shape
    tok = lambda width, col=0: pl.BlockSpec((1, tm, width), lambda b, i: (b, i, col))
    return pl.pallas_call(
        _outproj_kernel,
        grid=(b_sz, n // tm),
        in_specs=[
            tok(D_MODEL), tok(FN_WIDTH), tok(HG_WIDTH), tok(HG_WIDTH), tok(HG_WIDTH, COL_HGATE),
            tok(MLA_WIDTH),
            pl.BlockSpec((1, D_MODEL, D_MODEL), lambda b, i: (layer, 0, 0)),
            pl.BlockSpec((1, HG_WIDTH), lambda b, i: (0, 0)),
            pl.BlockSpec((1, D_MODEL), lambda b, i: (0, 0)),
            pl.BlockSpec((1, 6, D_MODEL), lambda b, i: (b, 0, 0)),
        ],
        out_specs=tok(D_MODEL),
        out_shape=jax.ShapeDtypeStruct((b_sz, n, D_MODEL), f32),
        compiler_params=_params(("parallel", "parallel")),
        name="outproj",
    )(x, y_fn, o_f, o_b, proj, y_mla, w_out, hg_gain, g_post, mod)


def _ffn_kernel(x_ref, mod_ref, gpre_ref, gpost_ref, w1_ref, w2_ref, o_ref, h_scr, inv_scr):
    k = pl.program_id(2)

    def hidden_slice():
        a = jnp.maximum(_dot(h_scr[...], w1_ref[0]), 0.0)
        return _dot((a * a).astype(bf16), w2_ref[0])

    @pl.when(k == 0)
    def _():
        _modulated_norm(x_ref, gpre_ref, mod_ref[0, 4:5, :], mod_ref[0, 3:4, :], h_scr, inv_scr)
        o_ref[0] = hidden_slice()

    last = pl.num_programs(2) - 1

    @pl.when((k > 0) & (k < last))
    def _():
        o_ref[0] += hidden_slice()

    @pl.when(k == last)
    def _():
        o_ref[0] += hidden_slice()
        f = o_ref[0]
        inv_scr[...] = lax.rsqrt(jnp.mean(f * f, axis=-1, keepdims=True) + EPS)
        gain = mod_ref[0, 5:6, :] * gpost_ref[...]
        o_ref[0] = x_ref[0] + o_ref[0] * inv_scr[...] * gain


def _ffn(x, mod, g_pre, g_post, w1, w2, layer, tm, tf):
    b_sz, n, _ = x.shape
    return pl.pallas_call(
        _ffn_kernel,
        grid=(b_sz, n // tm, D_FF // tf),
        in_specs=[
            pl.BlockSpec((1, tm, D_MODEL), lambda b, i, k: (b, i, 0)),
            pl.BlockSpec((1, 6, D_MODEL), lambda b, i, k: (b, 0, 0)),
            pl.BlockSpec((1, D_MODEL), lambda b, i, k: (0, 0)),
            pl.BlockSpec((1, D_MODEL), lambda b, i, k: (0, 0)),
            pl.BlockSpec((1, D_MODEL, tf), lambda b, i, k: (layer, 0, k)),
            pl.BlockSpec((1, tf, D_MODEL), lambda b, i, k: (layer, k, 0)),
        ],
        out_specs=pl.BlockSpec((1, tm, D_MODEL), lambda b, i, k: (b, i, 0)),
        out_shape=jax.ShapeDtypeStruct((b_sz, n, D_MODEL), f32),
        scratch_shapes=[pltpu.VMEM((tm, D_MODEL), bf16), pltpu.VMEM((tm, 1), f32)],
        compiler_params=_params(("parallel", "parallel", "arbitrary")),
        name="ffn",
    )(x, mod, g_pre, g_post, w1, w2)


def _layer(x, mod, wl, hg_tables, dft, layer, n_seq, ctx, s0, rope, is_context, tq):
    b_sz, n, _ = x.shape
    n_seqs = b_sz * n // n_seq
    proj = _inproj(x, mod, wl["g_pre_mix"], wl["w_in"], layer, INPROJ_TM).reshape(n_seqs, n_seq, IN_COLS_PAD)
    y_fn = _fourier(proj, dft[0], dft[1], dft[2], min(n_seq, 512))
    hg = _hgrn(proj, wl["hg_lb"], hg_tables, layer, s0, is_context)
    mla = _mla(proj, wl["wq"], wl["wkt"], wl["wv"], wl["q_norm"], wl["kv_norm"], ctx, layer, rope, is_context,
               tq)
    flat = lambda a: a.reshape(b_sz, n, a.shape[-1])
    x = _outproj(x, flat(y_fn), flat(hg[0]), flat(hg[1]), flat(proj), flat(mla[0]), wl["w_out"],
                 wl["hg_gain"], wl["g_post_mix"], mod, layer, OUTPROJ_TM)
    x = _ffn(x, mod, wl["g_pre_ff"], wl["g_post_ff"], wl["w_ff1"], wl["w_ff2"], layer, FFN_TM, FFN_TF)
    extras = (mla[1], mla[2], hg[2]) if is_context else None
    return x, extras


def kernel(x_prompt, x_sample, c, cache_ckv, cache_krope, state_hgrn, c_ctx, w_ada, b_ada, g_pre_mix,
           g_post_mix, g_pre_ff, g_post_ff, w_in, hg_lb, hg_gain, mla_q_norm, mla_kv_norm, w_q_b, w_kv_b,
           w_out, w_ff1, w_ff2):
    batch, seq, _ = x_prompt.shape
    dec_batch, dec_seq, _ = x_sample.shape

    rows = 16
    cond = jnp.concatenate([c_ctx[None, :], c, jnp.zeros((rows - 1 - dec_batch, D_MODEL), f32)], axis=0)
    mod = _ada(cond, w_ada, b_ada).reshape(DEPTH, rows, 6, D_MODEL)

    hg_tables = _hgrn_tables()
    ccs = _channel_tables()
    dft_p = (ccs,) + _dft_tables(seq)
    dft_s = (ccs,) + _dft_tables(dec_seq)
    rope = _rope_tables(dec_seq)

    yp = x_prompt.reshape(1, batch * seq, D_MODEL)
    ys = x_sample
    ckv_list, kr_list, st_list = [], [], []
    wq = w_q_b.reshape(DEPTH, Q_LORA, MLA_HEADS, MLA_NOPE + MLA_ROPE).astype(bf16)
    wq = jnp.pad(wq, ((0, 0), (0, 0), (0, 0), (0, QHEAD_PAD - MLA_NOPE - MLA_ROPE)))
    wkv = w_kv_b.astype(bf16).reshape(DEPTH, KV_LORA, MLA_HEADS, MLA_NOPE + MLA_V)
    stacked = {
        "w_in": jnp.pad(w_in.astype(bf16), ((0, 0), (0, 0), (0, IN_COLS_PAD - IN_COLS))),
        "wq": wq.transpose(0, 2, 1, 3),
        "wkt": wkv[:, :, :, :MLA_NOPE].reshape(DEPTH, KV_LORA, MLA_HEADS * MLA_NOPE).transpose(0, 2, 1),
        "wv": wkv[:, :, :, MLA_NOPE:].reshape(DEPTH, KV_LORA, MLA_HEADS * MLA_V),
        "w_out": w_out.astype(bf16),
        "w_ff1": w_ff1.astype(bf16),
        "w_ff2": w_ff2.astype(bf16),
    }
    for l in range(DEPTH):
        wl = {
            **stacked,
            "g_pre_mix": g_pre_mix[l][None, :], "g_post_mix": g_post_mix[l][None, :],
            "g_pre_ff": g_pre_ff[l][None, :], "g_post_ff": g_post_ff[l][None, :],
            "hg_lb": hg_lb, "hg_gain": hg_gain[l][None, :],
            "q_norm": mla_q_norm[l][None, :], "kv_norm": mla_kv_norm[l][None, :],
        }
        yp, (ckv_l, kr_l, st_l) = _layer(yp, mod[l, 0:1], wl, hg_tables, dft_p, l, seq, None, None, None,
                                         True, seq)
        ckv_list.append(ckv_l)
        kr_list.append(kr_l)
        st_list.append(st_l)
        ys, _ = _layer(ys, mod[l, 1:1 + dec_batch], wl, hg_tables, dft_s, l, dec_seq,
                       (cache_ckv, cache_krope), state_hgrn, rope, False, 1024)

    return (yp.reshape(batch, seq, D_MODEL), ys, jnp.stack(ckv_list, axis=1), jnp.stack(kr_list, axis=1),
            jnp.stack(st_list, axis=1))
```

```python
import functools

import numpy as np
import jax
import jax.numpy as jnp
from jax import lax
from jax.experimental import pallas as pl
from jax.experimental.pallas import tpu as pltpu

f32 = jnp.float32
bf16 = jnp.bfloat16

D_MODEL = 2048
DEPTH = 2
GRID_W = 64
FN_HEADS = 4
FN_DH = 128
FN_WIDTH = FN_HEADS * FN_DH
HG_HEADS = 4
HG_DK = 128
HG_DV = 128
HG_QK = HG_HEADS * HG_DK
HG_WIDTH = HG_HEADS * HG_DV
MLA_HEADS = 8
MLA_NOPE = 128
MLA_ROPE = 64
MLA_V = 128
Q_LORA = 768
KV_LORA = 512
MLA_WIDTH = MLA_HEADS * MLA_V
D_FF = 4 * D_MODEL
ROPE_BASE = 10000.0
EPS = 1e-6
F_FLOOR = 1e-30

LANES = 128
SUBLANES = 8
MXU_DIM = 256
IN_COLS = 4416
IN_COLS_PAD = 4608
IN_TILE = 1536
COL_HQ, COL_HV, COL_HFF, COL_HFB, COL_HGATE = 1, 2, 3, 4, 5
COL_QA = 4
KVR_WIDTH = 640
COL_KVR = 6
QHEAD_PAD = 256
HG_CHUNK = 128
HG_LEVELS = (64, 32, 16, 8, 4, 2, 1)
HG_SPLIT = 2
HG_SEQS = 4
INPROJ_TM = 1024
OUTPROJ_TM = 512
MLA_KEY_CHUNK = 1280
FFN_TM = 1024
FFN_TF = 512
VMEM_LIMIT = 56 * 1024 * 1024


def _params(semantics):
    return pltpu.CompilerParams(dimension_semantics=semantics, vmem_limit_bytes=VMEM_LIMIT)


def _dot(a, b):
    return jnp.dot(a, b, preferred_element_type=f32)


def _dot_nt(a, b):
    return lax.dot_general(a, b, (((1,), (1,)), ((), ())), preferred_element_type=f32)


def _dot_tn(a, b):
    return lax.dot_general(a, b, (((0,), (0,)), ((), ())), preferred_element_type=f32)


def _rms(x):
    return x * lax.rsqrt(jnp.mean(x * x, axis=-1, keepdims=True) + EPS)


def _modulated_norm(x_ref, g_ref, scale_row, shift_row, h_scr, inv_scr):
    x = x_ref[0]
    inv_scr[...] = lax.rsqrt(jnp.mean(x * x, axis=-1, keepdims=True) + EPS)
    gain = g_ref[...] * (1.0 + scale_row)
    h_scr[...] = (x_ref[0] * inv_scr[...] * gain + shift_row).astype(bf16)


def _ada_kernel(cond_ref, w_ref, b_ref, o_ref):
    cnd = cond_ref[...]
    act = cnd * jax.nn.sigmoid(cnd)
    o_ref[0] = _dot(act.astype(bf16), w_ref[0].astype(bf16)) + b_ref[0]


def _ada(cond, w_ada, b_ada):
    rows = cond.shape[0]
    tn = 1024
    return pl.pallas_call(
        _ada_kernel,
        grid=(DEPTH, 6 * D_MODEL // tn),
        in_specs=[
            pl.BlockSpec((rows, D_MODEL), lambda l, j: (0, 0)),
            pl.BlockSpec((1, D_MODEL, tn), lambda l, j: (l, 0, j)),
            pl.BlockSpec((1, 1, tn), lambda l, j: (l, 0, j)),
        ],
        out_specs=pl.BlockSpec((1, rows, tn), lambda l, j: (l, 0, j)),
        out_shape=jax.ShapeDtypeStruct((DEPTH, rows, 6 * D_MODEL), f32),
        compiler_params=_params(("parallel", "parallel")),
        name="ada_mod",
    )(cond, w_ada, b_ada.reshape(DEPTH, 1, 6 * D_MODEL))


def _inproj_kernel(x_ref, mod_ref, g_ref, w_ref, o_ref, h_scr, inv_scr):
    @pl.when(pl.program_id(2) == 0)
    def _():
        _modulated_norm(x_ref, g_ref, mod_ref[0, 1:2, :], mod_ref[0, 0:1, :], h_scr, inv_scr)
        o_ref[0] = _dot(h_scr[...], w_ref[0])

    @pl.when(pl.program_id(2) > 0)
    def _():
        o_ref[0] = _dot(h_scr[...], w_ref[0])


def _inproj(x, mod, g, w_in, layer, tm):
    b_sz, n, _ = x.shape
    return pl.pallas_call(
        _inproj_kernel,
        grid=(b_sz, n // tm, IN_COLS_PAD // IN_TILE),
        in_specs=[
            pl.BlockSpec((1, tm, D_MODEL), lambda b, i, j: (b, i, 0)),
            pl.BlockSpec((1, 6, D_MODEL), lambda b, i, j: (b, 0, 0)),
            pl.BlockSpec((1, D_MODEL), lambda b, i, j: (0, 0)),
            pl.BlockSpec((1, D_MODEL, IN_TILE), lambda b, i, j: (layer, 0, j)),
        ],
        out_specs=pl.BlockSpec((1, tm, IN_TILE), lambda b, i, j: (b, i, j)),
        out_shape=jax.ShapeDtypeStruct((b_sz, n, IN_COLS_PAD), f32),
        scratch_shapes=[pltpu.VMEM((tm, D_MODEL), bf16), pltpu.VMEM((tm, 1), f32)],
        compiler_params=_params(("parallel", "parallel", "arbitrary")),
        name="inproj",
    )(x, mod, g, w_in)


def _fourier_kernel(u_ref, ccs_ref, cn_ref, sn_ref, o_ref, ucs_scr):
    @pl.when(pl.program_id(1) == 0)
    def _():
        for h in range(FN_HEADS):
            sl = slice(h * FN_DH, (h + 1) * FN_DH)
            r = _dot(u_ref[0, :, sl].astype(bf16), ccs_ref[...]).astype(bf16)
            ucs_scr[:, sl] = r[:, :FN_DH]
            ucs_scr[:, FN_WIDTH + h * FN_DH:FN_WIDTH + (h + 1) * FN_DH] = r[:, FN_DH:]

    y = _dot(cn_ref[...], ucs_scr[:, :FN_WIDTH]) + _dot(sn_ref[...], ucs_scr[:, FN_WIDTH:])
    o_ref[0] = y.astype(bf16)


def _fourier(proj, ccs, cn, sn_neg, tr):
    b_sz, n, _ = proj.shape
    return pl.pallas_call(
        _fourier_kernel,
        grid=(b_sz, n // tr),
        in_specs=[
            pl.BlockSpec((1, n, FN_WIDTH), lambda b, i: (b, 0, 0)),
            pl.BlockSpec((FN_DH, 2 * FN_DH), lambda b, i: (0, 0)),
            pl.BlockSpec((tr, n), lambda b, i: (i, 0)),
            pl.BlockSpec((tr, n), lambda b, i: (i, 0)),
        ],
        out_specs=pl.BlockSpec((1, tr, FN_WIDTH), lambda b, i: (b, i, 0)),
        out_shape=jax.ShapeDtypeStruct((b_sz, n, FN_WIDTH), bf16),
        scratch_shapes=[pltpu.VMEM((n, 2 * FN_WIDTH), bf16)],
        compiler_params=_params(("parallel", "arbitrary")),
        name="fourier",
    )(proj, ccs, cn, sn_neg)


def _dft_tables(n):
    j = np.arange(n, dtype=np.int64)
    ang = ((j[:, None] * j[None, :]) % n).astype(np.float64) * (2.0 * np.pi / n)
    scale_n = 1.0 / np.sqrt(n)
    cn = jnp.asarray(np.cos(ang) * scale_n, dtype=bf16)
    sn_neg = jnp.asarray(-np.sin(ang) * scale_n, dtype=bf16)
    return cn, sn_neg


def _channel_tables():
    k = np.arange(FN_DH)
    ang = 2.0 * np.pi * ((k[:, None] * k[None, :]) % FN_DH) / FN_DH
    table = np.concatenate([np.cos(ang), np.sin(ang)], axis=1) / np.sqrt(FN_DH)
    return jnp.asarray(table, dtype=bf16)


def _hgrn_tables():
    c = HG_CHUNK
    idx = np.arange(c)
    blocks = [(idx[None, :] <= idx[:, None])]
    masks = []
    for h in HG_LEVELS:
        mid = (idx // (2 * h)) * (2 * h) + h
        upper = idx >= mid
        if h < SUBLANES:
            row_up = (idx[None, :] >= mid[:, None]) & (idx[None, :] <= idx[:, None])
            row_lo = (idx[None, :] > idx[:, None]) & (idx[None, :] < mid[:, None])
            blocks.append(np.where(upper[:, None], row_up, row_lo))
        same = (idx[:, None] // (2 * h)) == (idx[None, :] // (2 * h))
        masks.append(same & upper[:, None] & ~upper[None, :])
    masks.append(np.eye(c, dtype=bool))
    a_f = np.concatenate([b.astype(np.float32) for b in blocks], axis=0)
    a_b = np.concatenate([b[::-1, ::-1].astype(np.float32) for b in blocks], axis=0)
    m_f = np.stack([m.astype(np.float32) for m in masks])
    m_b = np.stack([m[::-1, ::-1].astype(np.float32) for m in masks])
    rep = lambda a: jnp.asarray(np.concatenate([a] * HG_SPLIT, axis=1), dtype=bf16)
    return rep(a_f), rep(a_b), jnp.asarray(m_f), jnp.asarray(m_b)


def _block_row(x, h, row):
    c, w = x.shape
    xb = x.reshape(c // (2 * h), 2 * h, w)
    return jnp.broadcast_to(xb[:, row:row + 1, :], xb.shape).reshape(c, w)


def _hgrn_direction(q, v, pre, lb, a_ref, m_ref, states, backward):
    c = HG_CHUNK
    nl = len(HG_LEVELS)
    f = lb + (1.0 - lb) * jax.nn.sigmoid(pre)
    lf = jnp.log(jnp.maximum(f, F_FLOOR))
    kk = 1.0 - f
    pieces = []
    rem = lf
    for _ in range(HG_SPLIT):
        piece = rem.astype(bf16)
        pieces.append(piece)
        rem = rem - piece.astype(f32)
    sums = _dot(a_ref[...], jnp.concatenate(pieces, axis=0))
    cum = sums[0:c]
    total = cum[0:1] if backward else cum[c - 1:c]
    q16 = q.astype(bf16)
    k16 = kk.astype(bf16)
    v16 = v.astype(bf16)
    decay16 = lambda log_decay: jnp.exp(log_decay.astype(bf16))
    q_inter = q16 * decay16(cum)
    k_state = k16 * decay16(jnp.minimum(total - cum, 0.0))
    e_total = jnp.exp(total)
    q_levels, k_levels = [q16], [k16]
    fine = 0
    for h in HG_LEVELS:
        if h >= SUBLANES:
            ref = _block_row(cum, h, h if backward else h - 1)
            el = decay16(-jnp.abs(cum - ref))
        else:
            fine += 1
            el = decay16(sums[fine * c:(fine + 1) * c])
        q_levels.append(q16 * el)
        k_levels.append(k16 * el)
    outs, new_states = [], []
    for h in range(HG_HEADS):
        sl = slice(h * HG_DK, (h + 1) * HG_DK)
        vh = v16[:, sl]
        sc = m_ref[nl] * _dot_nt(q_levels[0][:, sl], k_levels[0][:, sl])
        for li in range(nl):
            sc = sc + m_ref[li] * _dot_nt(q_levels[li + 1][:, sl], k_levels[li + 1][:, sl])
        st = states[h]
        outs.append(_dot(sc.astype(bf16), vh) + _dot_nt(q_inter[:, sl], st.astype(bf16)))
        new_states.append(e_total[:, sl] * st + _dot_tn(vh, k_state[:, sl]))
    return outs, new_states


def _hgrn_kernel(layer, has_s0, has_sout, *refs):
    (qf_ref, vf_ref, ff_ref, qb_ref, vb_ref, fb_ref, lb_ref, af_ref, ab_ref, mf_ref, mb_ref) = refs[:11]
    pos = 11
    s0_ref = None
    if has_s0:
        s0_ref = refs[pos]
        pos += 1
    of_ref, ob_ref = refs[pos], refs[pos + 1]
    pos += 2
    sout_ref = None
    if has_sout:
        sout_ref = refs[pos]
        pos += 1
    st_scr = refs[pos]
    ci = pl.program_id(1)

    n_seq = qf_ref.shape[0]

    @pl.when(ci == 0)
    def _():
        for s in range(n_seq):
            for d in range(2):
                for h in range(HG_HEADS):
                    if has_s0:
                        st_scr[s, d, h] = s0_ref[s, 0, d, h].T
                    else:
                        st_scr[s, d, h] = jnp.zeros((HG_DV, HG_DK), f32)

    raw = lb_ref[...]
    mx = raw[0]
    for i in range(1, DEPTH):
        mx = jnp.maximum(mx, raw[i])
    ex = [jnp.exp(raw[i] - mx) for i in range(DEPTH)]
    den = ex[0]
    for i in range(1, DEPTH):
        den = den + ex[i]
    lb = jnp.zeros_like(den)
    for i in range(1, layer + 1):
        lb = lb + ex[i] / den

    loaded = [[[st_scr[s, d, h] for h in range(HG_HEADS)] for d in range(2)] for s in range(n_seq)]
    results = []
    for s in range(n_seq):
        o_f, st_f = _hgrn_direction(qf_ref[s], vf_ref[s], ff_ref[s], lb[0:1], af_ref, mf_ref, loaded[s][0], False)
        o_b, st_b = _hgrn_direction(qb_ref[s], vb_ref[s], fb_ref[s], lb[1:2], ab_ref, mb_ref, loaded[s][1], True)
        of_ref[s] = jnp.concatenate(o_f, axis=1)
        ob_ref[s] = jnp.concatenate(o_b, axis=1)
        results.append((st_f, st_b))
    for s in range(n_seq):
        for h in range(HG_HEADS):
            st_scr[s, 0, h] = results[s][0][h]
            st_scr[s, 1, h] = results[s][1][h]

    if has_sout:
        @pl.when(ci == pl.num_programs(1) - 1)
        def _():
            for s in range(n_seq):
                for d in range(2):
                    for h in range(HG_HEADS):
                        sout_ref[s, d, h] = st_scr[s, d, h].T


def _hgrn(proj, hg_lb, tables, layer, s0, want_state):
    b_sz, n, _ = proj.shape
    nc = n // HG_CHUNK
    a_f, a_b, m_f, m_b = tables
    c = HG_CHUNK
    sb = HG_SEQS
    fwd = lambda col: pl.BlockSpec((sb, c, HG_QK), lambda b, i: (b, i, col))
    bwd = lambda col: pl.BlockSpec((sb, c, HG_QK), lambda b, i: (b, nc - 1 - i, col))
    const = lambda shape: pl.BlockSpec(shape, lambda b, i: (0,) * len(shape))
    in_specs = [fwd(COL_HQ), fwd(COL_HV), fwd(COL_HFF), bwd(COL_HQ), bwd(COL_HV), bwd(COL_HFB),
                const(hg_lb.shape), const(a_f.shape), const(a_b.shape), const(m_f.shape), const(m_b.shape)]
    args = [proj, proj, proj, proj, proj, proj, hg_lb, a_f, a_b, m_f, m_b]
    state_spec = pl.BlockSpec((sb, 2, HG_HEADS, HG_DK, HG_DV), lambda b, i: (b, 0, 0, 0, 0))
    if s0 is not None:
        in_specs.append(pl.BlockSpec((sb, 1, 2, HG_HEADS, HG_DK, HG_DV), lambda b, i: (b, layer, 0, 0, 0, 0)))
        args.append(s0)
    out_specs = [pl.BlockSpec((sb, c, HG_WIDTH), lambda b, i: (b, i, 0)),
                 pl.BlockSpec((sb, c, HG_WIDTH), lambda b, i: (b, nc - 1 - i, 0))]
    out_shape = [jax.ShapeDtypeStruct((b_sz, n, HG_WIDTH), f32)] * 2
    if want_state:
        out_specs.append(state_spec)
        out_shape.append(jax.ShapeDtypeStruct((b_sz, 2, HG_HEADS, HG_DK, HG_DV), f32))
    return pl.pallas_call(
        functools.partial(_hgrn_kernel, layer, s0 is not None, want_state),
        grid=(b_sz // sb, nc),
        in_specs=in_specs,
        out_specs=out_specs,
        out_shape=out_shape,
        scratch_shapes=[pltpu.VMEM((sb, 2, HG_HEADS, HG_DV, HG_DK), f32)],
        compiler_params=_params(("parallel", "arbitrary")),
        name="hgrn",
    )(*args)


def _rope_rotate(x, cos, sin):
    lane = lax.broadcasted_iota(jnp.int32, x.shape, 1)
    first_half = (lane % (MLA_ROPE // 2)) < (MLA_ROPE // 4)
    quarter = MLA_ROPE // 4
    rot = jnp.where(first_half, -pltpu.roll(x, LANES - quarter, 1), pltpu.roll(x, quarter, 1))
    return x * cos + rot * sin


def _mla_kernel(n_ctx, use_rope, want_cache, tq, *refs):
    qa_ref, kvr_ref, wq_ref, wkt_ref, wv_ref, qn_ref, kvn_ref = refs[:7]
    pos = 7
    if n_ctx:
        cckv_ref, ckr_ref = refs[pos], refs[pos + 1]
        pos += 2
    if use_rope:
        cosq_ref, sinq_ref, cosk_ref, sink_ref = refs[pos:pos + 4]
        pos += 4
    y_ref = refs[pos]
    pos += 1
    if want_cache:
        ockv_ref, okr_ref = refs[pos], refs[pos + 1]
        pos += 2
    kcat_scr, v_scr, o_scr = refs[pos:pos + 3]
    n = kvr_ref.shape[1]
    rows = 512 if n % 512 == 0 else n

    def expand(r0, nrows, c_kv, krp):
        k_t = _dot_nt(wkt_ref[0], c_kv)
        eye = (lax.broadcasted_iota(jnp.int32, (LANES, LANES), 0)
               == lax.broadcasted_iota(jnp.int32, (LANES, LANES), 1)).astype(bf16)
        krp_t = _dot_nt(eye, krp).astype(bf16)
        vals = _dot(c_kv, wv_ref[0])
        ones = jnp.ones((nrows, MLA_V), bf16)
        for h in range(MLA_HEADS):
            kcat_scr[h, 0:MLA_NOPE, r0:r0 + nrows] = k_t[h * MLA_NOPE:(h + 1) * MLA_NOPE].astype(bf16)
            kcat_scr[h, MLA_NOPE:QHEAD_PAD, r0:r0 + nrows] = krp_t
            v_scr[h, r0:r0 + nrows, 0:MLA_V] = vals[:, h * MLA_V:(h + 1) * MLA_V].astype(bf16)
            v_scr[h, r0:r0 + nrows, MLA_V:2 * MLA_V] = ones

    @pl.when(pl.program_id(1) == 0)
    def _():
        if n_ctx:
            zeros = jnp.zeros((n_ctx, LANES - MLA_ROPE), f32)
            krp_ctx = jnp.concatenate([ckr_ref[0, 0], zeros], axis=1).astype(bf16)
            expand(0, n_ctx, cckv_ref[0, 0].astype(bf16), krp_ctx)
        for r0 in range(0, n, rows):
            blk = kvr_ref[0, r0:r0 + rows]
            c_kv = _rms(blk[:, :KV_LORA]) * kvn_ref[...]
            krp = blk[:, KV_LORA:]
            if want_cache:
                ockv_ref[0, r0:r0 + rows] = c_kv
                okr_ref[0, r0:r0 + rows] = krp[:, :MLA_ROPE]
            if use_rope:
                krp = _rope_rotate(krp, cosk_ref[r0:r0 + rows], sink_ref[r0:r0 + rows])
            expand(n_ctx + r0, rows, c_kv.astype(bf16), krp.astype(bf16))

    scale = float(MLA_NOPE + MLA_ROPE) ** -0.5
    qn = (_rms(qa_ref[0]) * qn_ref[...]).astype(bf16)
    sk = kcat_scr.shape[2]
    n_kc = 1 if sk <= MLA_KEY_CHUNK else sk // (2 * MXU_DIM)
    kc = sk // n_kc

    def head(h):
        q = _dot(qn, wq_ref[0, h]) * scale
        q_rope = q[:, MLA_NOPE:]
        if use_rope:
            q_rope = _rope_rotate(q_rope, cosq_ref[...], sinq_ref[...])
        q_cat = jnp.concatenate([q[:, :MLA_NOPE], q_rope], axis=1).astype(bf16)
        m_run = acc = None
        for c in range(n_kc):
            s = _dot(q_cat, kcat_scr[h, :, c * kc:(c + 1) * kc])
            m_c = jnp.max(s, axis=-1, keepdims=True)
            m_new = m_c if c == 0 else jnp.maximum(m_run, m_c)
            p = jnp.exp((s - m_new).astype(bf16))
            pv = _dot(p, v_scr[h, c * kc:(c + 1) * kc])
            acc = pv if c == 0 else jnp.exp(m_run - m_new) * acc + pv
            m_run = m_new
        return (acc[:, :MLA_V] / acc[:, MLA_V:]).astype(bf16)

    if n_kc == 1:
        for h in range(MLA_HEADS):
            y_ref[0, :, h * MLA_V:(h + 1) * MLA_V] = head(h)
    else:
        def body(h, carry):
            o_scr[h] = head(h)
            return carry

        lax.fori_loop(0, MLA_HEADS, body, 0, unroll=4)
        for h in range(MLA_HEADS):
            y_ref[0, :, h * MLA_V:(h + 1) * MLA_V] = o_scr[h]


def _mla(proj, wq, wkt, wv, q_norm, kv_norm, ctx, layer, rope, want_cache, tq):
    b_sz, n, _ = proj.shape
    n_ctx = 0 if ctx is None else ctx[0].shape[2]
    sk = n_ctx + n
    single = pl.Buffered(1)
    const = lambda shape: pl.BlockSpec(shape, lambda b, i: (0,) * len(shape), pipeline_mode=single)
    in_specs = [
        pl.BlockSpec((1, tq, Q_LORA), lambda b, i: (b, i, COL_QA)),
        pl.BlockSpec((1, n, KVR_WIDTH), lambda b, i: (b, 0, COL_KVR), pipeline_mode=single),
        pl.BlockSpec((1,) + wq.shape[1:], lambda b, i: (layer, 0, 0, 0), pipeline_mode=single),
        pl.BlockSpec((1,) + wkt.shape[1:], lambda b, i: (layer, 0, 0), pipeline_mode=single),
        pl.BlockSpec((1,) + wv.shape[1:], lambda b, i: (layer, 0, 0), pipeline_mode=single),
        const(q_norm.shape), const(kv_norm.shape),
    ]
    args = [proj, proj, wq, wkt, wv, q_norm, kv_norm]
    if ctx is not None:
        in_specs += [
            pl.BlockSpec((1, 1, n_ctx, KV_LORA), lambda b, i: (b, layer, 0, 0), pipeline_mode=single),
            pl.BlockSpec((1, 1, n_ctx, MLA_ROPE), lambda b, i: (b, layer, 0, 0), pipeline_mode=single)]
        args += [ctx[0], ctx[1]]
    if rope is not None:
        cos, sin = rope
        in_specs += [pl.BlockSpec((tq, LANES), lambda b, i: (i, 0)),
                     pl.BlockSpec((tq, LANES), lambda b, i: (i, 0)),
                     const(cos.shape), const(sin.shape)]
        args += [cos, sin, cos, sin]
    out_specs = [pl.BlockSpec((1, tq, MLA_WIDTH), lambda b, i: (b, i, 0))]
    out_shape = [jax.ShapeDtypeStruct((b_sz, n, MLA_WIDTH), bf16)]
    if want_cache:
        out_specs += [pl.BlockSpec((1, n, KV_LORA), lambda b, i: (b, 0, 0)),
                      pl.BlockSpec((1, n, MLA_ROPE), lambda b, i: (b, 0, 0))]
        out_shape += [jax.ShapeDtypeStruct((b_sz, n, KV_LORA), f32),
                      jax.ShapeDtypeStruct((b_sz, n, MLA_ROPE), f32)]
    return pl.pallas_call(
        functools.partial(_mla_kernel, n_ctx, rope is not None, want_cache, tq),
        grid=(b_sz, n // tq),
        in_specs=in_specs,
        out_specs=out_specs,
        out_shape=out_shape,
        scratch_shapes=[pltpu.VMEM((MLA_HEADS, QHEAD_PAD, sk), bf16), pltpu.VMEM((MLA_HEADS, sk, 2 * MLA_V), bf16),
                        pltpu.VMEM((MLA_HEADS, tq, MLA_V), bf16)],
        compiler_params=_params(("parallel", "arbitrary")),
        name="mla",
    )(*args)


def _rope_tables(n):
    t = np.arange(n)
    r = (t // GRID_W).astype(np.float32)
    col = (t % GRID_W).astype(np.float32)
    nf = MLA_ROPE // 4
    inv = np.float32(ROPE_BASE) ** (-np.arange(nf, dtype=np.float32) / np.float32(nf))
    ar = r[:, None] * inv
    ac = col[:, None] * inv
    ang = np.concatenate([ar, ar, ac, ac], axis=-1).astype(np.float32)
    pad = LANES - MLA_ROPE
    cos = np.concatenate([np.cos(ang), np.ones((n, pad), np.float32)], axis=-1)
    sin = np.concatenate([np.sin(ang), np.zeros((n, pad), np.float32)], axis=-1)
    return jnp.asarray(cos, dtype=f32), jnp.asarray(sin, dtype=f32)


def _outproj_kernel(x_ref, yfn_ref, of_ref, ob_ref, gate_ref, ymla_ref, w_ref, hgg_ref, gpost_ref,
                    mod_ref, o_ref, inv_scr):
    o_hg = of_ref[0] + ob_ref[0]
    gate = gate_ref[0]
    gate = gate * jax.nn.sigmoid(gate)
    parts = [yfn_ref[0]]
    for h in range(HG_HEADS):
        sl = slice(h * HG_DV, (h + 1) * HG_DV)
        parts.append((_rms(o_hg[:, sl]) * hgg_ref[:, sl] * gate[:, sl]).astype(bf16))
    parts.append(ymla_ref[0])
    o_ref[0] = _dot(jnp.concatenate(parts, axis=1), w_ref[0])
    y = o_ref[0]
    inv_scr[...] = lax.rsqrt(jnp.mean(y * y, axis=-1, keepdims=True) + EPS)
    gain = mod_ref[0, 2:3, :] * gpost_ref[...]
    o_ref[0] = x_ref[0] + o_ref[0] * inv_scr[...] * gain


def _outproj(x, y_fn, o_f, o_b, proj, y_mla, w_out, hg_gain, g_post, mod, layer, tm):
    b_sz, n, _ = x.shape
    tok = lambda width, col=0: pl.BlockSpec((1, tm, width), lambda b, i: (b, i, col))
    return pl.pallas_call(
        _outproj_kernel,
        grid=(b_sz, n // tm),
        in_specs=[
            tok(D_MODEL), tok(FN_WIDTH), tok(HG_WIDTH), tok(HG_WIDTH), tok(HG_WIDTH, COL_HGATE),
            tok(MLA_WIDTH),
            pl.BlockSpec((1, D_MODEL, D_MODEL), lambda b, i: (layer, 0, 0)),
            pl.BlockSpec((1, HG_WIDTH), lambda b, i: (0, 0)),
            pl.BlockSpec((1, D_MODEL), lambda b, i: (0, 0)),
            pl.BlockSpec((1, 6, D_MODEL), lambda b, i: (b, 0, 0)),
        ],
        out_specs=tok(D_MODEL),
        out_shape=jax.ShapeDtypeStruct((b_sz, n, D_MODEL), f32),
        scratch_shapes=[pltpu.VMEM((tm, 1), f32)],
        compiler_params=_params(("parallel", "parallel")),
        name="outproj",
    )(x, y_fn, o_f, o_b, proj, y_mla, w_out, hg_gain, g_post, mod)


def _ffn_kernel(x_ref, mod_ref, gpre_ref, gpost_ref, w1_ref, w2_ref, o_ref, h_scr, inv_scr):
    k = pl.program_id(2)

    def hidden_slice():
        a = jnp.maximum(_dot(h_scr[...], w1_ref[0]), 0.0)
        return _dot((a * a).astype(bf16), w2_ref[0])

    @pl.when(k == 0)
    def _():
        _modulated_norm(x_ref, gpre_ref, mod_ref[0, 4:5, :], mod_ref[0, 3:4, :], h_scr, inv_scr)
        o_ref[0] = hidden_slice()

    last = pl.num_programs(2) - 1

    @pl.when((k > 0) & (k < last))
    def _():
        o_ref[0] += hidden_slice()

    @pl.when(k == last)
    def _():
        o_ref[0] += hidden_slice()
        f = o_ref[0]
        inv_scr[...] = lax.rsqrt(jnp.mean(f * f, axis=-1, keepdims=True) + EPS)
        gain = mod_ref[0, 5:6, :] * gpost_ref[...]
        o_ref[0] = x_ref[0] + o_ref[0] * inv_scr[...] * gain


def _ffn(x, mod, g_pre, g_post, w1, w2, layer, tm, tf):
    b_sz, n, _ = x.shape
    return pl.pallas_call(
        _ffn_kernel,
        grid=(b_sz, n // tm, D_FF // tf),
        in_specs=[
            pl.BlockSpec((1, tm, D_MODEL), lambda b, i, k: (b, i, 0)),
            pl.BlockSpec((1, 6, D_MODEL), lambda b, i, k: (b, 0, 0)),
            pl.BlockSpec((1, D_MODEL), lambda b, i, k: (0, 0)),
            pl.BlockSpec((1, D_MODEL), lambda b, i, k: (0, 0)),
            pl.BlockSpec((1, D_MODEL, tf), lambda b, i, k: (layer, 0, k)),
            pl.BlockSpec((1, tf, D_MODEL), lambda b, i, k: (layer, k, 0)),
        ],
        out_specs=pl.BlockSpec((1, tm, D_MODEL), lambda b, i, k: (b, i, 0)),
        out_shape=jax.ShapeDtypeStruct((b_sz, n, D_MODEL), f32),
        scratch_shapes=[pltpu.VMEM((tm, D_MODEL), bf16), pltpu.VMEM((tm, 1), f32)],
        compiler_params=_params(("parallel", "parallel", "arbitrary")),
        name="ffn",
    )(x, mod, g_pre, g_post, w1, w2)


def _layer(x, mod, wl, hg_tables, dft, layer, n_seq, ctx, s0, rope, is_context, tq):
    b_sz, n, _ = x.shape
    n_seqs = b_sz * n // n_seq
    proj = _inproj(x, mod, wl["g_pre_mix"], wl["w_in"], layer, INPROJ_TM).reshape(n_seqs, n_seq, IN_COLS_PAD)
    y_fn = _fourier(proj, dft[0], dft[1], dft[2], min(n_seq, 512))
    hg = _hgrn(proj, wl["hg_lb"], hg_tables, layer, s0, is_context)
    mla = _mla(proj, wl["wq"], wl["wkt"], wl["wv"], wl["q_norm"], wl["kv_norm"], ctx, layer, rope, is_context,
               tq)
    flat = lambda a: a.reshape(b_sz, n, a.shape[-1])
    x = _outproj(x, flat(y_fn), flat(hg[0]), flat(hg[1]), flat(proj), flat(mla[0]), wl["w_out"],
                 wl["hg_gain"], wl["g_post_mix"], mod, layer, OUTPROJ_TM)
    x = _ffn(x, mod, wl["g_pre_ff"], wl["g_post_ff"], wl["w_ff1"], wl["w_ff2"], layer, FFN_TM, FFN_TF)
    extras = (mla[1], mla[2], hg[2]) if is_context else None
    return x, extras


def kernel(x_prompt, x_sample, c, cache_ckv, cache_krope, state_hgrn, c_ctx, w_ada, b_ada, g_pre_mix,
           g_post_mix, g_pre_ff, g_post_ff, w_in, hg_lb, hg_gain, mla_q_norm, mla_kv_norm, w_q_b, w_kv_b,
           w_out, w_ff1, w_ff2):
    batch, seq, _ = x_prompt.shape
    dec_batch, dec_seq, _ = x_sample.shape

    rows = 16
    cond = jnp.concatenate([c_ctx[None, :], c, jnp.zeros((rows - 1 - dec_batch, D_MODEL), f32)], axis=0)
    mod = _ada(cond, w_ada, b_ada).reshape(DEPTH, rows, 6, D_MODEL)

    hg_tables = _hgrn_tables()
    ccs = _channel_tables()
    dft_p = (ccs,) + _dft_tables(seq)
    dft_s = (ccs,) + _dft_tables(dec_seq)
    rope = _rope_tables(dec_seq)

    yp = x_prompt.reshape(1, batch * seq, D_MODEL)
    ys = x_sample
    ckv_list, kr_list, st_list = [], [], []
    wq = w_q_b.reshape(DEPTH, Q_LORA, MLA_HEADS, MLA_NOPE + MLA_ROPE).astype(bf16)
    wq = jnp.pad(wq, ((0, 0), (0, 0), (0, 0), (0, QHEAD_PAD - MLA_NOPE - MLA_ROPE)))
    wkv = w_kv_b.astype(bf16).reshape(DEPTH, KV_LORA, MLA_HEADS, MLA_NOPE + MLA_V)
    stacked = {
        "w_in": jnp.pad(w_in.astype(bf16), ((0, 0), (0, 0), (0, IN_COLS_PAD - IN_COLS))),
        "wq": wq.transpose(0, 2, 1, 3),
        "wkt": wkv[:, :, :, :MLA_NOPE].reshape(DEPTH, KV_LORA, MLA_HEADS * MLA_NOPE).transpose(0, 2, 1),
        "wv": wkv[:, :, :, MLA_NOPE:].reshape(DEPTH, KV_LORA, MLA_HEADS * MLA_V),
        "w_out": w_out.astype(bf16),
        "w_ff1": w_ff1.astype(bf16),
        "w_ff2": w_ff2.astype(bf16),
    }
    for l in range(DEPTH):
        wl = {
            **stacked,
            "g_pre_mix": g_pre_mix[l][None, :], "g_post_mix": g_post_mix[l][None, :],
            "g_pre_ff": g_pre_ff[l][None, :], "g_post_ff": g_post_ff[l][None, :],
            "hg_lb": hg_lb, "hg_gain": hg_gain[l][None, :],
            "q_norm": mla_q_norm[l][None, :], "kv_norm": mla_kv_norm[l][None, :],
        }
        yp, (ckv_l, kr_l, st_l) = _layer(yp, mod[l, 0:1], wl, hg_tables, dft_p, l, seq, None, None, None,
                                         True, seq)
        ckv_list.append(ckv_l)
        kr_list.append(kr_l)
        st_list.append(st_l)
        ys, _ = _layer(ys, mod[l, 1:1 + dec_batch], wl, hg_tables, dft_s, l, dec_seq,
                       (cache_ckv, cache_krope), state_hgrn, rope, False, 1024)

    return (yp.reshape(batch, seq, D_MODEL), ys, jnp.stack(ckv_list, axis=1), jnp.stack(kr_list, axis=1),
            jnp.stack(st_list, axis=1))
```

```python
import functools

import numpy as np
import jax
import jax.numpy as jnp
from jax import lax
from jax.experimental import pallas as pl
from jax.experimental.pallas import tpu as pltpu

f32 = jnp.float32
bf16 = jnp.bfloat16

D_MODEL = 2048
DEPTH = 2
GRID_W = 64
FN_HEADS = 4
FN_DH = 128
FN_WIDTH = FN_HEADS * FN_DH
HG_HEADS = 4
HG_DK = 128
HG_DV = 128
HG_QK = HG_HEADS * HG_DK
HG_WIDTH = HG_HEADS * HG_DV
MLA_HEADS = 8
MLA_NOPE = 128
MLA_ROPE = 64
MLA_V = 128
Q_LORA = 768
KV_LORA = 512
MLA_WIDTH = MLA_HEADS * MLA_V
D_FF = 4 * D_MODEL
ROPE_BASE = 10000.0
EPS = 1e-6
F_FLOOR = 1e-30

LANES = 128
SUBLANES = 8
MXU_DIM = 256
IN_COLS = 4416
IN_COLS_PAD = 4608
IN_TILE = 1536
COL_HQ, COL_HV, COL_HFF, COL_HFB, COL_HGATE = 1, 2, 3, 4, 5
COL_QA = 4
KVR_WIDTH = 640
COL_KVR = 6
QHEAD_PAD = 256
HG_CHUNK = 128
HG_LEVELS = (64, 32, 16, 8, 4, 2, 1)
HG_SPLIT = 2
HG_SEQS = 4
INPROJ_TM = 1024
OUTPROJ_TM = 512
MLA_KEY_CHUNK = 1280
MLA_SEQS = 4
FFN_TM = 1024
FFN_TF = 512
VMEM_LIMIT = 56 * 1024 * 1024


def _params(semantics):
    return pltpu.CompilerParams(dimension_semantics=semantics, vmem_limit_bytes=VMEM_LIMIT)


def _dot(a, b):
    return jnp.dot(a, b, preferred_element_type=f32)


def _dot_nt(a, b):
    return lax.dot_general(a, b, (((1,), (1,)), ((), ())), preferred_element_type=f32)


def _dot_tn(a, b):
    return lax.dot_general(a, b, (((0,), (0,)), ((), ())), preferred_element_type=f32)


def _rms(x):
    return x * lax.rsqrt(jnp.mean(x * x, axis=-1, keepdims=True) + EPS)


def _modulated_norm(x_ref, g_ref, scale_row, shift_row, h_scr, inv_scr):
    x = x_ref[0]
    inv_scr[...] = lax.rsqrt(jnp.mean(x * x, axis=-1, keepdims=True) + EPS)
    gain = g_ref[...] * (1.0 + scale_row)
    h_scr[...] = (x_ref[0] * inv_scr[...] * gain + shift_row).astype(bf16)


def _ada_kernel(cond_ref, w_ref, b_ref, o_ref):
    cnd = cond_ref[...]
    act = cnd * jax.nn.sigmoid(cnd)
    o_ref[0] = _dot(act.astype(bf16), w_ref[0].astype(bf16)) + b_ref[0]


def _ada(cond, w_ada, b_ada):
    rows = cond.shape[0]
    tn = 1024
    return pl.pallas_call(
        _ada_kernel,
        grid=(DEPTH, 6 * D_MODEL // tn),
        in_specs=[
            pl.BlockSpec((rows, D_MODEL), lambda l, j: (0, 0)),
            pl.BlockSpec((1, D_MODEL, tn), lambda l, j: (l, 0, j)),
            pl.BlockSpec((1, 1, tn), lambda l, j: (l, 0, j)),
        ],
        out_specs=pl.BlockSpec((1, rows, tn), lambda l, j: (l, 0, j)),
        out_shape=jax.ShapeDtypeStruct((DEPTH, rows, 6 * D_MODEL), f32),
        compiler_params=_params(("parallel", "parallel")),
        name="ada_mod",
    )(cond, w_ada, b_ada.reshape(DEPTH, 1, 6 * D_MODEL))


def _inproj_kernel(x_ref, mod_ref, g_ref, w_ref, o_ref, h_scr, inv_scr):
    @pl.when(pl.program_id(2) == 0)
    def _():
        _modulated_norm(x_ref, g_ref, mod_ref[0, 1:2, :], mod_ref[0, 0:1, :], h_scr, inv_scr)
        o_ref[0] = _dot(h_scr[...], w_ref[0, 0])

    @pl.when(pl.program_id(2) > 0)
    def _():
        o_ref[0] = _dot(h_scr[...], w_ref[0, 0])


def _inproj(x, mod, g, w_in, layer, tm):
    b_sz, n, _ = x.shape
    return pl.pallas_call(
        _inproj_kernel,
        grid=(b_sz, n // tm, IN_COLS_PAD // IN_TILE),
        in_specs=[
            pl.BlockSpec((1, tm, D_MODEL), lambda b, i, j: (b, i, 0)),
            pl.BlockSpec((1, 6, D_MODEL), lambda b, i, j: (b, 0, 0)),
            pl.BlockSpec((1, D_MODEL), lambda b, i, j: (0, 0)),
            pl.BlockSpec((1, 1, D_MODEL, IN_TILE), lambda b, i, j: (layer, j, 0, 0)),
        ],
        out_specs=pl.BlockSpec((1, tm, IN_TILE), lambda b, i, j: (b, i, j)),
        out_shape=jax.ShapeDtypeStruct((b_sz, n, IN_COLS_PAD), f32),
        scratch_shapes=[pltpu.VMEM((tm, D_MODEL), bf16), pltpu.VMEM((tm, 1), f32)],
        compiler_params=_params(("parallel", "parallel", "arbitrary")),
        name="inproj",
    )(x, mod, g, w_in)


def _fourier_kernel(u_ref, ccs_ref, cn_ref, sn_ref, o_ref, ucs_scr):
    @pl.when(pl.program_id(1) == 0)
    def _():
        for h in range(FN_HEADS):
            sl = slice(h * FN_DH, (h + 1) * FN_DH)
            r = _dot(u_ref[0, :, sl].astype(bf16), ccs_ref[...]).astype(bf16)
            ucs_scr[:, sl] = r[:, :FN_DH]
            ucs_scr[:, FN_WIDTH + h * FN_DH:FN_WIDTH + (h + 1) * FN_DH] = r[:, FN_DH:]

    y = _dot(cn_ref[...], ucs_scr[:, :FN_WIDTH]) + _dot(sn_ref[...], ucs_scr[:, FN_WIDTH:])
    o_ref[0] = y.astype(bf16)


def _fourier(proj, ccs, cn, sn_neg, tr):
    b_sz, n, _ = proj.shape
    return pl.pallas_call(
        _fourier_kernel,
        grid=(b_sz, n // tr),
        in_specs=[
            pl.BlockSpec((1, n, FN_WIDTH), lambda b, i: (b, 0, 0)),
            pl.BlockSpec((FN_DH, 2 * FN_DH), lambda b, i: (0, 0)),
            pl.BlockSpec((tr, n), lambda b, i: (i, 0)),
            pl.BlockSpec((tr, n), lambda b, i: (i, 0)),
        ],
        out_specs=pl.BlockSpec((1, tr, FN_WIDTH), lambda b, i: (b, i, 0)),
        out_shape=jax.ShapeDtypeStruct((b_sz, n, FN_WIDTH), bf16),
        scratch_shapes=[pltpu.VMEM((n, 2 * FN_WIDTH), bf16)],
        compiler_params=_params(("parallel", "arbitrary")),
        name="fourier",
    )(proj, ccs, cn, sn_neg)


def _dft_tables(n):
    j = np.arange(n, dtype=np.int64)
    ang = ((j[:, None] * j[None, :]) % n).astype(np.float64) * (2.0 * np.pi / n)
    scale_n = 1.0 / np.sqrt(n)
    cn = jnp.asarray(np.cos(ang) * scale_n, dtype=bf16)
    sn_neg = jnp.asarray(-np.sin(ang) * scale_n, dtype=bf16)
    return cn, sn_neg


def _channel_tables():
    k = np.arange(FN_DH)
    ang = 2.0 * np.pi * ((k[:, None] * k[None, :]) % FN_DH) / FN_DH
    table = np.concatenate([np.cos(ang), np.sin(ang)], axis=1) / np.sqrt(FN_DH)
    return jnp.asarray(table, dtype=bf16)


def _hgrn_tables():
    c = HG_CHUNK
    idx = np.arange(c)
    blocks = [(idx[None, :] <= idx[:, None])]
    masks = []
    for h in HG_LEVELS:
        mid = (idx // (2 * h)) * (2 * h) + h
        upper = idx >= mid
        if h < SUBLANES:
            row_up = (idx[None, :] >= mid[:, None]) & (idx[None, :] <= idx[:, None])
            row_lo = (idx[None, :] > idx[:, None]) & (idx[None, :] < mid[:, None])
            blocks.append(np.where(upper[:, None], row_up, row_lo))
        same = (idx[:, None] // (2 * h)) == (idx[None, :] // (2 * h))
        masks.append(same & upper[:, None] & ~upper[None, :])
    masks.append(np.eye(c, dtype=bool))
    a_f = np.concatenate([b.astype(np.float32) for b in blocks], axis=0)
    a_b = np.concatenate([b[::-1, ::-1].astype(np.float32) for b in blocks], axis=0)
    m_f = np.stack([m.astype(np.float32) for m in masks])
    m_b = np.stack([m[::-1, ::-1].astype(np.float32) for m in masks])
    rep = lambda a: jnp.asarray(np.concatenate([a] * HG_SPLIT, axis=1), dtype=bf16)
    return rep(a_f), rep(a_b), jnp.asarray(m_f), jnp.asarray(m_b)


def _block_row(x, h, row):
    c, w = x.shape
    xb = x.reshape(c // (2 * h), 2 * h, w)
    return jnp.broadcast_to(xb[:, row:row + 1, :], xb.shape).reshape(c, w)


def _hgrn_direction(q, v, pre, lb, a_ref, m_ref, states, backward):
    c = HG_CHUNK
    nl = len(HG_LEVELS)
    f = lb + (1.0 - lb) * jax.nn.sigmoid(pre)
    lf = jnp.log(jnp.maximum(f, F_FLOOR))
    kk = 1.0 - f
    pieces = []
    rem = lf
    for _ in range(HG_SPLIT):
        piece = rem.astype(bf16)
        pieces.append(piece)
        rem = rem - piece.astype(f32)
    sums = _dot(a_ref[...], jnp.concatenate(pieces, axis=0))
    cum = sums[0:c]
    total = cum[0:1] if backward else cum[c - 1:c]
    q16 = q.astype(bf16)
    k16 = kk.astype(bf16)
    v16 = v.astype(bf16)
    decay16 = lambda log_decay: jnp.exp(log_decay.astype(bf16))
    q_inter = q16 * decay16(cum)
    k_state = k16 * decay16(jnp.minimum(total - cum, 0.0))
    e_total = jnp.exp(total)
    q_levels, k_levels = [q16], [k16]
    fine = 0
    for h in HG_LEVELS:
        if h >= SUBLANES:
            ref = _block_row(cum, h, h if backward else h - 1)
            el = decay16(-jnp.abs(cum - ref))
        else:
            fine += 1
            el = decay16(sums[fine * c:(fine + 1) * c])
        q_levels.append(q16 * el)
        k_levels.append(k16 * el)
    outs, new_states = [], []
    for h in range(HG_HEADS):
        sl = slice(h * HG_DK, (h + 1) * HG_DK)
        vh = v16[:, sl]
        sc = m_ref[nl] * _dot_nt(q_levels[0][:, sl], k_levels[0][:, sl])
        for li in range(nl):
            sc = sc + m_ref[li] * _dot_nt(q_levels[li + 1][:, sl], k_levels[li + 1][:, sl])
        st = states[h]
        outs.append(_dot(sc.astype(bf16), vh) + _dot_nt(q_inter[:, sl], st.astype(bf16)))
        new_states.append(e_total[:, sl] * st + _dot_tn(vh, k_state[:, sl]))
    return outs, new_states


def _hgrn_kernel(layer, has_s0, has_sout, *refs):
    (qf_ref, vf_ref, ff_ref, qb_ref, vb_ref, fb_ref, lb_ref, af_ref, ab_ref, mf_ref, mb_ref) = refs[:11]
    pos = 11
    s0_ref = None
    if has_s0:
        s0_ref = refs[pos]
        pos += 1
    of_ref, ob_ref = refs[pos], refs[pos + 1]
    pos += 2
    sout_ref = None
    if has_sout:
        sout_ref = refs[pos]
        pos += 1
    st_scr = refs[pos]
    ci = pl.program_id(1)

    n_seq = qf_ref.shape[0]

    @pl.when(ci == 0)
    def _():
        for s in range(n_seq):
            for d in range(2):
                for h in range(HG_HEADS):
                    if has_s0:
                        st_scr[s, d, h] = s0_ref[s, 0, d, h].T
                    else:
                        st_scr[s, d, h] = jnp.zeros((HG_DV, HG_DK), f32)

    raw = lb_ref[...]
    mx = raw[0]
    for i in range(1, DEPTH):
        mx = jnp.maximum(mx, raw[i])
    ex = [jnp.exp(raw[i] - mx) for i in range(DEPTH)]
    den = ex[0]
    for i in range(1, DEPTH):
        den = den + ex[i]
    lb = jnp.zeros_like(den)
    for i in range(1, layer + 1):
        lb = lb + ex[i] / den

    loaded = [[[st_scr[s, d, h] for h in range(HG_HEADS)] for d in range(2)] for s in range(n_seq)]
    results = []
    for s in range(n_seq):
        o_f, st_f = _hgrn_direction(qf_ref[s], vf_ref[s], ff_ref[s], lb[0:1], af_ref, mf_ref, loaded[s][0], False)
        o_b, st_b = _hgrn_direction(qb_ref[s], vb_ref[s], fb_ref[s], lb[1:2], ab_ref, mb_ref, loaded[s][1], True)
        of_ref[s] = jnp.concatenate(o_f, axis=1)
        ob_ref[s] = jnp.concatenate(o_b, axis=1)
        results.append((st_f, st_b))
    for s in range(n_seq):
        for h in range(HG_HEADS):
            st_scr[s, 0, h] = results[s][0][h]
            st_scr[s, 1, h] = results[s][1][h]

    if has_sout:
        @pl.when(ci == pl.num_programs(1) - 1)
        def _():
            for s in range(n_seq):
                for d in range(2):
                    for h in range(HG_HEADS):
                        sout_ref[s, d, h] = st_scr[s, d, h].T


def _hgrn(proj, hg_lb, tables, layer, s0, want_state):
    b_sz, n, _ = proj.shape
    nc = n // HG_CHUNK
    a_f, a_b, m_f, m_b = tables
    c = HG_CHUNK
    sb = HG_SEQS
    fwd = lambda col: pl.BlockSpec((sb, c, HG_QK), lambda b, i: (b, i, col))
    bwd = lambda col: pl.BlockSpec((sb, c, HG_QK), lambda b, i: (b, nc - 1 - i, col))
    const = lambda shape: pl.BlockSpec(shape, lambda b, i: (0,) * len(shape))
    in_specs = [fwd(COL_HQ), fwd(COL_HV), fwd(COL_HFF), bwd(COL_HQ), bwd(COL_HV), bwd(COL_HFB),
                const(hg_lb.shape), const(a_f.shape), const(a_b.shape), const(m_f.shape), const(m_b.shape)]
    args = [proj, proj, proj, proj, proj, proj, hg_lb, a_f, a_b, m_f, m_b]
    state_spec = pl.BlockSpec((sb, 2, HG_HEADS, HG_DK, HG_DV), lambda b, i: (b, 0, 0, 0, 0))
    if s0 is not None:
        in_specs.append(pl.BlockSpec((sb, 1, 2, HG_HEADS, HG_DK, HG_DV), lambda b, i: (b, layer, 0, 0, 0, 0)))
        args.append(s0)
    out_specs = [pl.BlockSpec((sb, c, HG_WIDTH), lambda b, i: (b, i, 0)),
                 pl.BlockSpec((sb, c, HG_WIDTH), lambda b, i: (b, nc - 1 - i, 0))]
    out_shape = [jax.ShapeDtypeStruct((b_sz, n, HG_WIDTH), f32)] * 2
    if want_state:
        out_specs.append(state_spec)
        out_shape.append(jax.ShapeDtypeStruct((b_sz, 2, HG_HEADS, HG_DK, HG_DV), f32))
    return pl.pallas_call(
        functools.partial(_hgrn_kernel, layer, s0 is not None, want_state),
        grid=(b_sz // sb, nc),
        in_specs=in_specs,
        out_specs=out_specs,
        out_shape=out_shape,
        scratch_shapes=[pltpu.VMEM((sb, 2, HG_HEADS, HG_DV, HG_DK), f32)],
        compiler_params=_params(("parallel", "arbitrary")),
        name="hgrn",
    )(*args)


def _rope_rotate(x, cos, sin):
    lane = lax.broadcasted_iota(jnp.int32, x.shape, 1)
    first_half = (lane % (MLA_ROPE // 2)) < (MLA_ROPE // 4)
    quarter = MLA_ROPE // 4
    rot = jnp.where(first_half, -pltpu.roll(x, LANES - quarter, 1), pltpu.roll(x, quarter, 1))
    return x * cos + rot * sin


def _mla_kernel(n_ctx, use_rope, want_cache, tq, *refs):
    qa_ref, kvr_ref, wq_ref, wkt_ref, wv_ref, qn_ref, kvn_ref = refs[:7]
    pos = 7
    if n_ctx:
        cckv_ref, ckr_ref = refs[pos], refs[pos + 1]
        pos += 2
    if use_rope:
        cosq_ref, sinq_ref, cosk_ref, sink_ref = refs[pos:pos + 4]
        pos += 4
    y_ref = refs[pos]
    pos += 1
    if want_cache:
        ockv_ref, okr_ref = refs[pos], refs[pos + 1]
        pos += 2
    kcat_scr, v_scr, o_scr = refs[pos:pos + 3]
    n_seq, n = kvr_ref.shape[0], kvr_ref.shape[1]
    sk = n_ctx + n
    total = n_seq * n
    rows = 512 if total % 512 == 0 else total

    def pieces(r0, nrows):
        out, r = [], r0
        while r < r0 + nrows:
            s, off = divmod(r, n)
            cnt = min(n - off, r0 + nrows - r)
            out.append((s, off, cnt))
            r += cnt
        return out

    def expand(r0, nrows, c_kv, krp):
        k_t = _dot_nt(wkt_ref[0], c_kv)
        eye = (lax.broadcasted_iota(jnp.int32, (LANES, LANES), 0)
               == lax.broadcasted_iota(jnp.int32, (LANES, LANES), 1)).astype(bf16)
        krp_t = _dot_nt(eye, krp).astype(bf16)
        vals = _dot(c_kv, wv_ref[0])
        ones = jnp.ones((nrows, MLA_V), bf16)
        for h in range(MLA_HEADS):
            kcat_scr[h, 0:MLA_NOPE, r0:r0 + nrows] = k_t[h * MLA_NOPE:(h + 1) * MLA_NOPE].astype(bf16)
            kcat_scr[h, MLA_NOPE:QHEAD_PAD, r0:r0 + nrows] = krp_t
            v_scr[h, r0:r0 + nrows, 0:MLA_V] = vals[:, h * MLA_V:(h + 1) * MLA_V].astype(bf16)
            v_scr[h, r0:r0 + nrows, MLA_V:2 * MLA_V] = ones

    @pl.when(pl.program_id(1) == 0)
    def _():
        if n_ctx:
            zeros = jnp.zeros((n_ctx, LANES - MLA_ROPE), f32)
            krp_ctx = jnp.concatenate([ckr_ref[0, 0], zeros], axis=1).astype(bf16)
            expand(0, n_ctx, cckv_ref[0, 0].astype(bf16), krp_ctx)
        for r0 in range(0, total, rows):
            parts = pieces(r0, rows)
            blk = jnp.concatenate([kvr_ref[s, off:off + cnt] for s, off, cnt in parts], axis=0)
            c_kv = _rms(blk[:, :KV_LORA]) * kvn_ref[...]
            krp = blk[:, KV_LORA:]
            if want_cache:
                at = 0
                for s, off, cnt in parts:
                    ockv_ref[s, off:off + cnt] = c_kv[at:at + cnt]
                    okr_ref[s, off:off + cnt] = krp[at:at + cnt, :MLA_ROPE]
                    at += cnt
            if use_rope:
                krp = _rope_rotate(krp, cosk_ref[r0:r0 + rows], sink_ref[r0:r0 + rows])
            expand(n_ctx + r0, rows, c_kv.astype(bf16), krp.astype(bf16))

    scale = float(MLA_NOPE + MLA_ROPE) ** -0.5
    qa = jnp.concatenate([qa_ref[s] for s in range(n_seq)], axis=0)
    qn = (_rms(qa) * qn_ref[...]).astype(bf16)
    n_kc = 1 if sk <= MLA_KEY_CHUNK else sk // (2 * MXU_DIM)
    kc = sk // n_kc

    def head(h):
        q = _dot(qn, wq_ref[0, h]) * scale
        q_rope = q[:, MLA_NOPE:]
        if use_rope:
            q_rope = _rope_rotate(q_rope, cosq_ref[...], sinq_ref[...])
        q_cat = jnp.concatenate([q[:, :MLA_NOPE], q_rope], axis=1).astype(bf16)
        outs = []
        for s in range(n_seq):
            q_s = q_cat[s * tq:(s + 1) * tq]
            m_run = acc = None
            for c in range(n_kc):
                k0 = s * sk + c * kc
                sc = _dot(q_s, kcat_scr[h, :, k0:k0 + kc])
                m_c = jnp.max(sc, axis=-1, keepdims=True)
                m_new = m_c if c == 0 else jnp.maximum(m_run, m_c)
                p = jnp.exp((sc - m_new).astype(bf16))
                pv = _dot(p, v_scr[h, k0:k0 + kc])
                acc = pv if c == 0 else jnp.exp(m_run - m_new) * acc + pv
                m_run = m_new
            outs.append((acc[:, :MLA_V] / acc[:, MLA_V:]).astype(bf16))
        return outs

    if n_kc == 1:
        for h in range(MLA_HEADS):
            for s, o in enumerate(head(h)):
                y_ref[s, :, h * MLA_V:(h + 1) * MLA_V] = o
    else:
        def body(h, carry):
            o_scr[h] = head(h)[0]
            return carry

        lax.fori_loop(0, MLA_HEADS, body, 0, unroll=4)
        for h in range(MLA_HEADS):
            y_ref[0, :, h * MLA_V:(h + 1) * MLA_V] = o_scr[h]


def _mla(proj, wq, wkt, wv, q_norm, kv_norm, ctx, layer, rope, want_cache, tq):
    b_sz, n, _ = proj.shape
    n_ctx = 0 if ctx is None else ctx[0].shape[2]
    sk = n_ctx + n
    sb = MLA_SEQS if (ctx is None and rope is None and n == tq) else 1
    single = pl.Buffered(1)
    const = lambda shape: pl.BlockSpec(shape, lambda b, i: (0,) * len(shape), pipeline_mode=single)
    in_specs = [
        pl.BlockSpec((sb, tq, Q_LORA), lambda b, i: (b, i, COL_QA)),
        pl.BlockSpec((sb, n, KVR_WIDTH), lambda b, i: (b, 0, COL_KVR), pipeline_mode=single),
        pl.BlockSpec((1,) + wq.shape[1:], lambda b, i: (layer, 0, 0, 0), pipeline_mode=single),
        pl.BlockSpec((1,) + wkt.shape[1:], lambda b, i: (layer, 0, 0), pipeline_mode=single),
        pl.BlockSpec((1,) + wv.shape[1:], lambda b, i: (layer, 0, 0), pipeline_mode=single),
        const(q_norm.shape), const(kv_norm.shape),
    ]
    args = [proj, proj, wq, wkt, wv, q_norm, kv_norm]
    if ctx is not None:
        in_specs += [
            pl.BlockSpec((1, 1, n_ctx, KV_LORA), lambda b, i: (b, layer, 0, 0), pipeline_mode=single),
            pl.BlockSpec((1, 1, n_ctx, MLA_ROPE), lambda b, i: (b, layer, 0, 0), pipeline_mode=single)]
        args += [ctx[0], ctx[1]]
    if rope is not None:
        cos, sin = rope
        in_specs += [pl.BlockSpec((tq, LANES), lambda b, i: (i, 0)),
                     pl.BlockSpec((tq, LANES), lambda b, i: (i, 0)),
                     const(cos.shape), const(sin.shape)]
        args += [cos, sin, cos, sin]
    out_specs = [pl.BlockSpec((sb, tq, MLA_WIDTH), lambda b, i: (b, i, 0))]
    out_shape = [jax.ShapeDtypeStruct((b_sz, n, MLA_WIDTH), bf16)]
    if want_cache:
        out_specs += [pl.BlockSpec((sb, n, KV_LORA), lambda b, i: (b, 0, 0)),
                      pl.BlockSpec((sb, n, MLA_ROPE), lambda b, i: (b, 0, 0))]
        out_shape += [jax.ShapeDtypeStruct((b_sz, n, KV_LORA), f32),
                      jax.ShapeDtypeStruct((b_sz, n, MLA_ROPE), f32)]
    return pl.pallas_call(
        functools.partial(_mla_kernel, n_ctx, rope is not None, want_cache, tq),
        grid=(b_sz // sb, n // tq),
        in_specs=in_specs,
        out_specs=out_specs,
        out_shape=out_shape,
        scratch_shapes=[pltpu.VMEM((MLA_HEADS, QHEAD_PAD, sb * sk), bf16),
                        pltpu.VMEM((MLA_HEADS, sb * sk, 2 * MLA_V), bf16),
                        pltpu.VMEM((MLA_HEADS, tq, MLA_V), bf16)],
        compiler_params=_params(("parallel", "arbitrary")),
        name="mla",
    )(*args)


def _rope_tables(n):
    t = np.arange(n)
    r = (t // GRID_W).astype(np.float32)
    col = (t % GRID_W).astype(np.float32)
    nf = MLA_ROPE // 4
    inv = np.float32(ROPE_BASE) ** (-np.arange(nf, dtype=np.float32) / np.float32(nf))
    ar = r[:, None] * inv
    ac = col[:, None] * inv
    ang = np.concatenate([ar, ar, ac, ac], axis=-1).astype(np.float32)
    pad = LANES - MLA_ROPE
    cos = np.concatenate([np.cos(ang), np.ones((n, pad), np.float32)], axis=-1)
    sin = np.concatenate([np.sin(ang), np.zeros((n, pad), np.float32)], axis=-1)
    return jnp.asarray(cos, dtype=f32), jnp.asarray(sin, dtype=f32)


def _outproj_kernel(x_ref, yfn_ref, of_ref, ob_ref, gate_ref, ymla_ref, w_ref, hgg_ref, gpost_ref,
                    mod_ref, o_ref, inv_scr):
    o_hg = of_ref[0] + ob_ref[0]
    gate = gate_ref[0]
    gate = gate * jax.nn.sigmoid(gate)
    parts = [yfn_ref[0]]
    for h in range(HG_HEADS):
        sl = slice(h * HG_DV, (h + 1) * HG_DV)
        parts.append((_rms(o_hg[:, sl]) * hgg_ref[:, sl] * gate[:, sl]).astype(bf16))
    parts.append(ymla_ref[0])
    o_ref[0] = _dot(jnp.concatenate(parts, axis=1), w_ref[0])
    y = o_ref[0]
    inv_scr[...] = lax.rsqrt(jnp.mean(y * y, axis=-1, keepdims=True) + EPS)
    gain = mod_ref[0, 2:3, :] * gpost_ref[...]
    o_ref[0] = x_ref[0] + o_ref[0] * inv_scr[...] * gain


def _outproj(x, y_fn, o_f, o_b, proj, y_mla, w_out, hg_gain, g_post, mod, layer, tm):
    b_sz, n, _ = x.shape
    tok = lambda width, col=0: pl.BlockSpec((1, tm, width), lambda b, i: (b, i, col))
    return pl.pallas_call(
        _outproj_kernel,
        grid=(b_sz, n // tm),
        in_specs=[
            tok(D_MODEL), tok(FN_WIDTH), tok(HG_WIDTH), tok(HG_WIDTH), tok(HG_WIDTH, COL_HGATE),
            tok(MLA_WIDTH),
            pl.BlockSpec((1, D_MODEL, D_MODEL), lambda b, i: (layer, 0, 0)),
            pl.BlockSpec((1, HG_WIDTH), lambda b, i: (0, 0)),
            pl.BlockSpec((1, D_MODEL), lambda b, i: (0, 0)),
            pl.BlockSpec((1, 6, D_MODEL), lambda b, i: (b, 0, 0)),
        ],
        out_specs=tok(D_MODEL),
        out_shape=jax.ShapeDtypeStruct((b_sz, n, D_MODEL), f32),
        scratch_shapes=[pltpu.VMEM((tm, 1), f32)],
        compiler_params=_params(("parallel", "parallel")),
        name="outproj",
    )(x, y_fn, o_f, o_b, proj, y_mla, w_out, hg_gain, g_post, mod)


def _ffn_kernel(x_ref, mod_ref, gpre_ref, gpost_ref, w1_ref, w2_ref, o_ref, h_scr, inv_scr):
    k = pl.program_id(2)

    def hidden_slice():
        a = jnp.maximum(_dot(h_scr[...], w1_ref[0, 0]), 0.0)
        return _dot((a * a).astype(bf16), w2_ref[0])

    @pl.when(k == 0)
    def _():
        _modulated_norm(x_ref, gpre_ref, mod_ref[0, 4:5, :], mod_ref[0, 3:4, :], h_scr, inv_scr)
        o_ref[0] = hidden_slice()

    last = pl.num_programs(2) - 1

    @pl.when((k > 0) & (k < last))
    def _():
        o_ref[0] += hidden_slice()

    @pl.when(k == last)
    def _():
        o_ref[0] += hidden_slice()
        f = o_ref[0]
        inv_scr[...] = lax.rsqrt(jnp.mean(f * f, axis=-1, keepdims=True) + EPS)
        gain = mod_ref[0, 5:6, :] * gpost_ref[...]
        o_ref[0] = x_ref[0] + o_ref[0] * inv_scr[...] * gain


def _ffn(x, mod, g_pre, g_post, w1, w2, layer, tm, tf):
    b_sz, n, _ = x.shape
    return pl.pallas_call(
        _ffn_kernel,
        grid=(b_sz, n // tm, D_FF // tf),
        in_specs=[
            pl.BlockSpec((1, tm, D_MODEL), lambda b, i, k: (b, i, 0)),
            pl.BlockSpec((1, 6, D_MODEL), lambda b, i, k: (b, 0, 0)),
            pl.BlockSpec((1, D_MODEL), lambda b, i, k: (0, 0)),
            pl.BlockSpec((1, D_MODEL), lambda b, i, k: (0, 0)),
            pl.BlockSpec((1, 1, D_MODEL, tf), lambda b, i, k: (layer, k, 0, 0)),
            pl.BlockSpec((1, tf, D_MODEL), lambda b, i, k: (layer, k, 0)),
        ],
        out_specs=pl.BlockSpec((1, tm, D_MODEL), lambda b, i, k: (b, i, 0)),
        out_shape=jax.ShapeDtypeStruct((b_sz, n, D_MODEL), f32),
        scratch_shapes=[pltpu.VMEM((tm, D_MODEL), bf16), pltpu.VMEM((tm, 1), f32)],
        compiler_params=_params(("parallel", "parallel", "arbitrary")),
        name="ffn",
    )(x, mod, g_pre, g_post, w1, w2)


def _layer(x, mod, wl, hg_tables, dft, layer, n_seq, ctx, s0, rope, is_context, tq):
    b_sz, n, _ = x.shape
    n_seqs = b_sz * n // n_seq
    proj = _inproj(x, mod, wl["g_pre_mix"], wl["w_in"], layer, INPROJ_TM).reshape(n_seqs, n_seq, IN_COLS_PAD)
    y_fn = _fourier(proj, dft[0], dft[1], dft[2], min(n_seq, 512))
    hg = _hgrn(proj, wl["hg_lb"], hg_tables, layer, s0, is_context)
    mla = _mla(proj, wl["wq"], wl["wkt"], wl["wv"], wl["q_norm"], wl["kv_norm"], ctx, layer, rope, is_context,
               tq)
    flat = lambda a: a.reshape(b_sz, n, a.shape[-1])
    x = _outproj(x, flat(y_fn), flat(hg[0]), flat(hg[1]), flat(proj), flat(mla[0]), wl["w_out"],
                 wl["hg_gain"], wl["g_post_mix"], mod, layer, OUTPROJ_TM)
    x = _ffn(x, mod, wl["g_pre_ff"], wl["g_post_ff"], wl["w_ff1"], wl["w_ff2"], layer, FFN_TM, FFN_TF)
    extras = (mla[1], mla[2], hg[2]) if is_context else None
    return x, extras


def kernel(x_prompt, x_sample, c, cache_ckv, cache_krope, state_hgrn, c_ctx, w_ada, b_ada, g_pre_mix,
           g_post_mix, g_pre_ff, g_post_ff, w_in, hg_lb, hg_gain, mla_q_norm, mla_kv_norm, w_q_b, w_kv_b,
           w_out, w_ff1, w_ff2):
    batch, seq, _ = x_prompt.shape
    dec_batch, dec_seq, _ = x_sample.shape

    rows = 16
    cond = jnp.concatenate([c_ctx[None, :], c, jnp.zeros((rows - 1 - dec_batch, D_MODEL), f32)], axis=0)
    mod = _ada(cond, w_ada, b_ada).reshape(DEPTH, rows, 6, D_MODEL)

    hg_tables = _hgrn_tables()
    ccs = _channel_tables()
    dft_p = (ccs,) + _dft_tables(seq)
    dft_s = (ccs,) + _dft_tables(dec_seq)
    rope = _rope_tables(dec_seq)

    yp = x_prompt.reshape(1, batch * seq, D_MODEL)
    ys = x_sample
    ckv_list, kr_list, st_list = [], [], []
    wq = w_q_b.reshape(DEPTH, Q_LORA, MLA_HEADS, MLA_NOPE + MLA_ROPE).astype(bf16)
    wq = jnp.pad(wq, ((0, 0), (0, 0), (0, 0), (0, QHEAD_PAD - MLA_NOPE - MLA_ROPE)))
    wkv = w_kv_b.astype(bf16).reshape(DEPTH, KV_LORA, MLA_HEADS, MLA_NOPE + MLA_V)
    stacked = {
        "w_in": jnp.pad(w_in.astype(bf16), ((0, 0), (0, 0), (0, IN_COLS_PAD - IN_COLS))).reshape(
            DEPTH, D_MODEL, IN_COLS_PAD // IN_TILE, IN_TILE).transpose(0, 2, 1, 3),
        "wq": wq.transpose(0, 2, 1, 3),
        "wkt": wkv[:, :, :, :MLA_NOPE].reshape(DEPTH, KV_LORA, MLA_HEADS * MLA_NOPE).transpose(0, 2, 1),
        "wv": wkv[:, :, :, MLA_NOPE:].reshape(DEPTH, KV_LORA, MLA_HEADS * MLA_V),
        "w_out": w_out.astype(bf16),
        "w_ff1": w_ff1.astype(bf16).reshape(DEPTH, D_MODEL, D_FF // FFN_TF, FFN_TF).transpose(0, 2, 1, 3),
        "w_ff2": w_ff2.astype(bf16),
    }
    for l in range(DEPTH):
        wl = {
            **stacked,
            "g_pre_mix": g_pre_mix[l][None, :], "g_post_mix": g_post_mix[l][None, :],
            "g_pre_ff": g_pre_ff[l][None, :], "g_post_ff": g_post_ff[l][None, :],
            "hg_lb": hg_lb, "hg_gain": hg_gain[l][None, :],
            "q_norm": mla_q_norm[l][None, :], "kv_norm": mla_kv_norm[l][None, :],
        }
        yp, (ckv_l, kr_l, st_l) = _layer(yp, mod[l, 0:1], wl, hg_tables, dft_p, l, seq, None, None, None,
                                         True, seq)
        ckv_list.append(ckv_l)
        kr_list.append(kr_l)
        st_list.append(st_l)
        ys, _ = _layer(ys, mod[l, 1:1 + dec_batch], wl, hg_tables, dft_s, l, dec_seq,
                       (cache_ckv, cache_krope), state_hgrn, rope, False, 1024)

    return (yp.reshape(batch, seq, D_MODEL), ys, jnp.stack(ckv_list, axis=1), jnp.stack(kr_list, axis=1),
            jnp.stack(st_list, axis=1))
```

```python
import functools

import numpy as np
import jax
import jax.numpy as jnp
from jax import lax
from jax.experimental import pallas as pl
from jax.experimental.pallas import tpu as pltpu

f32 = jnp.float32
bf16 = jnp.bfloat16

D_MODEL = 2048
DEPTH = 2
GRID_W = 64
FN_HEADS = 4
FN_DH = 128
FN_WIDTH = FN_HEADS * FN_DH
HG_HEADS = 4
HG_DK = 128
HG_DV = 128
HG_QK = HG_HEADS * HG_DK
HG_WIDTH = HG_HEADS * HG_DV
MLA_HEADS = 8
MLA_NOPE = 128
MLA_ROPE = 64
MLA_V = 128
Q_LORA = 768
KV_LORA = 512
MLA_WIDTH = MLA_HEADS * MLA_V
D_FF = 4 * D_MODEL
ROPE_BASE = 10000.0
EPS = 1e-6
F_FLOOR = 1e-30

LANES = 128
SUBLANES = 8
MXU_DIM = 256
IN_COLS = 4416
IN_COLS_PAD = 4608
IN_TILE = 1536
COL_HQ, COL_HV, COL_HFF, COL_HFB, COL_HGATE = 1, 2, 3, 4, 5
COL_QA = 4
KVR_WIDTH = 640
COL_KVR = 6
QHEAD_PAD = 256
HG_CHUNK = 128
HG_LEVELS = (64, 32, 16, 8, 4, 2, 1)
HG_SPLIT = 2
HG_SEQS = 4
INPROJ_TM = 1024
OUTPROJ_TM = 512
MLA_KEY_CHUNK = 1280
MLA_SEQS = 4
FFN_TM = 1024
FFN_TF = 1024
VMEM_LIMIT = 56 * 1024 * 1024
VMEM_LIMIT_FFN = 62 * 1024 * 1024


def _params(semantics, vmem_limit=VMEM_LIMIT):
    return pltpu.CompilerParams(dimension_semantics=semantics, vmem_limit_bytes=vmem_limit)


def _dot(a, b):
    return jnp.dot(a, b, preferred_element_type=f32)


def _dot_nt(a, b):
    return lax.dot_general(a, b, (((1,), (1,)), ((), ())), preferred_element_type=f32)


def _dot_tn(a, b):
    return lax.dot_general(a, b, (((0,), (0,)), ((), ())), preferred_element_type=f32)


def _rms(x):
    return x * lax.rsqrt(jnp.mean(x * x, axis=-1, keepdims=True) + EPS)


def _modulated_norm(x_ref, g_ref, scale_row, shift_row, h_scr, inv_scr):
    x = x_ref[0]
    inv_scr[...] = lax.rsqrt(jnp.mean(x * x, axis=-1, keepdims=True) + EPS)
    gain = g_ref[...] * (1.0 + scale_row)
    h_scr[...] = (x_ref[0] * inv_scr[...] * gain + shift_row).astype(bf16)


def _ada_kernel(cond_ref, w_ref, b_ref, o_ref):
    cnd = cond_ref[...]
    act = cnd * jax.nn.sigmoid(cnd)
    o_ref[0] = _dot(act.astype(bf16), w_ref[0].astype(bf16)) + b_ref[0]


def _ada(cond, w_ada, b_ada):
    rows = cond.shape[0]
    tn = 1024
    return pl.pallas_call(
        _ada_kernel,
        grid=(DEPTH, 6 * D_MODEL // tn),
        in_specs=[
            pl.BlockSpec((rows, D_MODEL), lambda l, j: (0, 0)),
            pl.BlockSpec((1, D_MODEL, tn), lambda l, j: (l, 0, j)),
            pl.BlockSpec((1, 1, tn), lambda l, j: (l, 0, j)),
        ],
        out_specs=pl.BlockSpec((1, rows, tn), lambda l, j: (l, 0, j)),
        out_shape=jax.ShapeDtypeStruct((DEPTH, rows, 6 * D_MODEL), f32),
        compiler_params=_params(("parallel", "parallel")),
        name="ada_mod",
    )(cond, w_ada, b_ada.reshape(DEPTH, 1, 6 * D_MODEL))


def _inproj_kernel(x_ref, mod_ref, g_ref, w_ref, o_ref, h_scr, inv_scr):
    @pl.when(pl.program_id(2) == 0)
    def _():
        _modulated_norm(x_ref, g_ref, mod_ref[0, 1:2, :], mod_ref[0, 0:1, :], h_scr, inv_scr)
        o_ref[0] = _dot(h_scr[...], w_ref[0])

    @pl.when(pl.program_id(2) > 0)
    def _():
        o_ref[0] = _dot(h_scr[...], w_ref[0])


def _inproj(x, mod, g, w_in, layer, tm):
    b_sz, n, _ = x.shape
    return pl.pallas_call(
        _inproj_kernel,
        grid=(b_sz, n // tm, IN_COLS_PAD // IN_TILE),
        in_specs=[
            pl.BlockSpec((1, tm, D_MODEL), lambda b, i, j: (b, i, 0)),
            pl.BlockSpec((1, 6, D_MODEL), lambda b, i, j: (b, 0, 0)),
            pl.BlockSpec((1, D_MODEL), lambda b, i, j: (0, 0)),
            pl.BlockSpec((1, D_MODEL, IN_TILE), lambda b, i, j: (layer, 0, j)),
        ],
        out_specs=pl.BlockSpec((1, tm, IN_TILE), lambda b, i, j: (b, i, j)),
        out_shape=jax.ShapeDtypeStruct((b_sz, n, IN_COLS_PAD), f32),
        scratch_shapes=[pltpu.VMEM((tm, D_MODEL), bf16), pltpu.VMEM((tm, 1), f32)],
        compiler_params=_params(("parallel", "parallel", "arbitrary")),
        name="inproj",
    )(x, mod, g, w_in)


def _fourier_kernel(u_ref, ccs_ref, cn_ref, sn_ref, o_ref, ucs_scr):
    @pl.when(pl.program_id(1) == 0)
    def _():
        for h in range(FN_HEADS):
            sl = slice(h * FN_DH, (h + 1) * FN_DH)
            r = _dot(u_ref[0, :, sl].astype(bf16), ccs_ref[...]).astype(bf16)
            ucs_scr[:, sl] = r[:, :FN_DH]
            ucs_scr[:, FN_WIDTH + h * FN_DH:FN_WIDTH + (h + 1) * FN_DH] = r[:, FN_DH:]

    y = _dot(cn_ref[...], ucs_scr[:, :FN_WIDTH]) + _dot(sn_ref[...], ucs_scr[:, FN_WIDTH:])
    o_ref[0] = y.astype(bf16)


def _fourier(proj, ccs, cn, sn_neg, tr):
    b_sz, n, _ = proj.shape
    return pl.pallas_call(
        _fourier_kernel,
        grid=(b_sz, n // tr),
        in_specs=[
            pl.BlockSpec((1, n, FN_WIDTH), lambda b, i: (b, 0, 0)),
            pl.BlockSpec((FN_DH, 2 * FN_DH), lambda b, i: (0, 0)),
            pl.BlockSpec((tr, n), lambda b, i: (i, 0)),
            pl.BlockSpec((tr, n), lambda b, i: (i, 0)),
        ],
        out_specs=pl.BlockSpec((1, tr, FN_WIDTH), lambda b, i: (b, i, 0)),
        out_shape=jax.ShapeDtypeStruct((b_sz, n, FN_WIDTH), bf16),
        scratch_shapes=[pltpu.VMEM((n, 2 * FN_WIDTH), bf16)],
        compiler_params=_params(("parallel", "arbitrary")),
        name="fourier",
    )(proj, ccs, cn, sn_neg)


def _dft_tables(n):
    j = np.arange(n, dtype=np.int64)
    ang = ((j[:, None] * j[None, :]) % n).astype(np.float64) * (2.0 * np.pi / n)
    scale_n = 1.0 / np.sqrt(n)
    cn = jnp.asarray(np.cos(ang) * scale_n, dtype=bf16)
    sn_neg = jnp.asarray(-np.sin(ang) * scale_n, dtype=bf16)
    return cn, sn_neg


def _channel_tables():
    k = np.arange(FN_DH)
    ang = 2.0 * np.pi * ((k[:, None] * k[None, :]) % FN_DH) / FN_DH
    table = np.concatenate([np.cos(ang), np.sin(ang)], axis=1) / np.sqrt(FN_DH)
    return jnp.asarray(table, dtype=bf16)


def _hgrn_tables():
    c = HG_CHUNK
    idx = np.arange(c)
    blocks = [(idx[None, :] <= idx[:, None])]
    masks = []
    for h in HG_LEVELS:
        mid = (idx // (2 * h)) * (2 * h) + h
        upper = idx >= mid
        if h < SUBLANES:
            row_up = (idx[None, :] >= mid[:, None]) & (idx[None, :] <= idx[:, None])
            row_lo = (idx[None, :] > idx[:, None]) & (idx[None, :] < mid[:, None])
            blocks.append(np.where(upper[:, None], row_up, row_lo))
        same = (idx[:, None] // (2 * h)) == (idx[None, :] // (2 * h))
        masks.append(same & upper[:, None] & ~upper[None, :])
    masks.append(np.eye(c, dtype=bool))
    a_f = np.concatenate([b.astype(np.float32) for b in blocks], axis=0)
    a_b = np.concatenate([b[::-1, ::-1].astype(np.float32) for b in blocks], axis=0)
    m_f = np.stack([m.astype(np.float32) for m in masks])
    m_b = np.stack([m[::-1, ::-1].astype(np.float32) for m in masks])
    rep = lambda a: jnp.asarray(np.concatenate([a] * HG_SPLIT, axis=1), dtype=bf16)
    return rep(a_f), rep(a_b), jnp.asarray(m_f), jnp.asarray(m_b)


def _block_row(x, h, row):
    c, w = x.shape
    xb = x.reshape(c // (2 * h), 2 * h, w)
    return jnp.broadcast_to(xb[:, row:row + 1, :], xb.shape).reshape(c, w)


def _hgrn_direction(q, v, pre, lb, a_ref, m_ref, states, backward):
    c = HG_CHUNK
    nl = len(HG_LEVELS)
    f = lb + (1.0 - lb) * jax.nn.sigmoid(pre)
    lf = jnp.log(jnp.maximum(f, F_FLOOR))
    kk = 1.0 - f
    pieces = []
    rem = lf
    for _ in range(HG_SPLIT):
        piece = rem.astype(bf16)
        pieces.append(piece)
        rem = rem - piece.astype(f32)
    sums = _dot(a_ref[...], jnp.concatenate(pieces, axis=0))
    cum = sums[0:c]
    total = cum[0:1] if backward else cum[c - 1:c]
    q16 = q.astype(bf16)
    k16 = kk.astype(bf16)
    v16 = v.astype(bf16)
    decay16 = lambda log_decay: jnp.exp(log_decay.astype(bf16))
    q_inter = q16 * decay16(cum)
    k_state = k16 * decay16(jnp.minimum(total - cum, 0.0))
    e_total = jnp.exp(total)
    q_levels, k_levels = [q16], [k16]
    fine = 0
    for h in HG_LEVELS:
        if h >= SUBLANES:
            ref = _block_row(cum, h, h if backward else h - 1)
            el = decay16(-jnp.abs(cum - ref))
        else:
            fine += 1
            el = decay16(sums[fine * c:(fine + 1) * c])
        q_levels.append(q16 * el)
        k_levels.append(k16 * el)
    outs, new_states = [], []
    for h in range(HG_HEADS):
        sl = slice(h * HG_DK, (h + 1) * HG_DK)
        vh = v16[:, sl]
        sc = m_ref[nl] * _dot_nt(q_levels[0][:, sl], k_levels[0][:, sl])
        for li in range(nl):
            sc = sc + m_ref[li] * _dot_nt(q_levels[li + 1][:, sl], k_levels[li + 1][:, sl])
        st = states[h]
        outs.append(_dot(sc.astype(bf16), vh) + _dot_nt(q_inter[:, sl], st.astype(bf16)))
        new_states.append(e_total[:, sl] * st + _dot_tn(vh, k_state[:, sl]))
    return outs, new_states


def _hgrn_kernel(layer, has_s0, has_sout, *refs):
    (qf_ref, vf_ref, ff_ref, qb_ref, vb_ref, fb_ref, lb_ref, af_ref, ab_ref, mf_ref, mb_ref) = refs[:11]
    pos = 11
    s0_ref = None
    if has_s0:
        s0_ref = refs[pos]
        pos += 1
    of_ref, ob_ref = refs[pos], refs[pos + 1]
    pos += 2
    sout_ref = None
    if has_sout:
        sout_ref = refs[pos]
        pos += 1
    st_scr = refs[pos]
    ci = pl.program_id(1)

    n_seq = qf_ref.shape[0]

    @pl.when(ci == 0)
    def _():
        for s in range(n_seq):
            for d in range(2):
                for h in range(HG_HEADS):
                    if has_s0:
                        st_scr[s, d, h] = s0_ref[s, 0, d, h].T
                    else:
                        st_scr[s, d, h] = jnp.zeros((HG_DV, HG_DK), f32)

    raw = lb_ref[...]
    mx = raw[0]
    for i in range(1, DEPTH):
        mx = jnp.maximum(mx, raw[i])
    ex = [jnp.exp(raw[i] - mx) for i in range(DEPTH)]
    den = ex[0]
    for i in range(1, DEPTH):
        den = den + ex[i]
    lb = jnp.zeros_like(den)
    for i in range(1, layer + 1):
        lb = lb + ex[i] / den

    loaded = [[[st_scr[s, d, h] for h in range(HG_HEADS)] for d in range(2)] for s in range(n_seq)]
    results = []
    for s in range(n_seq):
        o_f, st_f = _hgrn_direction(qf_ref[s], vf_ref[s], ff_ref[s], lb[0:1], af_ref, mf_ref, loaded[s][0], False)
        o_b, st_b = _hgrn_direction(qb_ref[s], vb_ref[s], fb_ref[s], lb[1:2], ab_ref, mb_ref, loaded[s][1], True)
        of_ref[s] = jnp.concatenate(o_f, axis=1)
        ob_ref[s] = jnp.concatenate(o_b, axis=1)
        results.append((st_f, st_b))
    for s in range(n_seq):
        for h in range(HG_HEADS):
            st_scr[s, 0, h] = results[s][0][h]
            st_scr[s, 1, h] = results[s][1][h]

    if has_sout:
        @pl.when(ci == pl.num_programs(1) - 1)
        def _():
            for s in range(n_seq):
                for d in range(2):
                    for h in range(HG_HEADS):
                        sout_ref[s, d, h] = st_scr[s, d, h].T


def _hgrn(proj, hg_lb, tables, layer, s0, want_state):
    b_sz, n, _ = proj.shape
    nc = n // HG_CHUNK
    a_f, a_b, m_f, m_b = tables
    c = HG_CHUNK
    sb = HG_SEQS
    fwd = lambda col: pl.BlockSpec((sb, c, HG_QK), lambda b, i: (b, i, col))
    bwd = lambda col: pl.BlockSpec((sb, c, HG_QK), lambda b, i: (b, nc - 1 - i, col))
    const = lambda shape: pl.BlockSpec(shape, lambda b, i: (0,) * len(shape))
    in_specs = [fwd(COL_HQ), fwd(COL_HV), fwd(COL_HFF), bwd(COL_HQ), bwd(COL_HV), bwd(COL_HFB),
                const(hg_lb.shape), const(a_f.shape), const(a_b.shape), const(m_f.shape), const(m_b.shape)]
    args = [proj, proj, proj, proj, proj, proj, hg_lb, a_f, a_b, m_f, m_b]
    state_spec = pl.BlockSpec((sb, 2, HG_HEADS, HG_DK, HG_DV), lambda b, i: (b, 0, 0, 0, 0))
    if s0 is not None:
        in_specs.append(pl.BlockSpec((sb, 1, 2, HG_HEADS, HG_DK, HG_DV), lambda b, i: (b, layer, 0, 0, 0, 0)))
        args.append(s0)
    out_specs = [pl.BlockSpec((sb, c, HG_WIDTH), lambda b, i: (b, i, 0)),
                 pl.BlockSpec((sb, c, HG_WIDTH), lambda b, i: (b, nc - 1 - i, 0))]
    out_shape = [jax.ShapeDtypeStruct((b_sz, n, HG_WIDTH), f32)] * 2
    if want_state:
        out_specs.append(state_spec)
        out_shape.append(jax.ShapeDtypeStruct((b_sz, 2, HG_HEADS, HG_DK, HG_DV), f32))
    return pl.pallas_call(
        functools.partial(_hgrn_kernel, layer, s0 is not None, want_state),
        grid=(b_sz // sb, nc),
        in_specs=in_specs,
        out_specs=out_specs,
        out_shape=out_shape,
        scratch_shapes=[pltpu.VMEM((sb, 2, HG_HEADS, HG_DV, HG_DK), f32)],
        compiler_params=_params(("parallel", "arbitrary")),
        name="hgrn",
    )(*args)


def _rope_rotate(x, cos, sin):
    lane = lax.broadcasted_iota(jnp.int32, x.shape, 1)
    first_half = (lane % (MLA_ROPE // 2)) < (MLA_ROPE // 4)
    quarter = MLA_ROPE // 4
    rot = jnp.where(first_half, -pltpu.roll(x, LANES - quarter, 1), pltpu.roll(x, quarter, 1))
    return x * cos + rot * sin


def _mla_kernel(n_ctx, use_rope, want_cache, tq, *refs):
    qa_ref, kvr_ref, wq_ref, wkt_ref, wv_ref, qn_ref, kvn_ref = refs[:7]
    pos = 7
    if n_ctx:
        cckv_ref, ckr_ref = refs[pos], refs[pos + 1]
        pos += 2
    if use_rope:
        cosq_ref, sinq_ref, cosk_ref, sink_ref = refs[pos:pos + 4]
        pos += 4
    y_ref = refs[pos]
    pos += 1
    if want_cache:
        ockv_ref, okr_ref = refs[pos], refs[pos + 1]
        pos += 2
    kcat_scr, v_scr, o_scr = refs[pos:pos + 3]
    n_seq, n = kvr_ref.shape[0], kvr_ref.shape[1]
    sk = n_ctx + n
    total = n_seq * n
    rows = 512 if total % 512 == 0 else total

    def pieces(r0, nrows):
        out, r = [], r0
        while r < r0 + nrows:
            s, off = divmod(r, n)
            cnt = min(n - off, r0 + nrows - r)
            out.append((s, off, cnt))
            r += cnt
        return out

    def expand(r0, nrows, c_kv, krp):
        k_t = _dot_nt(wkt_ref[0], c_kv)
        eye = (lax.broadcasted_iota(jnp.int32, (LANES, LANES), 0)
               == lax.broadcasted_iota(jnp.int32, (LANES, LANES), 1)).astype(bf16)
        krp_t = _dot_nt(eye, krp).astype(bf16)
        vals = _dot(c_kv, wv_ref[0])
        ones = jnp.ones((nrows, MLA_V), bf16)
        for h in range(MLA_HEADS):
            kcat_scr[h, 0:MLA_NOPE, r0:r0 + nrows] = k_t[h * MLA_NOPE:(h + 1) * MLA_NOPE].astype(bf16)
            kcat_scr[h, MLA_NOPE:QHEAD_PAD, r0:r0 + nrows] = krp_t
            v_scr[h, r0:r0 + nrows, 0:MLA_V] = vals[:, h * MLA_V:(h + 1) * MLA_V].astype(bf16)
            v_scr[h, r0:r0 + nrows, MLA_V:2 * MLA_V] = ones

    @pl.when(pl.program_id(1) == 0)
    def _():
        if n_ctx:
            zeros = jnp.zeros((n_ctx, LANES - MLA_ROPE), f32)
            krp_ctx = jnp.concatenate([ckr_ref[0, 0], zeros], axis=1).astype(bf16)
            expand(0, n_ctx, cckv_ref[0, 0].astype(bf16), krp_ctx)
        for r0 in range(0, total, rows):
            parts = pieces(r0, rows)
            blk = jnp.concatenate([kvr_ref[s, off:off + cnt] for s, off, cnt in parts], axis=0)
            c_kv = _rms(blk[:, :KV_LORA]) * kvn_ref[...]
            krp = blk[:, KV_LORA:]
            if want_cache:
                at = 0
                for s, off, cnt in parts:
                    ockv_ref[s, off:off + cnt] = c_kv[at:at + cnt]
                    okr_ref[s, off:off + cnt] = krp[at:at + cnt, :MLA_ROPE]
                    at += cnt
            if use_rope:
                krp = _rope_rotate(krp, cosk_ref[r0:r0 + rows], sink_ref[r0:r0 + rows])
            expand(n_ctx + r0, rows, c_kv.astype(bf16), krp.astype(bf16))

    scale = float(MLA_NOPE + MLA_ROPE) ** -0.5
    qa = jnp.concatenate([qa_ref[s] for s in range(n_seq)], axis=0)
    qn = (_rms(qa) * qn_ref[...]).astype(bf16)
    n_kc = 1 if sk <= MLA_KEY_CHUNK else sk // (2 * MXU_DIM)
    kc = sk // n_kc

    def head(h):
        q = _dot(qn, wq_ref[0, h]) * scale
        q_rope = q[:, MLA_NOPE:]
        if use_rope:
            q_rope = _rope_rotate(q_rope, cosq_ref[...], sinq_ref[...])
        q_cat = jnp.concatenate([q[:, :MLA_NOPE], q_rope], axis=1).astype(bf16)
        outs = []
        for s in range(n_seq):
            q_s = q_cat[s * tq:(s + 1) * tq]
            m_run = acc = None
            for c in range(n_kc):
                k0 = s * sk + c * kc
                sc = _dot(q_s, kcat_scr[h, :, k0:k0 + kc])
                m_c = jnp.max(sc, axis=-1, keepdims=True)
                m_new = m_c if c == 0 else jnp.maximum(m_run, m_c)
                p = jnp.exp((sc - m_new).astype(bf16))
                pv = _dot(p, v_scr[h, k0:k0 + kc])
                acc = pv if c == 0 else jnp.exp(m_run - m_new) * acc + pv
                m_run = m_new
            outs.append((acc[:, :MLA_V] / acc[:, MLA_V:]).astype(bf16))
        return outs

    if n_kc == 1:
        for h in range(MLA_HEADS):
            for s, o in enumerate(head(h)):
                y_ref[s, :, h * MLA_V:(h + 1) * MLA_V] = o
    else:
        def body(h, carry):
            o_scr[h] = head(h)[0]
            return carry

        lax.fori_loop(0, MLA_HEADS, body, 0, unroll=4)
        for h in range(MLA_HEADS):
            y_ref[0, :, h * MLA_V:(h + 1) * MLA_V] = o_scr[h]


def _mla(proj, wq, wkt, wv, q_norm, kv_norm, ctx, layer, rope, want_cache, tq):
    b_sz, n, _ = proj.shape
    n_ctx = 0 if ctx is None else ctx[0].shape[2]
    sk = n_ctx + n
    sb = MLA_SEQS if (ctx is None and rope is None and n == tq) else 1
    single = pl.Buffered(1)
    const = lambda shape: pl.BlockSpec(shape, lambda b, i: (0,) * len(shape), pipeline_mode=single)
    in_specs = [
        pl.BlockSpec((sb, tq, Q_LORA), lambda b, i: (b, i, COL_QA)),
        pl.BlockSpec((sb, n, KVR_WIDTH), lambda b, i: (b, 0, COL_KVR), pipeline_mode=single),
        pl.BlockSpec((1,) + wq.shape[1:], lambda b, i: (layer, 0, 0, 0), pipeline_mode=single),
        pl.BlockSpec((1,) + wkt.shape[1:], lambda b, i: (layer, 0, 0), pipeline_mode=single),
        pl.BlockSpec((1,) + wv.shape[1:], lambda b, i: (layer, 0, 0), pipeline_mode=single),
        const(q_norm.shape), const(kv_norm.shape),
    ]
    args = [proj, proj, wq, wkt, wv, q_norm, kv_norm]
    if ctx is not None:
        in_specs += [
            pl.BlockSpec((1, 1, n_ctx, KV_LORA), lambda b, i: (b, layer, 0, 0), pipeline_mode=single),
            pl.BlockSpec((1, 1, n_ctx, MLA_ROPE), lambda b, i: (b, layer, 0, 0), pipeline_mode=single)]
        args += [ctx[0], ctx[1]]
    if rope is not None:
        cos, sin = rope
        in_specs += [pl.BlockSpec((tq, LANES), lambda b, i: (i, 0)),
                     pl.BlockSpec((tq, LANES), lambda b, i: (i, 0)),
                     const(cos.shape), const(sin.shape)]
        args += [cos, sin, cos, sin]
    out_specs = [pl.BlockSpec((sb, tq, MLA_WIDTH), lambda b, i: (b, i, 0))]
    out_shape = [jax.ShapeDtypeStruct((b_sz, n, MLA_WIDTH), bf16)]
    if want_cache:
        out_specs += [pl.BlockSpec((sb, n, KV_LORA), lambda b, i: (b, 0, 0)),
                      pl.BlockSpec((sb, n, MLA_ROPE), lambda b, i: (b, 0, 0))]
        out_shape += [jax.ShapeDtypeStruct((b_sz, n, KV_LORA), f32),
                      jax.ShapeDtypeStruct((b_sz, n, MLA_ROPE), f32)]
    return pl.pallas_call(
        functools.partial(_mla_kernel, n_ctx, rope is not None, want_cache, tq),
        grid=(b_sz // sb, n // tq),
        in_specs=in_specs,
        out_specs=out_specs,
        out_shape=out_shape,
        scratch_shapes=[pltpu.VMEM((MLA_HEADS, QHEAD_PAD, sb * sk), bf16),
                        pltpu.VMEM((MLA_HEADS, sb * sk, 2 * MLA_V), bf16),
                        pltpu.VMEM((MLA_HEADS, tq, MLA_V), bf16)],
        compiler_params=_params(("parallel", "arbitrary")),
        name="mla",
    )(*args)


def _rope_tables(n):
    t = np.arange(n)
    r = (t // GRID_W).astype(np.float32)
    col = (t % GRID_W).astype(np.float32)
    nf = MLA_ROPE // 4
    inv = np.float32(ROPE_BASE) ** (-np.arange(nf, dtype=np.float32) / np.float32(nf))
    ar = r[:, None] * inv
    ac = col[:, None] * inv
    ang = np.concatenate([ar, ar, ac, ac], axis=-1).astype(np.float32)
    pad = LANES - MLA_ROPE
    cos = np.concatenate([np.cos(ang), np.ones((n, pad), np.float32)], axis=-1)
    sin = np.concatenate([np.sin(ang), np.zeros((n, pad), np.float32)], axis=-1)
    return jnp.asarray(cos, dtype=f32), jnp.asarray(sin, dtype=f32)


def _outproj_kernel(x_ref, yfn_ref, of_ref, ob_ref, gate_ref, ymla_ref, w_ref, hgg_ref, gpost_ref,
                    mod_ref, o_ref, inv_scr):
    o_hg = of_ref[0] + ob_ref[0]
    gate = gate_ref[0]
    gate = gate * jax.nn.sigmoid(gate)
    parts = [yfn_ref[0]]
    for h in range(HG_HEADS):
        sl = slice(h * HG_DV, (h + 1) * HG_DV)
        parts.append((_rms(o_hg[:, sl]) * hgg_ref[:, sl] * gate[:, sl]).astype(bf16))
    parts.append(ymla_ref[0])
    o_ref[0] = _dot(jnp.concatenate(parts, axis=1), w_ref[0])
    y = o_ref[0]
    inv_scr[...] = lax.rsqrt(jnp.mean(y * y, axis=-1, keepdims=True) + EPS)
    gain = mod_ref[0, 2:3, :] * gpost_ref[...]
    o_ref[0] = x_ref[0] + o_ref[0] * inv_scr[...] * gain


def _outproj(x, y_fn, o_f, o_b, proj, y_mla, w_out, hg_gain, g_post, mod, layer, tm):
    b_sz, n, _ = x.shape
    tok = lambda width, col=0: pl.BlockSpec((1, tm, width), lambda b, i: (b, i, col))
    return pl.pallas_call(
        _outproj_kernel,
        grid=(b_sz, n // tm),
        in_specs=[
            tok(D_MODEL), tok(FN_WIDTH), tok(HG_WIDTH), tok(HG_WIDTH), tok(HG_WIDTH, COL_HGATE),
            tok(MLA_WIDTH),
            pl.BlockSpec((1, D_MODEL, D_MODEL), lambda b, i: (layer, 0, 0)),
            pl.BlockSpec((1, HG_WIDTH), lambda b, i: (0, 0)),
            pl.BlockSpec((1, D_MODEL), lambda b, i: (0, 0)),
            pl.BlockSpec((1, 6, D_MODEL), lambda b, i: (b, 0, 0)),
        ],
        out_specs=tok(D_MODEL),
        out_shape=jax.ShapeDtypeStruct((b_sz, n, D_MODEL), f32),
        scratch_shapes=[pltpu.VMEM((tm, 1), f32)],
        compiler_params=_params(("parallel", "parallel")),
        name="outproj",
    )(x, y_fn, o_f, o_b, proj, y_mla, w_out, hg_gain, g_post, mod)


def _ffn_kernel(x_ref, mod_ref, gpre_ref, gpost_ref, w1_ref, w2_ref, o_ref, h_scr, inv_scr):
    k = pl.program_id(2)

    def hidden_slice():
        a = jnp.maximum(_dot(h_scr[...], w1_ref[0]), 0.0)
        return _dot((a * a).astype(bf16), w2_ref[0])

    @pl.when(k == 0)
    def _():
        _modulated_norm(x_ref, gpre_ref, mod_ref[0, 4:5, :], mod_ref[0, 3:4, :], h_scr, inv_scr)
        o_ref[0] = hidden_slice()

    last = pl.num_programs(2) - 1

    @pl.when((k > 0) & (k < last))
    def _():
        o_ref[0] += hidden_slice()

    @pl.when(k == last)
    def _():
        o_ref[0] += hidden_slice()
        f = o_ref[0]
        inv_scr[...] = lax.rsqrt(jnp.mean(f * f, axis=-1, keepdims=True) + EPS)
        gain = mod_ref[0, 5:6, :] * gpost_ref[...]
        o_ref[0] = x_ref[0] + o_ref[0] * inv_scr[...] * gain


def _ffn(x, mod, g_pre, g_post, w1, w2, layer, tm, tf):
    b_sz, n, _ = x.shape
    return pl.pallas_call(
        _ffn_kernel,
        grid=(b_sz, n // tm, D_FF // tf),
        in_specs=[
            pl.BlockSpec((1, tm, D_MODEL), lambda b, i, k: (b, i, 0)),
            pl.BlockSpec((1, 6, D_MODEL), lambda b, i, k: (b, 0, 0)),
            pl.BlockSpec((1, D_MODEL), lambda b, i, k: (0, 0)),
            pl.BlockSpec((1, D_MODEL), lambda b, i, k: (0, 0)),
            pl.BlockSpec((1, D_MODEL, tf), lambda b, i, k: (layer, 0, k)),
            pl.BlockSpec((1, tf, D_MODEL), lambda b, i, k: (layer, k, 0)),
        ],
        out_specs=pl.BlockSpec((1, tm, D_MODEL), lambda b, i, k: (b, i, 0)),
        out_shape=jax.ShapeDtypeStruct((b_sz, n, D_MODEL), f32),
        scratch_shapes=[pltpu.VMEM((tm, D_MODEL), bf16), pltpu.VMEM((tm, 1), f32)],
        compiler_params=_params(("parallel", "parallel", "arbitrary"), VMEM_LIMIT_FFN),
        name="ffn",
    )(x, mod, g_pre, g_post, w1, w2)


def _layer(x, mod, wl, hg_tables, dft, layer, n_seq, ctx, s0, rope, is_context, tq):
    b_sz, n, _ = x.shape
    n_seqs = b_sz * n // n_seq
    proj = _inproj(x, mod, wl["g_pre_mix"], wl["w_in"], layer, INPROJ_TM).reshape(n_seqs, n_seq, IN_COLS_PAD)
    y_fn = _fourier(proj, dft[0], dft[1], dft[2], min(n_seq, 512))
    hg = _hgrn(proj, wl["hg_lb"], hg_tables, layer, s0, is_context)
    mla = _mla(proj, wl["wq"], wl["wkt"], wl["wv"], wl["q_norm"], wl["kv_norm"], ctx, layer, rope, is_context,
               tq)
    flat = lambda a: a.reshape(b_sz, n, a.shape[-1])
    x = _outproj(x, flat(y_fn), flat(hg[0]), flat(hg[1]), flat(proj), flat(mla[0]), wl["w_out"],
                 wl["hg_gain"], wl["g_post_mix"], mod, layer, OUTPROJ_TM)
    x = _ffn(x, mod, wl["g_pre_ff"], wl["g_post_ff"], wl["w_ff1"], wl["w_ff2"], layer, FFN_TM, FFN_TF)
    extras = (mla[1], mla[2], hg[2]) if is_context else None
    return x, extras


def kernel(x_prompt, x_sample, c, cache_ckv, cache_krope, state_hgrn, c_ctx, w_ada, b_ada, g_pre_mix,
           g_post_mix, g_pre_ff, g_post_ff, w_in, hg_lb, hg_gain, mla_q_norm, mla_kv_norm, w_q_b, w_kv_b,
           w_out, w_ff1, w_ff2):
    batch, seq, _ = x_prompt.shape
    dec_batch, dec_seq, _ = x_sample.shape

    rows = 16
    cond = jnp.concatenate([c_ctx[None, :], c, jnp.zeros((rows - 1 - dec_batch, D_MODEL), f32)], axis=0)
    mod = _ada(cond, w_ada, b_ada).reshape(DEPTH, rows, 6, D_MODEL)

    hg_tables = _hgrn_tables()
    ccs = _channel_tables()
    dft_p = (ccs,) + _dft_tables(seq)
    dft_s = (ccs,) + _dft_tables(dec_seq)
    rope = _rope_tables(dec_seq)

    yp = x_prompt.reshape(1, batch * seq, D_MODEL)
    ys = x_sample
    ckv_list, kr_list, st_list = [], [], []
    wq = w_q_b.reshape(DEPTH, Q_LORA, MLA_HEADS, MLA_NOPE + MLA_ROPE).astype(bf16)
    wq = jnp.pad(wq, ((0, 0), (0, 0), (0, 0), (0, QHEAD_PAD - MLA_NOPE - MLA_ROPE)))
    wkv = w_kv_b.astype(bf16).reshape(DEPTH, KV_LORA, MLA_HEADS, MLA_NOPE + MLA_V)
    stacked = {
        "w_in": jnp.pad(w_in.astype(bf16), ((0, 0), (0, 0), (0, IN_COLS_PAD - IN_COLS))),
        "wq": wq.transpose(0, 2, 1, 3),
        "wkt": wkv[:, :, :, :MLA_NOPE].reshape(DEPTH, KV_LORA, MLA_HEADS * MLA_NOPE).transpose(0, 2, 1),
        "wv": wkv[:, :, :, MLA_NOPE:].reshape(DEPTH, KV_LORA, MLA_HEADS * MLA_V),
        "w_out": w_out.astype(bf16),
        "w_ff1": w_ff1.astype(bf16),
        "w_ff2": w_ff2.astype(bf16),
    }
    for l in range(DEPTH):
        wl = {
            **stacked,
            "g_pre_mix": g_pre_mix[l][None, :], "g_post_mix": g_post_mix[l][None, :],
            "g_pre_ff": g_pre_ff[l][None, :], "g_post_ff": g_post_ff[l][None, :],
            "hg_lb": hg_lb, "hg_gain": hg_gain[l][None, :],
            "q_norm": mla_q_norm[l][None, :], "kv_norm": mla_kv_norm[l][None, :],
        }
        yp, (ckv_l, kr_l, st_l) = _layer(yp, mod[l, 0:1], wl, hg_tables, dft_p, l, seq, None, None, None,
                                         True, seq)
        ckv_list.append(ckv_l)
        kr_list.append(kr_l)
        st_list.append(st_l)
        ys, _ = _layer(ys, mod[l, 1:1 + dec_batch], wl, hg_tables, dft_s, l, dec_seq,
                       (cache_ckv, cache_krope), state_hgrn, rope, False, 1024)

    return (yp.reshape(batch, seq, D_MODEL), ys, jnp.stack(ckv_list, axis=1), jnp.stack(kr_list, axis=1),
            jnp.stack(st_list, axis=1))
```

```python
import functools

import numpy as np
import jax
import jax.numpy as jnp
from jax import lax
from jax.experimental import pallas as pl
from jax.experimental.pallas import tpu as pltpu

f32 = jnp.float32
bf16 = jnp.bfloat16

D_MODEL = 2048
DEPTH = 2
GRID_W = 64
FN_HEADS = 4
FN_DH = 128
FN_WIDTH = FN_HEADS * FN_DH
HG_HEADS = 4
HG_DK = 128
HG_DV = 128
HG_QK = HG_HEADS * HG_DK
HG_WIDTH = HG_HEADS * HG_DV
MLA_HEADS = 8
MLA_NOPE = 128
MLA_ROPE = 64
MLA_V = 128
Q_LORA = 768
KV_LORA = 512
MLA_WIDTH = MLA_HEADS * MLA_V
D_FF = 4 * D_MODEL
ROPE_BASE = 10000.0
EPS = 1e-6
F_FLOOR = 1e-30

LANES = 128
SUBLANES = 8
MXU_DIM = 256
IN_COLS = 4416
IN_COLS_PAD = 4608
IN_TILE = 2304
COL_HQ, COL_HV, COL_HFF, COL_HFB, COL_HGATE = 1, 2, 3, 4, 5
COL_QA = 4
KVR_WIDTH = 640
COL_KVR = 6
QHEAD_PAD = 256
HG_CHUNK = 128
HG_LEVELS = (64, 32, 16, 8, 4, 2, 1)
HG_SPLIT = 2
HG_SEQS = 8
INPROJ_TM = 1024
OUTPROJ_TM = 512
MLA_KEY_CHUNK = 1280
MLA_SEQS = 4
FFN_TM = 1024
FFN_TF = 1024
VMEM_LIMIT = 56 * 1024 * 1024
VMEM_LIMIT_WIDE = 62 * 1024 * 1024


def _params(semantics, vmem_limit=VMEM_LIMIT):
    return pltpu.CompilerParams(dimension_semantics=semantics, vmem_limit_bytes=vmem_limit)


def _dot(a, b):
    return jnp.dot(a, b, preferred_element_type=f32)


def _dot_nt(a, b):
    return lax.dot_general(a, b, (((1,), (1,)), ((), ())), preferred_element_type=f32)


def _dot_tn(a, b):
    return lax.dot_general(a, b, (((0,), (0,)), ((), ())), preferred_element_type=f32)


def _rms(x):
    return x * lax.rsqrt(jnp.mean(x * x, axis=-1, keepdims=True) + EPS)


def _modulated_norm(x_ref, g_ref, scale_row, shift_row, h_scr, inv_scr):
    x = x_ref[0]
    inv_scr[...] = lax.rsqrt(jnp.mean(x * x, axis=-1, keepdims=True) + EPS)
    gain = g_ref[...] * (1.0 + scale_row)
    h_scr[...] = (x_ref[0] * inv_scr[...] * gain + shift_row).astype(bf16)


def _ada_kernel(cond_ref, w_ref, b_ref, o_ref):
    cnd = cond_ref[...]
    act = cnd * jax.nn.sigmoid(cnd)
    o_ref[0] = _dot(act.astype(bf16), w_ref[0].astype(bf16)) + b_ref[0]


def _ada(cond, w_ada, b_ada):
    rows = cond.shape[0]
    tn = 1024
    return pl.pallas_call(
        _ada_kernel,
        grid=(DEPTH, 6 * D_MODEL // tn),
        in_specs=[
            pl.BlockSpec((rows, D_MODEL), lambda l, j: (0, 0)),
            pl.BlockSpec((1, D_MODEL, tn), lambda l, j: (l, 0, j)),
            pl.BlockSpec((1, 1, tn), lambda l, j: (l, 0, j)),
        ],
        out_specs=pl.BlockSpec((1, rows, tn), lambda l, j: (l, 0, j)),
        out_shape=jax.ShapeDtypeStruct((DEPTH, rows, 6 * D_MODEL), f32),
        compiler_params=_params(("parallel", "parallel")),
        name="ada_mod",
    )(cond, w_ada, b_ada.reshape(DEPTH, 1, 6 * D_MODEL))


def _inproj_kernel(x_ref, mod_ref, g_ref, w_ref, o_ref, h_scr, inv_scr):
    @pl.when(pl.program_id(2) == 0)
    def _():
        _modulated_norm(x_ref, g_ref, mod_ref[0, 1:2, :], mod_ref[0, 0:1, :], h_scr, inv_scr)
        o_ref[0] = _dot(h_scr[...], w_ref[0])

    @pl.when(pl.program_id(2) > 0)
    def _():
        o_ref[0] = _dot(h_scr[...], w_ref[0])


def _inproj(x, mod, g, w_in, layer, tm):
    b_sz, n, _ = x.shape
    return pl.pallas_call(
        _inproj_kernel,
        grid=(b_sz, n // tm, IN_COLS_PAD // IN_TILE),
        in_specs=[
            pl.BlockSpec((1, tm, D_MODEL), lambda b, i, j: (b, i, 0)),
            pl.BlockSpec((1, 6, D_MODEL), lambda b, i, j: (b, 0, 0)),
            pl.BlockSpec((1, D_MODEL), lambda b, i, j: (0, 0)),
            pl.BlockSpec((1, D_MODEL, IN_TILE), lambda b, i, j: (layer, 0, j)),
        ],
        out_specs=pl.BlockSpec((1, tm, IN_TILE), lambda b, i, j: (b, i, j)),
        out_shape=jax.ShapeDtypeStruct((b_sz, n, IN_COLS_PAD), f32),
        scratch_shapes=[pltpu.VMEM((tm, D_MODEL), bf16), pltpu.VMEM((tm, 1), f32)],
        compiler_params=_params(("parallel", "parallel", "arbitrary"), VMEM_LIMIT_WIDE),
        name="inproj",
    )(x, mod, g, w_in)


def _fourier_kernel(u_ref, ccs_ref, cn_ref, sn_ref, o_ref, ucs_scr):
    @pl.when(pl.program_id(1) == 0)
    def _():
        for h in range(FN_HEADS):
            sl = slice(h * FN_DH, (h + 1) * FN_DH)
            r = _dot(u_ref[0, :, sl].astype(bf16), ccs_ref[...]).astype(bf16)
            ucs_scr[:, sl] = r[:, :FN_DH]
            ucs_scr[:, FN_WIDTH + h * FN_DH:FN_WIDTH + (h + 1) * FN_DH] = r[:, FN_DH:]

    y = _dot(cn_ref[...], ucs_scr[:, :FN_WIDTH]) + _dot(sn_ref[...], ucs_scr[:, FN_WIDTH:])
    o_ref[0] = y.astype(bf16)


def _fourier(proj, ccs, cn, sn_neg, tr):
    b_sz, n, _ = proj.shape
    return pl.pallas_call(
        _fourier_kernel,
        grid=(b_sz, n // tr),
        in_specs=[
            pl.BlockSpec((1, n, FN_WIDTH), lambda b, i: (b, 0, 0)),
            pl.BlockSpec((FN_DH, 2 * FN_DH), lambda b, i: (0, 0)),
            pl.BlockSpec((tr, n), lambda b, i: (i, 0)),
            pl.BlockSpec((tr, n), lambda b, i: (i, 0)),
        ],
        out_specs=pl.BlockSpec((1, tr, FN_WIDTH), lambda b, i: (b, i, 0)),
        out_shape=jax.ShapeDtypeStruct((b_sz, n, FN_WIDTH), bf16),
        scratch_shapes=[pltpu.VMEM((n, 2 * FN_WIDTH), bf16)],
        compiler_params=_params(("parallel", "arbitrary")),
        name="fourier",
    )(proj, ccs, cn, sn_neg)


def _dft_tables(n):
    j = np.arange(n, dtype=np.int64)
    ang = ((j[:, None] * j[None, :]) % n).astype(np.float64) * (2.0 * np.pi / n)
    scale_n = 1.0 / np.sqrt(n)
    cn = jnp.asarray(np.cos(ang) * scale_n, dtype=bf16)
    sn_neg = jnp.asarray(-np.sin(ang) * scale_n, dtype=bf16)
    return cn, sn_neg


def _channel_tables():
    k = np.arange(FN_DH)
    ang = 2.0 * np.pi * ((k[:, None] * k[None, :]) % FN_DH) / FN_DH
    table = np.concatenate([np.cos(ang), np.sin(ang)], axis=1) / np.sqrt(FN_DH)
    return jnp.asarray(table, dtype=bf16)


def _hgrn_tables():
    c = HG_CHUNK
    idx = np.arange(c)
    blocks = [(idx[None, :] <= idx[:, None])]
    masks = []
    for h in HG_LEVELS:
        mid = (idx // (2 * h)) * (2 * h) + h
        upper = idx >= mid
        if h < SUBLANES:
            row_up = (idx[None, :] >= mid[:, None]) & (idx[None, :] <= idx[:, None])
            row_lo = (idx[None, :] > idx[:, None]) & (idx[None, :] < mid[:, None])
            blocks.append(np.where(upper[:, None], row_up, row_lo))
        same = (idx[:, None] // (2 * h)) == (idx[None, :] // (2 * h))
        masks.append(same & upper[:, None] & ~upper[None, :])
    masks.append(np.eye(c, dtype=bool))
    a_f = np.concatenate([b.astype(np.float32) for b in blocks], axis=0)
    a_b = np.concatenate([b[::-1, ::-1].astype(np.float32) for b in blocks], axis=0)
    m_f = np.stack([m.astype(np.float32) for m in masks])
    m_b = np.stack([m[::-1, ::-1].astype(np.float32) for m in masks])
    rep = lambda a: jnp.asarray(np.concatenate([a] * HG_SPLIT, axis=1), dtype=bf16)
    return rep(a_f), rep(a_b), jnp.asarray(m_f), jnp.asarray(m_b)


def _block_row(x, h, row):
    c, w = x.shape
    xb = x.reshape(c // (2 * h), 2 * h, w)
    return jnp.broadcast_to(xb[:, row:row + 1, :], xb.shape).reshape(c, w)


def _hgrn_direction(q, v, pre, lb, a_ref, m_ref, states, backward):
    c = HG_CHUNK
    nl = len(HG_LEVELS)
    f = lb + (1.0 - lb) * jax.nn.sigmoid(pre)
    lf = jnp.log(jnp.maximum(f, F_FLOOR))
    kk = 1.0 - f
    pieces = []
    rem = lf
    for _ in range(HG_SPLIT):
        piece = rem.astype(bf16)
        pieces.append(piece)
        rem = rem - piece.astype(f32)
    sums = _dot(a_ref[...], jnp.concatenate(pieces, axis=0))
    cum = sums[0:c]
    total = cum[0:1] if backward else cum[c - 1:c]
    q16 = q.astype(bf16)
    k16 = kk.astype(bf16)
    v16 = v.astype(bf16)
    decay16 = lambda log_decay: jnp.exp(log_decay.astype(bf16))
    q_inter = q16 * decay16(cum)
    k_state = k16 * decay16(jnp.minimum(total - cum, 0.0))
    e_total = jnp.exp(total)
    q_levels, k_levels = [q16], [k16]
    fine = 0
    for h in HG_LEVELS:
        if h >= SUBLANES:
            ref = _block_row(cum, h, h if backward else h - 1)
            el = decay16(-jnp.abs(cum - ref))
        else:
            fine += 1
            el = decay16(sums[fine * c:(fine + 1) * c])
        q_levels.append(q16 * el)
        k_levels.append(k16 * el)
    outs, new_states = [], []
    for h in range(HG_HEADS):
        sl = slice(h * HG_DK, (h + 1) * HG_DK)
        vh = v16[:, sl]
        sc = m_ref[nl] * _dot_nt(q_levels[0][:, sl], k_levels[0][:, sl])
        for li in range(nl):
            sc = sc + m_ref[li] * _dot_nt(q_levels[li + 1][:, sl], k_levels[li + 1][:, sl])
        st = states[h]
        outs.append(_dot(sc.astype(bf16), vh) + _dot_nt(q_inter[:, sl], st.astype(bf16)))
        new_states.append(e_total[:, sl] * st + _dot_tn(vh, k_state[:, sl]))
    return outs, new_states


def _hgrn_kernel(layer, has_s0, has_sout, *refs):
    (qf_ref, vf_ref, ff_ref, qb_ref, vb_ref, fb_ref, lb_ref, af_ref, ab_ref, mf_ref, mb_ref) = refs[:11]
    pos = 11
    s0_ref = None
    if has_s0:
        s0_ref = refs[pos]
        pos += 1
    of_ref, ob_ref = refs[pos], refs[pos + 1]
    pos += 2
    sout_ref = None
    if has_sout:
        sout_ref = refs[pos]
        pos += 1
    st_scr = refs[pos]
    ci = pl.program_id(1)

    n_seq = qf_ref.shape[0]

    @pl.when(ci == 0)
    def _():
        for s in range(n_seq):
            for d in range(2):
                for h in range(HG_HEADS):
                    if has_s0:
                        st_scr[s, d, h] = s0_ref[s, 0, d, h].T
                    else:
                        st_scr[s, d, h] = jnp.zeros((HG_DV, HG_DK), f32)

    raw = lb_ref[...]
    mx = raw[0]
    for i in range(1, DEPTH):
        mx = jnp.maximum(mx, raw[i])
    ex = [jnp.exp(raw[i] - mx) for i in range(DEPTH)]
    den = ex[0]
    for i in range(1, DEPTH):
        den = den + ex[i]
    lb = jnp.zeros_like(den)
    for i in range(1, layer + 1):
        lb = lb + ex[i] / den

    loaded = [[[st_scr[s, d, h] for h in range(HG_HEADS)] for d in range(2)] for s in range(n_seq)]
    results = []
    for s in range(n_seq):
        o_f, st_f = _hgrn_direction(qf_ref[s], vf_ref[s], ff_ref[s], lb[0:1], af_ref, mf_ref, loaded[s][0], False)
        o_b, st_b = _hgrn_direction(qb_ref[s], vb_ref[s], fb_ref[s], lb[1:2], ab_ref, mb_ref, loaded[s][1], True)
        of_ref[s] = jnp.concatenate(o_f, axis=1)
        ob_ref[s] = jnp.concatenate(o_b, axis=1)
        results.append((st_f, st_b))
    for s in range(n_seq):
        for h in range(HG_HEADS):
            st_scr[s, 0, h] = results[s][0][h]
            st_scr[s, 1, h] = results[s][1][h]

    if has_sout:
        @pl.when(ci == pl.num_programs(1) - 1)
        def _():
            for s in range(n_seq):
                for d in range(2):
                    for h in range(HG_HEADS):
                        sout_ref[s, d, h] = st_scr[s, d, h].T


def _hgrn(proj, hg_lb, tables, layer, s0, want_state):
    b_sz, n, _ = proj.shape
    nc = n // HG_CHUNK
    a_f, a_b, m_f, m_b = tables
    c = HG_CHUNK
    sb = HG_SEQS
    fwd = lambda col: pl.BlockSpec((sb, c, HG_QK), lambda b, i: (b, i, col))
    bwd = lambda col: pl.BlockSpec((sb, c, HG_QK), lambda b, i: (b, nc - 1 - i, col))
    const = lambda shape: pl.BlockSpec(shape, lambda b, i: (0,) * len(shape))
    in_specs = [fwd(COL_HQ), fwd(COL_HV), fwd(COL_HFF), bwd(COL_HQ), bwd(COL_HV), bwd(COL_HFB),
                const(hg_lb.shape), const(a_f.shape), const(a_b.shape), const(m_f.shape), const(m_b.shape)]
    args = [proj, proj, proj, proj, proj, proj, hg_lb, a_f, a_b, m_f, m_b]
    state_spec = pl.BlockSpec((sb, 2, HG_HEADS, HG_DK, HG_DV), lambda b, i: (b, 0, 0, 0, 0))
    if s0 is not None:
        in_specs.append(pl.BlockSpec((sb, 1, 2, HG_HEADS, HG_DK, HG_DV), lambda b, i: (b, layer, 0, 0, 0, 0)))
        args.append(s0)
    out_specs = [pl.BlockSpec((sb, c, HG_WIDTH), lambda b, i: (b, i, 0)),
                 pl.BlockSpec((sb, c, HG_WIDTH), lambda b, i: (b, nc - 1 - i, 0))]
    out_shape = [jax.ShapeDtypeStruct((b_sz, n, HG_WIDTH), f32)] * 2
    if want_state:
        out_specs.append(state_spec)
        out_shape.append(jax.ShapeDtypeStruct((b_sz, 2, HG_HEADS, HG_DK, HG_DV), f32))
    return pl.pallas_call(
        functools.partial(_hgrn_kernel, layer, s0 is not None, want_state),
        grid=(b_sz // sb, nc),
        in_specs=in_specs,
        out_specs=out_specs,
        out_shape=out_shape,
        scratch_shapes=[pltpu.VMEM((sb, 2, HG_HEADS, HG_DV, HG_DK), f32)],
        compiler_params=_params(("parallel", "arbitrary")),
        name="hgrn",
    )(*args)


def _rope_rotate(x, cos, sin):
    lane = lax.broadcasted_iota(jnp.int32, x.shape, 1)
    first_half = (lane % (MLA_ROPE // 2)) < (MLA_ROPE // 4)
    quarter = MLA_ROPE // 4
    rot = jnp.where(first_half, -pltpu.roll(x, LANES - quarter, 1), pltpu.roll(x, quarter, 1))
    return x * cos + rot * sin


def _mla_kernel(n_ctx, use_rope, want_cache, tq, *refs):
    qa_ref, kvr_ref, wq_ref, wkt_ref, wv_ref, qn_ref, kvn_ref = refs[:7]
    pos = 7
    if n_ctx:
        cckv_ref, ckr_ref = refs[pos], refs[pos + 1]
        pos += 2
    if use_rope:
        cosq_ref, sinq_ref, cosk_ref, sink_ref = refs[pos:pos + 4]
        pos += 4
    y_ref = refs[pos]
    pos += 1
    if want_cache:
        ockv_ref, okr_ref = refs[pos], refs[pos + 1]
        pos += 2
    kcat_scr, v_scr, o_scr = refs[pos:pos + 3]
    n_seq, n = kvr_ref.shape[0], kvr_ref.shape[1]
    sk = n_ctx + n
    total = n_seq * n
    rows = 512 if total % 512 == 0 else total

    def pieces(r0, nrows):
        out, r = [], r0
        while r < r0 + nrows:
            s, off = divmod(r, n)
            cnt = min(n - off, r0 + nrows - r)
            out.append((s, off, cnt))
            r += cnt
        return out

    def expand(r0, nrows, c_kv, krp):
        k_t = _dot_nt(wkt_ref[0], c_kv)
        eye = (lax.broadcasted_iota(jnp.int32, (LANES, LANES), 0)
               == lax.broadcasted_iota(jnp.int32, (LANES, LANES), 1)).astype(bf16)
        krp_t = _dot_nt(eye, krp).astype(bf16)
        vals = _dot(c_kv, wv_ref[0])
        ones = jnp.ones((nrows, MLA_V), bf16)
        for h in range(MLA_HEADS):
            kcat_scr[h, 0:MLA_NOPE, r0:r0 + nrows] = k_t[h * MLA_NOPE:(h + 1) * MLA_NOPE].astype(bf16)
            kcat_scr[h, MLA_NOPE:QHEAD_PAD, r0:r0 + nrows] = krp_t
            v_scr[h, r0:r0 + nrows, 0:MLA_V] = vals[:, h * MLA_V:(h + 1) * MLA_V].astype(bf16)
            v_scr[h, r0:r0 + nrows, MLA_V:2 * MLA_V] = ones

    @pl.when(pl.program_id(1) == 0)
    def _():
        if n_ctx:
            zeros = jnp.zeros((n_ctx, LANES - MLA_ROPE), f32)
            krp_ctx = jnp.concatenate([ckr_ref[0, 0], zeros], axis=1).astype(bf16)
            expand(0, n_ctx, cckv_ref[0, 0].astype(bf16), krp_ctx)
        for r0 in range(0, total, rows):
            parts = pieces(r0, rows)
            blk = jnp.concatenate([kvr_ref[s, off:off + cnt] for s, off, cnt in parts], axis=0)
            c_kv = _rms(blk[:, :KV_LORA]) * kvn_ref[...]
            krp = blk[:, KV_LORA:]
            if want_cache:
                at = 0
                for s, off, cnt in parts:
                    ockv_ref[s, off:off + cnt] = c_kv[at:at + cnt]
                    okr_ref[s, off:off + cnt] = krp[at:at + cnt, :MLA_ROPE]
                    at += cnt
            if use_rope:
                krp = _rope_rotate(krp, cosk_ref[r0:r0 + rows], sink_ref[r0:r0 + rows])
            expand(n_ctx + r0, rows, c_kv.astype(bf16), krp.astype(bf16))

    scale = float(MLA_NOPE + MLA_ROPE) ** -0.5
    qa = jnp.concatenate([qa_ref[s] for s in range(n_seq)], axis=0)
    qn = (_rms(qa) * qn_ref[...]).astype(bf16)
    n_kc = 1 if sk <= MLA_KEY_CHUNK else sk // (2 * MXU_DIM)
    kc = sk // n_kc

    def head(h):
        q = _dot(qn, wq_ref[0, h]) * scale
        q_rope = q[:, MLA_NOPE:]
        if use_rope:
            q_rope = _rope_rotate(q_rope, cosq_ref[...], sinq_ref[...])
        q_cat = jnp.concatenate([q[:, :MLA_NOPE], q_rope], axis=1).astype(bf16)
        outs = []
        for s in range(n_seq):
            q_s = q_cat[s * tq:(s + 1) * tq]
            m_run = acc = None
            for c in range(n_kc):
                k0 = s * sk + c * kc
                sc = _dot(q_s, kcat_scr[h, :, k0:k0 + kc])
                m_c = jnp.max(sc, axis=-1, keepdims=True)
                m_new = m_c if c == 0 else jnp.maximum(m_run, m_c)
                p = jnp.exp((sc - m_new).astype(bf16))
                pv = _dot(p, v_scr[h, k0:k0 + kc])
                acc = pv if c == 0 else jnp.exp(m_run - m_new) * acc + pv
                m_run = m_new
            outs.append((acc[:, :MLA_V] / acc[:, MLA_V:]).astype(bf16))
        return outs

    if n_kc == 1:
        for h in range(MLA_HEADS):
            for s, o in enumerate(head(h)):
                y_ref[s, :, h * MLA_V:(h + 1) * MLA_V] = o
    else:
        def body(h, carry):
            o_scr[h] = head(h)[0]
            return carry

        lax.fori_loop(0, MLA_HEADS, body, 0, unroll=4)
        for h in range(MLA_HEADS):
            y_ref[0, :, h * MLA_V:(h + 1) * MLA_V] = o_scr[h]


def _mla(proj, wq, wkt, wv, q_norm, kv_norm, ctx, layer, rope, want_cache, tq):
    b_sz, n, _ = proj.shape
    n_ctx = 0 if ctx is None else ctx[0].shape[2]
    sk = n_ctx + n
    sb = MLA_SEQS if (ctx is None and rope is None and n == tq) else 1
    single = pl.Buffered(1)
    const = lambda shape: pl.BlockSpec(shape, lambda b, i: (0,) * len(shape), pipeline_mode=single)
    in_specs = [
        pl.BlockSpec((sb, tq, Q_LORA), lambda b, i: (b, i, COL_QA)),
        pl.BlockSpec((sb, n, KVR_WIDTH), lambda b, i: (b, 0, COL_KVR), pipeline_mode=single),
        pl.BlockSpec((1,) + wq.shape[1:], lambda b, i: (layer, 0, 0, 0), pipeline_mode=single),
        pl.BlockSpec((1,) + wkt.shape[1:], lambda b, i: (layer, 0, 0), pipeline_mode=single),
        pl.BlockSpec((1,) + wv.shape[1:], lambda b, i: (layer, 0, 0), pipeline_mode=single),
        const(q_norm.shape), const(kv_norm.shape),
    ]
    args = [proj, proj, wq, wkt, wv, q_norm, kv_norm]
    if ctx is not None:
        in_specs += [
            pl.BlockSpec((1, 1, n_ctx, KV_LORA), lambda b, i: (b, layer, 0, 0), pipeline_mode=single),
            pl.BlockSpec((1, 1, n_ctx, MLA_ROPE), lambda b, i: (b, layer, 0, 0), pipeline_mode=single)]
        args += [ctx[0], ctx[1]]
    if rope is not None:
        cos, sin = rope
        in_specs += [pl.BlockSpec((tq, LANES), lambda b, i: (i, 0)),
                     pl.BlockSpec((tq, LANES), lambda b, i: (i, 0)),
                     const(cos.shape), const(sin.shape)]
        args += [cos, sin, cos, sin]
    out_specs = [pl.BlockSpec((sb, tq, MLA_WIDTH), lambda b, i: (b, i, 0))]
    out_shape = [jax.ShapeDtypeStruct((b_sz, n, MLA_WIDTH), bf16)]
    if want_cache:
        out_specs += [pl.BlockSpec((sb, n, KV_LORA), lambda b, i: (b, 0, 0)),
                      pl.BlockSpec((sb, n, MLA_ROPE), lambda b, i: (b, 0, 0))]
        out_shape += [jax.ShapeDtypeStruct((b_sz, n, KV_LORA), f32),
                      jax.ShapeDtypeStruct((b_sz, n, MLA_ROPE), f32)]
    return pl.pallas_call(
        functools.partial(_mla_kernel, n_ctx, rope is not None, want_cache, tq),
        grid=(b_sz // sb, n // tq),
        in_specs=in_specs,
        out_specs=out_specs,
        out_shape=out_shape,
        scratch_shapes=[pltpu.VMEM((MLA_HEADS, QHEAD_PAD, sb * sk), bf16),
                        pltpu.VMEM((MLA_HEADS, sb * sk, 2 * MLA_V), bf16),
                        pltpu.VMEM((MLA_HEADS, tq, MLA_V), bf16)],
        compiler_params=_params(("parallel", "arbitrary")),
        name="mla",
    )(*args)


def _rope_tables(n):
    t = np.arange(n)
    r = (t // GRID_W).astype(np.float32)
    col = (t % GRID_W).astype(np.float32)
    nf = MLA_ROPE // 4
    inv = np.float32(ROPE_BASE) ** (-np.arange(nf, dtype=np.float32) / np.float32(nf))
    ar = r[:, None] * inv
    ac = col[:, None] * inv
    ang = np.concatenate([ar, ar, ac, ac], axis=-1).astype(np.float32)
    pad = LANES - MLA_ROPE
    cos = np.concatenate([np.cos(ang), np.ones((n, pad), np.float32)], axis=-1)
    sin = np.concatenate([np.sin(ang), np.zeros((n, pad), np.float32)], axis=-1)
    return jnp.asarray(cos, dtype=f32), jnp.asarray(sin, dtype=f32)


def _outproj_kernel(x_ref, yfn_ref, of_ref, ob_ref, gate_ref, ymla_ref, w_ref, hgg_ref, gpost_ref,
                    mod_ref, o_ref, inv_scr):
    o_hg = of_ref[0] + ob_ref[0]
    gate = gate_ref[0]
    gate = gate * jax.nn.sigmoid(gate)
    parts = [yfn_ref[0]]
    for h in range(HG_HEADS):
        sl = slice(h * HG_DV, (h + 1) * HG_DV)
        parts.append((_rms(o_hg[:, sl]) * hgg_ref[:, sl] * gate[:, sl]).astype(bf16))
    parts.append(ymla_ref[0])
    o_ref[0] = _dot(jnp.concatenate(parts, axis=1), w_ref[0])
    y = o_ref[0]
    inv_scr[...] = lax.rsqrt(jnp.mean(y * y, axis=-1, keepdims=True) + EPS)
    gain = mod_ref[0, 2:3, :] * gpost_ref[...]
    o_ref[0] = x_ref[0] + o_ref[0] * inv_scr[...] * gain


def _outproj(x, y_fn, o_f, o_b, proj, y_mla, w_out, hg_gain, g_post, mod, layer, tm):
    b_sz, n, _ = x.shape
    tok = lambda width, col=0: pl.BlockSpec((1, tm, width), lambda b, i: (b, i, col))
    return pl.pallas_call(
        _outproj_kernel,
        grid=(b_sz, n // tm),
        in_specs=[
            tok(D_MODEL), tok(FN_WIDTH), tok(HG_WIDTH), tok(HG_WIDTH), tok(HG_WIDTH, COL_HGATE),
            tok(MLA_WIDTH),
            pl.BlockSpec((1, D_MODEL, D_MODEL), lambda b, i: (layer, 0, 0)),
            pl.BlockSpec((1, HG_WIDTH), lambda b, i: (0, 0)),
            pl.BlockSpec((1, D_MODEL), lambda b, i: (0, 0)),
            pl.BlockSpec((1, 6, D_MODEL), lambda b, i: (b, 0, 0)),
        ],
        out_specs=tok(D_MODEL),
        out_shape=jax.ShapeDtypeStruct((b_sz, n, D_MODEL), f32),
        scratch_shapes=[pltpu.VMEM((tm, 1), f32)],
        compiler_params=_params(("parallel", "parallel")),
        name="outproj",
    )(x, y_fn, o_f, o_b, proj, y_mla, w_out, hg_gain, g_post, mod)


def _ffn_kernel(x_ref, mod_ref, gpre_ref, gpost_ref, w1_ref, w2_ref, o_ref, h_scr, inv_scr):
    k = pl.program_id(2)

    def hidden_slice():
        a = jnp.maximum(_dot(h_scr[...], w1_ref[0]), 0.0)
        return _dot((a * a).astype(bf16), w2_ref[0])

    @pl.when(k == 0)
    def _():
        _modulated_norm(x_ref, gpre_ref, mod_ref[0, 4:5, :], mod_ref[0, 3:4, :], h_scr, inv_scr)
        o_ref[0] = hidden_slice()

    last = pl.num_programs(2) - 1

    @pl.when((k > 0) & (k < last))
    def _():
        o_ref[0] += hidden_slice()

    @pl.when(k == last)
    def _():
        o_ref[0] += hidden_slice()
        f = o_ref[0]
        inv_scr[...] = lax.rsqrt(jnp.mean(f * f, axis=-1, keepdims=True) + EPS)
        gain = mod_ref[0, 5:6, :] * gpost_ref[...]
        o_ref[0] = x_ref[0] + o_ref[0] * inv_scr[...] * gain


def _ffn(x, mod, g_pre, g_post, w1, w2, layer, tm, tf):
    b_sz, n, _ = x.shape
    return pl.pallas_call(
        _ffn_kernel,
        grid=(b_sz, n // tm, D_FF // tf),
        in_specs=[
            pl.BlockSpec((1, tm, D_MODEL), lambda b, i, k: (b, i, 0)),
            pl.BlockSpec((1, 6, D_MODEL), lambda b, i, k: (b, 0, 0)),
            pl.BlockSpec((1, D_MODEL), lambda b, i, k: (0, 0)),
            pl.BlockSpec((1, D_MODEL), lambda b, i, k: (0, 0)),
            pl.BlockSpec((1, D_MODEL, tf), lambda b, i, k: (layer, 0, k)),
            pl.BlockSpec((1, tf, D_MODEL), lambda b, i, k: (layer, k, 0)),
        ],
        out_specs=pl.BlockSpec((1, tm, D_MODEL), lambda b, i, k: (b, i, 0)),
        out_shape=jax.ShapeDtypeStruct((b_sz, n, D_MODEL), f32),
        scratch_shapes=[pltpu.VMEM((tm, D_MODEL), bf16), pltpu.VMEM((tm, 1), f32)],
        compiler_params=_params(("parallel", "parallel", "arbitrary"), VMEM_LIMIT_WIDE),
        name="ffn",
    )(x, mod, g_pre, g_post, w1, w2)


def _layer(x, mod, wl, hg_tables, dft, layer, n_seq, ctx, s0, rope, is_context, tq):
    b_sz, n, _ = x.shape
    n_seqs = b_sz * n // n_seq
    proj = _inproj(x, mod, wl["g_pre_mix"], wl["w_in"], layer, INPROJ_TM).reshape(n_seqs, n_seq, IN_COLS_PAD)
    y_fn = _fourier(proj, dft[0], dft[1], dft[2], min(n_seq, 512))
    hg = _hgrn(proj, wl["hg_lb"], hg_tables, layer, s0, is_context)
    mla = _mla(proj, wl["wq"], wl["wkt"], wl["wv"], wl["q_norm"], wl["kv_norm"], ctx, layer, rope, is_context,
               tq)
    flat = lambda a: a.reshape(b_sz, n, a.shape[-1])
    x = _outproj(x, flat(y_fn), flat(hg[0]), flat(hg[1]), flat(proj), flat(mla[0]), wl["w_out"],
                 wl["hg_gain"], wl["g_post_mix"], mod, layer, OUTPROJ_TM)
    x = _ffn(x, mod, wl["g_pre_ff"], wl["g_post_ff"], wl["w_ff1"], wl["w_ff2"], layer, FFN_TM, FFN_TF)
    extras = (mla[1], mla[2], hg[2]) if is_context else None
    return x, extras


def kernel(x_prompt, x_sample, c, cache_ckv, cache_krope, state_hgrn, c_ctx, w_ada, b_ada, g_pre_mix,
           g_post_mix, g_pre_ff, g_post_ff, w_in, hg_lb, hg_gain, mla_q_norm, mla_kv_norm, w_q_b, w_kv_b,
           w_out, w_ff1, w_ff2):
    batch, seq, _ = x_prompt.shape
    dec_batch, dec_seq, _ = x_sample.shape

    rows = 16
    cond = jnp.concatenate([c_ctx[None, :], c, jnp.zeros((rows - 1 - dec_batch, D_MODEL), f32)], axis=0)
    mod = _ada(cond, w_ada, b_ada).reshape(DEPTH, rows, 6, D_MODEL)

    hg_tables = _hgrn_tables()
    ccs = _channel_tables()
    dft_p = (ccs,) + _dft_tables(seq)
    dft_s = (ccs,) + _dft_tables(dec_seq)
    rope = _rope_tables(dec_seq)

    yp = x_prompt.reshape(1, batch * seq, D_MODEL)
    ys = x_sample
    ckv_list, kr_list, st_list = [], [], []
    wq = w_q_b.reshape(DEPTH, Q_LORA, MLA_HEADS, MLA_NOPE + MLA_ROPE).astype(bf16)
    wq = jnp.pad(wq, ((0, 0), (0, 0), (0, 0), (0, QHEAD_PAD - MLA_NOPE - MLA_ROPE)))
    wkv = w_kv_b.astype(bf16).reshape(DEPTH, KV_LORA, MLA_HEADS, MLA_NOPE + MLA_V)
    stacked = {
        "w_in": jnp.pad(w_in.astype(bf16), ((0, 0), (0, 0), (0, IN_COLS_PAD - IN_COLS))),
        "wq": wq.transpose(0, 2, 1, 3),
        "wkt": wkv[:, :, :, :MLA_NOPE].reshape(DEPTH, KV_LORA, MLA_HEADS * MLA_NOPE).transpose(0, 2, 1),
        "wv": wkv[:, :, :, MLA_NOPE:].reshape(DEPTH, KV_LORA, MLA_HEADS * MLA_V),
        "w_out": w_out.astype(bf16),
        "w_ff1": w_ff1.astype(bf16),
        "w_ff2": w_ff2.astype(bf16),
    }
    for l in range(DEPTH):
        wl = {
            **stacked,
            "g_pre_mix": g_pre_mix[l][None, :], "g_post_mix": g_post_mix[l][None, :],
            "g_pre_ff": g_pre_ff[l][None, :], "g_post_ff": g_post_ff[l][None, :],
            "hg_lb": hg_lb, "hg_gain": hg_gain[l][None, :],
            "q_norm": mla_q_norm[l][None, :], "kv_norm": mla_kv_norm[l][None, :],
        }
        yp, (ckv_l, kr_l, st_l) = _layer(yp, mod[l, 0:1], wl, hg_tables, dft_p, l, seq, None, None, None,
                                         True, seq)
        ckv_list.append(ckv_l)
        kr_list.append(kr_l)
        st_list.append(st_l)
        ys, _ = _layer(ys, mod[l, 1:1 + dec_batch], wl, hg_tables, dft_s, l, dec_seq,
                       (cache_ckv, cache_krope), state_hgrn, rope, False, 1024)

    return (yp.reshape(batch, seq, D_MODEL), ys, jnp.stack(ckv_list, axis=1), jnp.stack(kr_list, axis=1),
            jnp.stack(st_list, axis=1))
```

```python
import functools

import numpy as np
import jax
import jax.numpy as jnp
from jax import lax
from jax.experimental import pallas as pl
from jax.experimental.pallas import tpu as pltpu

f32 = jnp.float32
bf16 = jnp.bfloat16

D_MODEL = 2048
DEPTH = 2
GRID_W = 64
FN_HEADS = 4
FN_DH = 128
FN_WIDTH = FN_HEADS * FN_DH
HG_HEADS = 4
HG_DK = 128
HG_DV = 128
HG_QK = HG_HEADS * HG_DK
HG_WIDTH = HG_HEADS * HG_DV
MLA_HEADS = 8
MLA_NOPE = 128
MLA_ROPE = 64
MLA_V = 128
Q_LORA = 768
KV_LORA = 512
MLA_WIDTH = MLA_HEADS * MLA_V
D_FF = 4 * D_MODEL
ROPE_BASE = 10000.0
EPS = 1e-6
F_FLOOR = 1e-30

LANES = 128
SUBLANES = 8
MXU_DIM = 256
IN_COLS = 4416
IN_COLS_PAD = 4608
IN_TILE = 2304
COL_HQ, COL_HV, COL_HFF, COL_HFB, COL_HGATE = 1, 2, 3, 4, 5
COL_QA = 4
KVR_WIDTH = 640
COL_KVR = 6
QHEAD_PAD = 256
HG_CHUNK = 128
HG_LEVELS = (64, 32, 16, 8, 4, 2, 1)
HG_SPLIT = 2
HG_SEQS = 4
INPROJ_TM = 1024
OUTPROJ_TM = 512
MLA_KEY_CHUNK = 1280
MLA_SEQS = 4
FFN_TM = 1024
FFN_TF = 1024
VMEM_LIMIT = 56 * 1024 * 1024
VMEM_LIMIT_WIDE = 62 * 1024 * 1024


def _params(semantics, vmem_limit=VMEM_LIMIT):
    return pltpu.CompilerParams(dimension_semantics=semantics, vmem_limit_bytes=vmem_limit)


def _dot(a, b):
    return jnp.dot(a, b, preferred_element_type=f32)


def _dot_nt(a, b):
    return lax.dot_general(a, b, (((1,), (1,)), ((), ())), preferred_element_type=f32)


def _dot_tn(a, b):
    return lax.dot_general(a, b, (((0,), (0,)), ((), ())), preferred_element_type=f32)


def _rms(x):
    return x * lax.rsqrt(jnp.mean(x * x, axis=-1, keepdims=True) + EPS)


def _modulated_norm(x_ref, g_ref, scale_row, shift_row, h_scr, inv_scr):
    x = x_ref[0]
    inv_scr[...] = lax.rsqrt(jnp.mean(x * x, axis=-1, keepdims=True) + EPS)
    gain = g_ref[...] * (1.0 + scale_row)
    h_scr[...] = (x_ref[0] * inv_scr[...] * gain + shift_row).astype(bf16)


def _ada_kernel(cond_ref, w_ref, b_ref, o_ref):
    cnd = cond_ref[...]
    act = cnd * jax.nn.sigmoid(cnd)
    o_ref[0] = _dot(act.astype(bf16), w_ref[0].astype(bf16)) + b_ref[0]


def _ada(cond, w_ada, b_ada):
    rows = cond.shape[0]
    tn = 1024
    return pl.pallas_call(
        _ada_kernel,
        grid=(DEPTH, 6 * D_MODEL // tn),
        in_specs=[
            pl.BlockSpec((rows, D_MODEL), lambda l, j: (0, 0)),
            pl.BlockSpec((1, D_MODEL, tn), lambda l, j: (l, 0, j)),
            pl.BlockSpec((1, 1, tn), lambda l, j: (l, 0, j)),
        ],
        out_specs=pl.BlockSpec((1, rows, tn), lambda l, j: (l, 0, j)),
        out_shape=jax.ShapeDtypeStruct((DEPTH, rows, 6 * D_MODEL), f32),
        compiler_params=_params(("parallel", "parallel")),
        name="ada_mod",
    )(cond, w_ada, b_ada.reshape(DEPTH, 1, 6 * D_MODEL))


def _inproj_kernel(x_ref, mod_ref, g_ref, w_ref, o_ref, h_scr, inv_scr):
    @pl.when(pl.program_id(2) == 0)
    def _():
        _modulated_norm(x_ref, g_ref, mod_ref[0, 1:2, :], mod_ref[0, 0:1, :], h_scr, inv_scr)
        o_ref[0] = _dot(h_scr[...], w_ref[0])

    @pl.when(pl.program_id(2) > 0)
    def _():
        o_ref[0] = _dot(h_scr[...], w_ref[0])


def _inproj(x, mod, g, w_in, layer, tm):
    b_sz, n, _ = x.shape
    return pl.pallas_call(
        _inproj_kernel,
        grid=(b_sz, n // tm, IN_COLS_PAD // IN_TILE),
        in_specs=[
            pl.BlockSpec((1, tm, D_MODEL), lambda b, i, j: (b, i, 0)),
            pl.BlockSpec((1, 6, D_MODEL), lambda b, i, j: (b, 0, 0)),
            pl.BlockSpec((1, D_MODEL), lambda b, i, j: (0, 0)),
            pl.BlockSpec((1, D_MODEL, IN_TILE), lambda b, i, j: (layer, 0, j)),
        ],
        out_specs=pl.BlockSpec((1, tm, IN_TILE), lambda b, i, j: (b, i, j)),
        out_shape=jax.ShapeDtypeStruct((b_sz, n, IN_COLS_PAD), f32),
        scratch_shapes=[pltpu.VMEM((tm, D_MODEL), bf16), pltpu.VMEM((tm, 1), f32)],
        compiler_params=_params(("parallel", "parallel", "arbitrary"), VMEM_LIMIT_WIDE),
        name="inproj",
    )(x, mod, g, w_in)


def _fourier_kernel(u_ref, ccs_ref, cn_ref, sn_ref, o_ref, ucs_scr):
    @pl.when(pl.program_id(1) == 0)
    def _():
        for h in range(FN_HEADS):
            sl = slice(h * FN_DH, (h + 1) * FN_DH)
            r = _dot(u_ref[0, :, sl].astype(bf16), ccs_ref[...]).astype(bf16)
            ucs_scr[:, sl] = r[:, :FN_DH]
            ucs_scr[:, FN_WIDTH + h * FN_DH:FN_WIDTH + (h + 1) * FN_DH] = r[:, FN_DH:]

    y = _dot(cn_ref[...], ucs_scr[:, :FN_WIDTH]) + _dot(sn_ref[...], ucs_scr[:, FN_WIDTH:])
    o_ref[0] = y.astype(bf16)


def _fourier(proj, ccs, cn, sn_neg, tr):
    b_sz, n, _ = proj.shape
    return pl.pallas_call(
        _fourier_kernel,
        grid=(b_sz, n // tr),
        in_specs=[
            pl.BlockSpec((1, n, FN_WIDTH), lambda b, i: (b, 0, 0)),
            pl.BlockSpec((FN_DH, 2 * FN_DH), lambda b, i: (0, 0)),
            pl.BlockSpec((tr, n), lambda b, i: (i, 0)),
            pl.BlockSpec((tr, n), lambda b, i: (i, 0)),
        ],
        out_specs=pl.BlockSpec((1, tr, FN_WIDTH), lambda b, i: (b, i, 0)),
        out_shape=jax.ShapeDtypeStruct((b_sz, n, FN_WIDTH), bf16),
        scratch_shapes=[pltpu.VMEM((n, 2 * FN_WIDTH), bf16)],
        compiler_params=_params(("parallel", "arbitrary")),
        name="fourier",
    )(proj, ccs, cn, sn_neg)


def _dft_tables(n):
    j = np.arange(n, dtype=np.int64)
    ang = ((j[:, None] * j[None, :]) % n).astype(np.float64) * (2.0 * np.pi / n)
    scale_n = 1.0 / np.sqrt(n)
    cn = jnp.asarray(np.cos(ang) * scale_n, dtype=bf16)
    sn_neg = jnp.asarray(-np.sin(ang) * scale_n, dtype=bf16)
    return cn, sn_neg


def _channel_tables():
    k = np.arange(FN_DH)
    ang = 2.0 * np.pi * ((k[:, None] * k[None, :]) % FN_DH) / FN_DH
    table = np.concatenate([np.cos(ang), np.sin(ang)], axis=1) / np.sqrt(FN_DH)
    return jnp.asarray(table, dtype=bf16)


def _hgrn_tables():
    c = HG_CHUNK
    idx = np.arange(c)
    blocks = [(idx[None, :] <= idx[:, None])]
    masks = []
    for h in HG_LEVELS:
        mid = (idx // (2 * h)) * (2 * h) + h
        upper = idx >= mid
        if h < SUBLANES:
            row_up = (idx[None, :] >= mid[:, None]) & (idx[None, :] <= idx[:, None])
            row_lo = (idx[None, :] > idx[:, None]) & (idx[None, :] < mid[:, None])
            blocks.append(np.where(upper[:, None], row_up, row_lo))
        same = (idx[:, None] // (2 * h)) == (idx[None, :] // (2 * h))
        masks.append(same & upper[:, None] & ~upper[None, :])
    masks.append(np.eye(c, dtype=bool))
    a_f = np.concatenate([b.astype(np.float32) for b in blocks], axis=0)
    a_b = np.concatenate([b[::-1, ::-1].astype(np.float32) for b in blocks], axis=0)
    m_f = np.stack([m.astype(np.float32) for m in masks])
    m_b = np.stack([m[::-1, ::-1].astype(np.float32) for m in masks])
    rep = lambda a: jnp.asarray(np.concatenate([a] * HG_SPLIT, axis=1), dtype=bf16)
    return rep(a_f), rep(a_b), jnp.asarray(m_f), jnp.asarray(m_b)


def _block_row(x, h, row):
    c, w = x.shape
    xb = x.reshape(c // (2 * h), 2 * h, w)
    return jnp.broadcast_to(xb[:, row:row + 1, :], xb.shape).reshape(c, w)


def _hgrn_direction(q, v, pre, lb, a_ref, m_ref, states, backward):
    c = HG_CHUNK
    nl = len(HG_LEVELS)
    f = lb + (1.0 - lb) * jax.nn.sigmoid(pre)
    lf = jnp.log(jnp.maximum(f, F_FLOOR))
    kk = 1.0 - f
    pieces = []
    rem = lf
    for _ in range(HG_SPLIT):
        piece = rem.astype(bf16)
        pieces.append(piece)
        rem = rem - piece.astype(f32)
    sums = _dot(a_ref[...], jnp.concatenate(pieces, axis=0))
    cum = sums[0:c]
    total = cum[0:1] if backward else cum[c - 1:c]
    q16 = q.astype(bf16)
    k16 = kk.astype(bf16)
    v16 = v.astype(bf16)
    decay16 = lambda log_decay: jnp.exp(log_decay.astype(bf16))
    q_inter = q16 * decay16(cum)
    k_state = k16 * decay16(jnp.minimum(total - cum, 0.0))
    e_total = jnp.exp(total)
    q_levels, k_levels = [q16], [k16]
    fine = 0
    for h in HG_LEVELS:
        if h >= SUBLANES:
            ref = _block_row(cum, h, h if backward else h - 1)
            el = decay16(-jnp.abs(cum - ref))
        else:
            fine += 1
            el = decay16(sums[fine * c:(fine + 1) * c])
        q_levels.append(q16 * el)
        k_levels.append(k16 * el)
    outs, new_states = [], []
    for h in range(HG_HEADS):
        sl = slice(h * HG_DK, (h + 1) * HG_DK)
        vh = v16[:, sl]
        sc = m_ref[nl] * _dot_nt(q_levels[0][:, sl], k_levels[0][:, sl])
        for li in range(nl):
            sc = sc + m_ref[li] * _dot_nt(q_levels[li + 1][:, sl], k_levels[li + 1][:, sl])
        st = states[h]
        outs.append(_dot(sc.astype(bf16), vh) + _dot_nt(q_inter[:, sl], st.astype(bf16)))
        new_states.append(e_total[:, sl] * st + _dot_tn(vh, k_state[:, sl]))
    return outs, new_states


def _hgrn_kernel(layer, has_s0, has_sout, *refs):
    (qf_ref, vf_ref, ff_ref, qb_ref, vb_ref, fb_ref, lb_ref, af_ref, ab_ref, mf_ref, mb_ref) = refs[:11]
    pos = 11
    s0_ref = None
    if has_s0:
        s0_ref = refs[pos]
        pos += 1
    of_ref, ob_ref = refs[pos], refs[pos + 1]
    pos += 2
    sout_ref = None
    if has_sout:
        sout_ref = refs[pos]
        pos += 1
    st_scr = refs[pos]
    ci = pl.program_id(1)

    n_seq = qf_ref.shape[0]

    @pl.when(ci == 0)
    def _():
        for s in range(n_seq):
            for d in range(2):
                for h in range(HG_HEADS):
                    if has_s0:
                        st_scr[s, d, h] = s0_ref[s, 0, d, h].T
                    else:
                        st_scr[s, d, h] = jnp.zeros((HG_DV, HG_DK), f32)

    raw = lb_ref[...]
    mx = raw[0]
    for i in range(1, DEPTH):
        mx = jnp.maximum(mx, raw[i])
    ex = [jnp.exp(raw[i] - mx) for i in range(DEPTH)]
    den = ex[0]
    for i in range(1, DEPTH):
        den = den + ex[i]
    lb = jnp.zeros_like(den)
    for i in range(1, layer + 1):
        lb = lb + ex[i] / den

    loaded = [[[st_scr[s, d, h] for h in range(HG_HEADS)] for d in range(2)] for s in range(n_seq)]
    results = []
    for s in range(n_seq):
        o_f, st_f = _hgrn_direction(qf_ref[s], vf_ref[s], ff_ref[s], lb[0:1], af_ref, mf_ref, loaded[s][0], False)
        o_b, st_b = _hgrn_direction(qb_ref[s], vb_ref[s], fb_ref[s], lb[1:2], ab_ref, mb_ref, loaded[s][1], True)
        of_ref[s] = jnp.concatenate(o_f, axis=1)
        ob_ref[s] = jnp.concatenate(o_b, axis=1)
        results.append((st_f, st_b))
    for s in range(n_seq):
        for h in range(HG_HEADS):
            st_scr[s, 0, h] = results[s][0][h]
            st_scr[s, 1, h] = results[s][1][h]

    if has_sout:
        @pl.when(ci == pl.num_programs(1) - 1)
        def _():
            for s in range(n_seq):
                for d in range(2):
                    for h in range(HG_HEADS):
                        sout_ref[s, d, h] = st_scr[s, d, h].T


def _hgrn(proj, hg_lb, tables, layer, s0, want_state):
    b_sz, n, _ = proj.shape
    nc = n // HG_CHUNK
    a_f, a_b, m_f, m_b = tables
    c = HG_CHUNK
    sb = HG_SEQS
    fwd = lambda col: pl.BlockSpec((sb, c, HG_QK), lambda b, i: (b, i, col))
    bwd = lambda col: pl.BlockSpec((sb, c, HG_QK), lambda b, i: (b, nc - 1 - i, col))
    const = lambda shape: pl.BlockSpec(shape, lambda b, i: (0,) * len(shape))
    in_specs = [fwd(COL_HQ), fwd(COL_HV), fwd(COL_HFF), bwd(COL_HQ), bwd(COL_HV), bwd(COL_HFB),
                const(hg_lb.shape), const(a_f.shape), const(a_b.shape), const(m_f.shape), const(m_b.shape)]
    args = [proj, proj, proj, proj, proj, proj, hg_lb, a_f, a_b, m_f, m_b]
    state_spec = pl.BlockSpec((sb, 2, HG_HEADS, HG_DK, HG_DV), lambda b, i: (b, 0, 0, 0, 0))
    if s0 is not None:
        in_specs.append(pl.BlockSpec((sb, 1, 2, HG_HEADS, HG_DK, HG_DV), lambda b, i: (b, layer, 0, 0, 0, 0)))
        args.append(s0)
    out_specs = [pl.BlockSpec((sb, c, HG_WIDTH), lambda b, i: (b, i, 0)),
                 pl.BlockSpec((sb, c, HG_WIDTH), lambda b, i: (b, nc - 1 - i, 0))]
    out_shape = [jax.ShapeDtypeStruct((b_sz, n, HG_WIDTH), f32)] * 2
    if want_state:
        out_specs.append(state_spec)
        out_shape.append(jax.ShapeDtypeStruct((b_sz, 2, HG_HEADS, HG_DK, HG_DV), f32))
    return pl.pallas_call(
        functools.partial(_hgrn_kernel, layer, s0 is not None, want_state),
        grid=(b_sz // sb, nc),
        in_specs=in_specs,
        out_specs=out_specs,
        out_shape=out_shape,
        scratch_shapes=[pltpu.VMEM((sb, 2, HG_HEADS, HG_DV, HG_DK), f32)],
        compiler_params=_params(("parallel", "arbitrary")),
        name="hgrn",
    )(*args)


def _rope_rotate(x, cos, sin):
    lane = lax.broadcasted_iota(jnp.int32, x.shape, 1)
    first_half = (lane % (MLA_ROPE // 2)) < (MLA_ROPE // 4)
    quarter = MLA_ROPE // 4
    rot = jnp.where(first_half, -pltpu.roll(x, LANES - quarter, 1), pltpu.roll(x, quarter, 1))
    return x * cos + rot * sin


def _mla_kernel(n_ctx, use_rope, want_cache, tq, *refs):
    qa_ref, kvr_ref, wq_ref, wkt_ref, wv_ref, qn_ref, kvn_ref = refs[:7]
    pos = 7
    if n_ctx:
        cckv_ref, ckr_ref = refs[pos], refs[pos + 1]
        pos += 2
    if use_rope:
        cosq_ref, sinq_ref, cosk_ref, sink_ref = refs[pos:pos + 4]
        pos += 4
    y_ref = refs[pos]
    pos += 1
    if want_cache:
        ockv_ref, okr_ref = refs[pos], refs[pos + 1]
        pos += 2
    kcat_scr, v_scr, o_scr = refs[pos:pos + 3]
    n_seq, n = kvr_ref.shape[0], kvr_ref.shape[1]
    sk = n_ctx + n
    total = n_seq * n
    rows = 512 if total % 512 == 0 else total

    def pieces(r0, nrows):
        out, r = [], r0
        while r < r0 + nrows:
            s, off = divmod(r, n)
            cnt = min(n - off, r0 + nrows - r)
            out.append((s, off, cnt))
            r += cnt
        return out

    def expand(r0, nrows, c_kv, krp):
        k_t = _dot_nt(wkt_ref[0], c_kv)
        eye = (lax.broadcasted_iota(jnp.int32, (LANES, LANES), 0)
               == lax.broadcasted_iota(jnp.int32, (LANES, LANES), 1)).astype(bf16)
        krp_t = _dot_nt(eye, krp).astype(bf16)
        vals = _dot(c_kv, wv_ref[0])
        ones = jnp.ones((nrows, MLA_V), bf16)
        for h in range(MLA_HEADS):
            kcat_scr[h, 0:MLA_NOPE, r0:r0 + nrows] = k_t[h * MLA_NOPE:(h + 1) * MLA_NOPE].astype(bf16)
            kcat_scr[h, MLA_NOPE:QHEAD_PAD, r0:r0 + nrows] = krp_t
            v_scr[h, r0:r0 + nrows, 0:MLA_V] = vals[:, h * MLA_V:(h + 1) * MLA_V].astype(bf16)
            v_scr[h, r0:r0 + nrows, MLA_V:2 * MLA_V] = ones

    @pl.when(pl.program_id(1) == 0)
    def _():
        if n_ctx:
            zeros = jnp.zeros((n_ctx, LANES - MLA_ROPE), f32)
            krp_ctx = jnp.concatenate([ckr_ref[0, 0], zeros], axis=1).astype(bf16)
            expand(0, n_ctx, cckv_ref[0, 0].astype(bf16), krp_ctx)
        for r0 in range(0, total, rows):
            parts = pieces(r0, rows)
            blk = jnp.concatenate([kvr_ref[s, off:off + cnt] for s, off, cnt in parts], axis=0)
            c_kv = _rms(blk[:, :KV_LORA]) * kvn_ref[...]
            krp = blk[:, KV_LORA:]
            if want_cache:
                at = 0
                for s, off, cnt in parts:
                    ockv_ref[s, off:off + cnt] = c_kv[at:at + cnt]
                    okr_ref[s, off:off + cnt] = krp[at:at + cnt, :MLA_ROPE]
                    at += cnt
            if use_rope:
                krp = _rope_rotate(krp, cosk_ref[r0:r0 + rows], sink_ref[r0:r0 + rows])
            expand(n_ctx + r0, rows, c_kv.astype(bf16), krp.astype(bf16))

    scale = float(MLA_NOPE + MLA_ROPE) ** -0.5
    qa = jnp.concatenate([qa_ref[s] for s in range(n_seq)], axis=0)
    qn = (_rms(qa) * qn_ref[...]).astype(bf16)
    n_kc = 1 if sk <= MLA_KEY_CHUNK else sk // (2 * MXU_DIM)
    kc = sk // n_kc

    def head(h):
        q = _dot(qn, wq_ref[0, h]) * scale
        q_rope = q[:, MLA_NOPE:]
        if use_rope:
            q_rope = _rope_rotate(q_rope, cosq_ref[...], sinq_ref[...])
        q_cat = jnp.concatenate([q[:, :MLA_NOPE], q_rope], axis=1).astype(bf16)
        outs = []
        for s in range(n_seq):
            q_s = q_cat[s * tq:(s + 1) * tq]
            m_run = acc = None
            for c in range(n_kc):
                k0 = s * sk + c * kc
                sc = _dot(q_s, kcat_scr[h, :, k0:k0 + kc])
                m_c = jnp.max(sc, axis=-1, keepdims=True)
                m_new = m_c if c == 0 else jnp.maximum(m_run, m_c)
                p = jnp.exp((sc - m_new).astype(bf16))
                pv = _dot(p, v_scr[h, k0:k0 + kc])
                acc = pv if c == 0 else jnp.exp(m_run - m_new) * acc + pv
                m_run = m_new
            outs.append((acc[:, :MLA_V] / acc[:, MLA_V:]).astype(bf16))
        return outs

    if n_kc == 1:
        for h in range(MLA_HEADS):
            for s, o in enumerate(head(h)):
                y_ref[s, :, h * MLA_V:(h + 1) * MLA_V] = o
    else:
        def body(h, carry):
            o_scr[h] = head(h)[0]
            return carry

        lax.fori_loop(0, MLA_HEADS, body, 0, unroll=4)
        for h in range(MLA_HEADS):
            y_ref[0, :, h * MLA_V:(h + 1) * MLA_V] = o_scr[h]


def _mla(proj, wq, wkt, wv, q_norm, kv_norm, ctx, layer, rope, want_cache, tq):
    b_sz, n, _ = proj.shape
    n_ctx = 0 if ctx is None else ctx[0].shape[2]
    sk = n_ctx + n
    sb = MLA_SEQS if (ctx is None and rope is None and n == tq) else 1
    single = pl.Buffered(1)
    const = lambda shape: pl.BlockSpec(shape, lambda b, i: (0,) * len(shape), pipeline_mode=single)
    in_specs = [
        pl.BlockSpec((sb, tq, Q_LORA), lambda b, i: (b, i, COL_QA)),
        pl.BlockSpec((sb, n, KVR_WIDTH), lambda b, i: (b, 0, COL_KVR), pipeline_mode=single),
        pl.BlockSpec((1,) + wq.shape[1:], lambda b, i: (layer, 0, 0, 0), pipeline_mode=single),
        pl.BlockSpec((1,) + wkt.shape[1:], lambda b, i: (layer, 0, 0), pipeline_mode=single),
        pl.BlockSpec((1,) + wv.shape[1:], lambda b, i: (layer, 0, 0), pipeline_mode=single),
        const(q_norm.shape), const(kv_norm.shape),
    ]
    args = [proj, proj, wq, wkt, wv, q_norm, kv_norm]
    if ctx is not None:
        in_specs += [
            pl.BlockSpec((1, 1, n_ctx, KV_LORA), lambda b, i: (b, layer, 0, 0), pipeline_mode=single),
            pl.BlockSpec((1, 1, n_ctx, MLA_ROPE), lambda b, i: (b, layer, 0, 0), pipeline_mode=single)]
        args += [ctx[0], ctx[1]]
    if rope is not None:
        cos, sin = rope
        in_specs += [pl.BlockSpec((tq, LANES), lambda b, i: (i, 0)),
                     pl.BlockSpec((tq, LANES), lambda b, i: (i, 0)),
                     const(cos.shape), const(sin.shape)]
        args += [cos, sin, cos, sin]
    out_specs = [pl.BlockSpec((sb, tq, MLA_WIDTH), lambda b, i: (b, i, 0))]
    out_shape = [jax.ShapeDtypeStruct((b_sz, n, MLA_WIDTH), bf16)]
    if want_cache:
        out_specs += [pl.BlockSpec((sb, n, KV_LORA), lambda b, i: (b, 0, 0)),
                      pl.BlockSpec((sb, n, MLA_ROPE), lambda b, i: (b, 0, 0))]
        out_shape += [jax.ShapeDtypeStruct((b_sz, n, KV_LORA), f32),
                      jax.ShapeDtypeStruct((b_sz, n, MLA_ROPE), f32)]
    return pl.pallas_call(
        functools.partial(_mla_kernel, n_ctx, rope is not None, want_cache, tq),
        grid=(b_sz // sb, n // tq),
        in_specs=in_specs,
        out_specs=out_specs,
        out_shape=out_shape,
        scratch_shapes=[pltpu.VMEM((MLA_HEADS, QHEAD_PAD, sb * sk), bf16),
                        pltpu.VMEM((MLA_HEADS, sb * sk, 2 * MLA_V), bf16),
                        pltpu.VMEM((MLA_HEADS, tq, MLA_V), bf16)],
        compiler_params=_params(("parallel", "arbitrary")),
        name="mla",
    )(*args)


def _rope_tables(n):
    t = np.arange(n)
    r = (t // GRID_W).astype(np.float32)
    col = (t % GRID_W).astype(np.float32)
    nf = MLA_ROPE // 4
    inv = np.float32(ROPE_BASE) ** (-np.arange(nf, dtype=np.float32) / np.float32(nf))
    ar = r[:, None] * inv
    ac = col[:, None] * inv
    ang = np.concatenate([ar, ar, ac, ac], axis=-1).astype(np.float32)
    pad = LANES - MLA_ROPE
    cos = np.concatenate([np.cos(ang), np.ones((n, pad), np.float32)], axis=-1)
    sin = np.concatenate([np.sin(ang), np.zeros((n, pad), np.float32)], axis=-1)
    return jnp.asarray(cos, dtype=f32), jnp.asarray(sin, dtype=f32)


def _outproj_kernel(x_ref, yfn_ref, of_ref, ob_ref, gate_ref, ymla_ref, w_ref, hgg_ref, gpost_ref,
                    mod_ref, o_ref, inv_scr):
    o_hg = of_ref[0] + ob_ref[0]
    gate = gate_ref[0]
    gate = gate * jax.nn.sigmoid(gate)
    parts = [yfn_ref[0]]
    for h in range(HG_HEADS):
        sl = slice(h * HG_DV, (h + 1) * HG_DV)
        parts.append((_rms(o_hg[:, sl]) * hgg_ref[:, sl] * gate[:, sl]).astype(bf16))
    parts.append(ymla_ref[0])
    o_ref[0] = _dot(jnp.concatenate(parts, axis=1), w_ref[0])
    y = o_ref[0]
    inv_scr[...] = lax.rsqrt(jnp.mean(y * y, axis=-1, keepdims=True) + EPS)
    gain = mod_ref[0, 2:3, :] * gpost_ref[...]
    o_ref[0] = x_ref[0] + o_ref[0] * inv_scr[...] * gain


def _outproj(x, y_fn, o_f, o_b, proj, y_mla, w_out, hg_gain, g_post, mod, layer, tm):
    b_sz, n, _ = x.shape
    tok = lambda width, col=0: pl.BlockSpec((1, tm, width), lambda b, i: (b, i, col))
    return pl.pallas_call(
        _outproj_kernel,
        grid=(b_sz, n // tm),
        in_specs=[
            tok(D_MODEL), tok(FN_WIDTH), tok(HG_WIDTH), tok(HG_WIDTH), tok(HG_WIDTH, COL_HGATE),
            tok(MLA_WIDTH),
            pl.BlockSpec((1, D_MODEL, D_MODEL), lambda b, i: (layer, 0, 0)),
            pl.BlockSpec((1, HG_WIDTH), lambda b, i: (0, 0)),
            pl.BlockSpec((1, D_MODEL), lambda b, i: (0, 0)),
            pl.BlockSpec((1, 6, D_MODEL), lambda b, i: (b, 0, 0)),
        ],
        out_specs=tok(D_MODEL),
        out_shape=jax.ShapeDtypeStruct((b_sz, n, D_MODEL), f32),
        scratch_shapes=[pltpu.VMEM((tm, 1), f32)],
        compiler_params=_params(("parallel", "parallel")),
        name="outproj",
    )(x, y_fn, o_f, o_b, proj, y_mla, w_out, hg_gain, g_post, mod)


def _ffn_kernel(x_ref, mod_ref, gpre_ref, gpost_ref, w1_ref, w2_ref, o_ref, h_scr, inv_scr):
    k = pl.program_id(2)

    def hidden_slice():
        a = jnp.maximum(_dot(h_scr[...], w1_ref[0]), 0.0)
        return _dot((a * a).astype(bf16), w2_ref[0])

    @pl.when(k == 0)
    def _():
        _modulated_norm(x_ref, gpre_ref, mod_ref[0, 4:5, :], mod_ref[0, 3:4, :], h_scr, inv_scr)
        o_ref[0] = hidden_slice()

    last = pl.num_programs(2) - 1

    @pl.when((k > 0) & (k < last))
    def _():
        o_ref[0] += hidden_slice()

    @pl.when(k == last)
    def _():
        o_ref[0] += hidden_slice()
        f = o_ref[0]
        inv_scr[...] = lax.rsqrt(jnp.mean(f * f, axis=-1, keepdims=True) + EPS)
        gain = mod_ref[0, 5:6, :] * gpost_ref[...]
        o_ref[0] = x_ref[0] + o_ref[0] * inv_scr[...] * gain


def _ffn(x, mod, g_pre, g_post, w1, w2, layer, tm, tf):
    b_sz, n, _ = x.shape
    return pl.pallas_call(
        _ffn_kernel,
        grid=(b_sz, n // tm, D_FF // tf),
        in_specs=[
            pl.BlockSpec((1, tm, D_MODEL), lambda b, i, k: (b, i, 0)),
            pl.BlockSpec((1, 6, D_MODEL), lambda b, i, k: (b, 0, 0)),
            pl.BlockSpec((1, D_MODEL), lambda b, i, k: (0, 0)),
            pl.BlockSpec((1, D_MODEL), lambda b, i, k: (0, 0)),
            pl.BlockSpec((1, D_MODEL, tf), lambda b, i, k: (layer, 0, k)),
            pl.BlockSpec((1, tf, D_MODEL), lambda b, i, k: (layer, k, 0)),
        ],
        out_specs=pl.BlockSpec((1, tm, D_MODEL), lambda b, i, k: (b, i, 0)),
        out_shape=jax.ShapeDtypeStruct((b_sz, n, D_MODEL), f32),
        scratch_shapes=[pltpu.VMEM((tm, D_MODEL), bf16), pltpu.VMEM((tm, 1), f32)],
        compiler_params=_params(("parallel", "parallel", "arbitrary"), VMEM_LIMIT_WIDE),
        name="ffn",
    )(x, mod, g_pre, g_post, w1, w2)


def _layer(x, mod, wl, hg_tables, dft, layer, n_seq, ctx, s0, rope, is_context, tq):
    b_sz, n, _ = x.shape
    n_seqs = b_sz * n // n_seq
    proj = _inproj(x, mod, wl["g_pre_mix"], wl["w_in"], layer, INPROJ_TM).reshape(n_seqs, n_seq, IN_COLS_PAD)
    y_fn = _fourier(proj, dft[0], dft[1], dft[2], min(n_seq, 512))
    hg = _hgrn(proj, wl["hg_lb"], hg_tables, layer, s0, is_context)
    mla = _mla(proj, wl["wq"], wl["wkt"], wl["wv"], wl["q_norm"], wl["kv_norm"], ctx, layer, rope, is_context,
               tq)
    flat = lambda a: a.reshape(b_sz, n, a.shape[-1])
    x = _outproj(x, flat(y_fn), flat(hg[0]), flat(hg[1]), flat(proj), flat(mla[0]), wl["w_out"],
                 wl["hg_gain"], wl["g_post_mix"], mod, layer, OUTPROJ_TM)
    x = _ffn(x, mod, wl["g_pre_ff"], wl["g_post_ff"], wl["w_ff1"], wl["w_ff2"], layer, FFN_TM, FFN_TF)
    extras = (mla[1], mla[2], hg[2]) if is_context else None
    return x, extras


def kernel(x_prompt, x_sample, c, cache_ckv, cache_krope, state_hgrn, c_ctx, w_ada, b_ada, g_pre_mix,
           g_post_mix, g_pre_ff, g_post_ff, w_in, hg_lb, hg_gain, mla_q_norm, mla_kv_norm, w_q_b, w_kv_b,
           w_out, w_ff1, w_ff2):
    batch, seq, _ = x_prompt.shape
    dec_batch, dec_seq, _ = x_sample.shape

    rows = 16
    cond = jnp.concatenate([c_ctx[None, :], c, jnp.zeros((rows - 1 - dec_batch, D_MODEL), f32)], axis=0)
    mod = _ada(cond, w_ada, b_ada).reshape(DEPTH, rows, 6, D_MODEL)

    hg_tables = _hgrn_tables()
    ccs = _channel_tables()
    dft_p = (ccs,) + _dft_tables(seq)
    dft_s = (ccs,) + _dft_tables(dec_seq)
    rope = _rope_tables(dec_seq)

    yp = x_prompt.reshape(1, batch * seq, D_MODEL)
    ys = x_sample
    ckv_list, kr_list, st_list = [], [], []
    wq = w_q_b.reshape(DEPTH, Q_LORA, MLA_HEADS, MLA_NOPE + MLA_ROPE).astype(bf16)
    wq = jnp.pad(wq, ((0, 0), (0, 0), (0, 0), (0, QHEAD_PAD - MLA_NOPE - MLA_ROPE)))
    wkv = w_kv_b.astype(bf16).reshape(DEPTH, KV_LORA, MLA_HEADS, MLA_NOPE + MLA_V)
    stacked = {
        "w_in": jnp.pad(w_in, ((0, 0), (0, 0), (0, IN_COLS_PAD - IN_COLS))).astype(bf16),
        "wq": wq.transpose(0, 2, 1, 3),
        "wkt": wkv[:, :, :, :MLA_NOPE].reshape(DEPTH, KV_LORA, MLA_HEADS * MLA_NOPE).transpose(0, 2, 1),
        "wv": wkv[:, :, :, MLA_NOPE:].reshape(DEPTH, KV_LORA, MLA_HEADS * MLA_V),
        "w_out": w_out.astype(bf16),
        "w_ff1": w_ff1.astype(bf16),
        "w_ff2": w_ff2.astype(bf16),
    }
    for l in range(DEPTH):
        wl = {
            **stacked,
            "g_pre_mix": g_pre_mix[l][None, :], "g_post_mix": g_post_mix[l][None, :],
            "g_pre_ff": g_pre_ff[l][None, :], "g_post_ff": g_post_ff[l][None, :],
            "hg_lb": hg_lb, "hg_gain": hg_gain[l][None, :],
            "q_norm": mla_q_norm[l][None, :], "kv_norm": mla_kv_norm[l][None, :],
        }
        yp, (ckv_l, kr_l, st_l) = _layer(yp, mod[l, 0:1], wl, hg_tables, dft_p, l, seq, None, None, None,
                                         True, seq)
        ckv_list.append(ckv_l)
        kr_list.append(kr_l)
        st_list.append(st_l)
        ys, _ = _layer(ys, mod[l, 1:1 + dec_batch], wl, hg_tables, dft_s, l, dec_seq,
                       (cache_ckv, cache_krope), state_hgrn, rope, False, 1024)

    return (yp.reshape(batch, seq, D_MODEL), ys, jnp.stack(ckv_list, axis=1), jnp.stack(kr_list, axis=1),
            jnp.stack(st_list, axis=1))
```

```python
import functools

import numpy as np
import jax
import jax.numpy as jnp
from jax import lax
from jax.experimental import pallas as pl
from jax.experimental.pallas import tpu as pltpu

f32 = jnp.float32
bf16 = jnp.bfloat16

D_MODEL = 2048
DEPTH = 2
GRID_W = 64
FN_HEADS = 4
FN_DH = 128
FN_WIDTH = FN_HEADS * FN_DH
HG_HEADS = 4
HG_DK = 128
HG_DV = 128
HG_QK = HG_HEADS * HG_DK
HG_WIDTH = HG_HEADS * HG_DV
MLA_HEADS = 8
MLA_NOPE = 128
MLA_ROPE = 64
MLA_V = 128
Q_LORA = 768
KV_LORA = 512
MLA_WIDTH = MLA_HEADS * MLA_V
D_FF = 4 * D_MODEL
ROPE_BASE = 10000.0
EPS = 1e-6
F_FLOOR = 1e-30

LANES = 128
SUBLANES = 8
MXU_DIM = 256
IN_COLS = 4416
IN_COLS_PAD = 4608
IN_TILE = 2304
COL_HQ, COL_HV, COL_HFF, COL_HFB, COL_HGATE = 1, 2, 3, 4, 5
COL_QA = 4
KVR_WIDTH = 640
COL_KVR = 6
QHEAD_PAD = 256
HG_CHUNK = 128
HG_LEVELS = (64, 32, 16, 8, 4, 2, 1)
HG_SPLIT = 2
HG_SEQS = 4
ADA_ROWS = 2 * SUBLANES
ADA_TN = 2048
FOURIER_TR = 512
MLA_EXPAND_ROWS = 512
INPROJ_TM = 1024
OUTPROJ_TM = 512
MLA_KEY_CHUNK = 1280
MLA_SEQS = 4
FFN_TM = 1024
FFN_TF = 1024
VMEM_LIMIT = 56 * 1024 * 1024
VMEM_LIMIT_WIDE = 62 * 1024 * 1024


def _params(semantics, vmem_limit=VMEM_LIMIT):
    return pltpu.CompilerParams(dimension_semantics=semantics, vmem_limit_bytes=vmem_limit)


def _dot(a, b):
    return jnp.dot(a, b, preferred_element_type=f32)


def _dot_nt(a, b):
    return lax.dot_general(a, b, (((1,), (1,)), ((), ())), preferred_element_type=f32)


def _dot_tn(a, b):
    return lax.dot_general(a, b, (((0,), (0,)), ((), ())), preferred_element_type=f32)


def _rms(x):
    return x * lax.rsqrt(jnp.mean(x * x, axis=-1, keepdims=True) + EPS)


def _modulated_norm(x_ref, g_ref, scale_row, shift_row, h_scr, inv_scr):
    x = x_ref[0]
    inv_scr[...] = lax.rsqrt(jnp.mean(x * x, axis=-1, keepdims=True) + EPS)
    gain = g_ref[...] * (1.0 + scale_row)
    h_scr[...] = (x_ref[0] * inv_scr[...] * gain + shift_row).astype(bf16)


def _ada_kernel(cond_ref, w_ref, b_ref, o_ref):
    cnd = cond_ref[...]
    act = cnd * jax.nn.sigmoid(cnd)
    o_ref[0] = _dot(act.astype(bf16), w_ref[0].astype(bf16)) + b_ref[0]


def _ada(cond, w_ada, b_ada):
    rows = cond.shape[0]
    tn = ADA_TN
    return pl.pallas_call(
        _ada_kernel,
        grid=(DEPTH, 6 * D_MODEL // tn),
        in_specs=[
            pl.BlockSpec((rows, D_MODEL), lambda l, j: (0, 0)),
            pl.BlockSpec((1, D_MODEL, tn), lambda l, j: (l, 0, j)),
            pl.BlockSpec((1, 1, tn), lambda l, j: (l, 0, j)),
        ],
        out_specs=pl.BlockSpec((1, rows, tn), lambda l, j: (l, 0, j)),
        out_shape=jax.ShapeDtypeStruct((DEPTH, rows, 6 * D_MODEL), f32),
        compiler_params=_params(("parallel", "parallel")),
        name="ada_mod",
    )(cond, w_ada, b_ada.reshape(DEPTH, 1, 6 * D_MODEL))


def _inproj_kernel(x_ref, mod_ref, g_ref, w_ref, o_ref, h_scr, inv_scr):
    @pl.when(pl.program_id(2) == 0)
    def _():
        _modulated_norm(x_ref, g_ref, mod_ref[0, 1:2, :], mod_ref[0, 0:1, :], h_scr, inv_scr)
        o_ref[0] = _dot(h_scr[...], w_ref[0])

    @pl.when(pl.program_id(2) > 0)
    def _():
        o_ref[0] = _dot(h_scr[...], w_ref[0])


def _inproj(x, mod, g, w_in, layer, tm):
    b_sz, n, _ = x.shape
    return pl.pallas_call(
        _inproj_kernel,
        grid=(b_sz, n // tm, IN_COLS_PAD // IN_TILE),
        in_specs=[
            pl.BlockSpec((1, tm, D_MODEL), lambda b, i, j: (b, i, 0)),
            pl.BlockSpec((1, 6, D_MODEL), lambda b, i, j: (b, 0, 0)),
            pl.BlockSpec((1, D_MODEL), lambda b, i, j: (0, 0)),
            pl.BlockSpec((1, D_MODEL, IN_TILE), lambda b, i, j: (layer, 0, j)),
        ],
        out_specs=pl.BlockSpec((1, tm, IN_TILE), lambda b, i, j: (b, i, j)),
        out_shape=jax.ShapeDtypeStruct((b_sz, n, IN_COLS_PAD), f32),
        scratch_shapes=[pltpu.VMEM((tm, D_MODEL), bf16), pltpu.VMEM((tm, 1), f32)],
        compiler_params=_params(("parallel", "parallel", "arbitrary"), VMEM_LIMIT_WIDE),
        name="inproj",
    )(x, mod, g, w_in)


def _fourier_kernel(u_ref, ccs_ref, cn_ref, sn_ref, o_ref, ucs_scr):
    @pl.when(pl.program_id(1) == 0)
    def _():
        for h in range(FN_HEADS):
            sl = slice(h * FN_DH, (h + 1) * FN_DH)
            r = _dot(u_ref[0, :, sl].astype(bf16), ccs_ref[...]).astype(bf16)
            ucs_scr[:, sl] = r[:, :FN_DH]
            ucs_scr[:, FN_WIDTH + h * FN_DH:FN_WIDTH + (h + 1) * FN_DH] = r[:, FN_DH:]

    y = _dot(cn_ref[...], ucs_scr[:, :FN_WIDTH]) + _dot(sn_ref[...], ucs_scr[:, FN_WIDTH:])
    o_ref[0] = y.astype(bf16)


def _fourier(proj, ccs, cn, sn_neg, tr):
    b_sz, n, _ = proj.shape
    return pl.pallas_call(
        _fourier_kernel,
        grid=(b_sz, n // tr),
        in_specs=[
            pl.BlockSpec((1, n, FN_WIDTH), lambda b, i: (b, 0, 0)),
            pl.BlockSpec((FN_DH, 2 * FN_DH), lambda b, i: (0, 0)),
            pl.BlockSpec((tr, n), lambda b, i: (i, 0)),
            pl.BlockSpec((tr, n), lambda b, i: (i, 0)),
        ],
        out_specs=pl.BlockSpec((1, tr, FN_WIDTH), lambda b, i: (b, i, 0)),
        out_shape=jax.ShapeDtypeStruct((b_sz, n, FN_WIDTH), bf16),
        scratch_shapes=[pltpu.VMEM((n, 2 * FN_WIDTH), bf16)],
        compiler_params=_params(("parallel", "arbitrary")),
        name="fourier",
    )(proj, ccs, cn, sn_neg)


def _dft_tables(n):
    j = np.arange(n, dtype=np.int64)
    ang = ((j[:, None] * j[None, :]) % n).astype(np.float64) * (2.0 * np.pi / n)
    scale_n = 1.0 / np.sqrt(n)
    cn = jnp.asarray(np.cos(ang) * scale_n, dtype=bf16)
    sn_neg = jnp.asarray(-np.sin(ang) * scale_n, dtype=bf16)
    return cn, sn_neg


def _channel_tables():
    k = np.arange(FN_DH)
    ang = 2.0 * np.pi * ((k[:, None] * k[None, :]) % FN_DH) / FN_DH
    table = np.concatenate([np.cos(ang), np.sin(ang)], axis=1) / np.sqrt(FN_DH)
    return jnp.asarray(table, dtype=bf16)


def _hgrn_tables():
    c = HG_CHUNK
    idx = np.arange(c)
    blocks = [(idx[None, :] <= idx[:, None])]
    masks = []
    for h in HG_LEVELS:
        mid = (idx // (2 * h)) * (2 * h) + h
        upper = idx >= mid
        if h < SUBLANES:
            row_up = (idx[None, :] >= mid[:, None]) & (idx[None, :] <= idx[:, None])
            row_lo = (idx[None, :] > idx[:, None]) & (idx[None, :] < mid[:, None])
            blocks.append(np.where(upper[:, None], row_up, row_lo))
        same = (idx[:, None] // (2 * h)) == (idx[None, :] // (2 * h))
        masks.append(same & upper[:, None] & ~upper[None, :])
    masks.append(np.eye(c, dtype=bool))
    a_f = np.concatenate([b.astype(np.float32) for b in blocks], axis=0)
    a_b = np.concatenate([b[::-1, ::-1].astype(np.float32) for b in blocks], axis=0)
    m_f = np.stack([m.astype(np.float32) for m in masks])
    m_b = np.stack([m[::-1, ::-1].astype(np.float32) for m in masks])
    rep = lambda a: jnp.asarray(np.concatenate([a] * HG_SPLIT, axis=1), dtype=bf16)
    return rep(a_f), rep(a_b), jnp.asarray(m_f), jnp.asarray(m_b)


def _block_row(x, h, row):
    c, w = x.shape
    xb = x.reshape(c // (2 * h), 2 * h, w)
    return jnp.broadcast_to(xb[:, row:row + 1, :], xb.shape).reshape(c, w)


def _hgrn_direction(q, v, pre, lb, a_ref, m_ref, states, backward):
    c = HG_CHUNK
    nl = len(HG_LEVELS)
    f = lb + (1.0 - lb) * jax.nn.sigmoid(pre)
    lf = jnp.log(jnp.maximum(f, F_FLOOR))
    kk = 1.0 - f
    pieces = []
    rem = lf
    for _ in range(HG_SPLIT):
        piece = rem.astype(bf16)
        pieces.append(piece)
        rem = rem - piece.astype(f32)
    sums = _dot(a_ref[...], jnp.concatenate(pieces, axis=0))
    cum = sums[0:c]
    total = cum[0:1] if backward else cum[c - 1:c]
    q16 = q.astype(bf16)
    k16 = kk.astype(bf16)
    v16 = v.astype(bf16)
    decay16 = lambda log_decay: jnp.exp(log_decay.astype(bf16))
    q_inter = q16 * decay16(cum)
    k_state = k16 * decay16(jnp.minimum(total - cum, 0.0))
    e_total = jnp.exp(total)
    q_levels, k_levels = [q16], [k16]
    fine = 0
    for h in HG_LEVELS:
        if h >= SUBLANES:
            ref = _block_row(cum, h, h if backward else h - 1)
            el = decay16(-jnp.abs(cum - ref))
        else:
            fine += 1
            el = decay16(sums[fine * c:(fine + 1) * c])
        q_levels.append(q16 * el)
        k_levels.append(k16 * el)
    outs, new_states = [], []
    for h in range(HG_HEADS):
        sl = slice(h * HG_DK, (h + 1) * HG_DK)
        vh = v16[:, sl]
        sc = m_ref[nl] * _dot_nt(q_levels[0][:, sl], k_levels[0][:, sl])
        for li in range(nl):
            sc = sc + m_ref[li] * _dot_nt(q_levels[li + 1][:, sl], k_levels[li + 1][:, sl])
        st = states[h]
        outs.append(_dot(sc.astype(bf16), vh) + _dot_nt(q_inter[:, sl], st.astype(bf16)))
        new_states.append(e_total[:, sl] * st + _dot_tn(vh, k_state[:, sl]))
    return outs, new_states


def _hgrn_kernel(layer, has_s0, has_sout, *refs):
    (qf_ref, vf_ref, ff_ref, qb_ref, vb_ref, fb_ref, lb_ref, af_ref, ab_ref, mf_ref, mb_ref) = refs[:11]
    pos = 11
    s0_ref = None
    if has_s0:
        s0_ref = refs[pos]
        pos += 1
    of_ref, ob_ref = refs[pos], refs[pos + 1]
    pos += 2
    sout_ref = None
    if has_sout:
        sout_ref = refs[pos]
        pos += 1
    st_scr = refs[pos]
    ci = pl.program_id(1)

    n_seq = qf_ref.shape[0]

    @pl.when(ci == 0)
    def _():
        for s in range(n_seq):
            for d in range(2):
                for h in range(HG_HEADS):
                    if has_s0:
                        st_scr[s, d, h] = s0_ref[s, 0, d, h].T
                    else:
                        st_scr[s, d, h] = jnp.zeros((HG_DV, HG_DK), f32)

    raw = lb_ref[...]
    mx = raw[0]
    for i in range(1, DEPTH):
        mx = jnp.maximum(mx, raw[i])
    ex = [jnp.exp(raw[i] - mx) for i in range(DEPTH)]
    den = ex[0]
    for i in range(1, DEPTH):
        den = den + ex[i]
    lb = jnp.zeros_like(den)
    for i in range(1, layer + 1):
        lb = lb + ex[i] / den

    loaded = [[[st_scr[s, d, h] for h in range(HG_HEADS)] for d in range(2)] for s in range(n_seq)]
    results = []
    for s in range(n_seq):
        o_f, st_f = _hgrn_direction(qf_ref[s], vf_ref[s], ff_ref[s], lb[0:1], af_ref, mf_ref, loaded[s][0], False)
        o_b, st_b = _hgrn_direction(qb_ref[s], vb_ref[s], fb_ref[s], lb[1:2], ab_ref, mb_ref, loaded[s][1], True)
        of_ref[s] = jnp.concatenate(o_f, axis=1)
        ob_ref[s] = jnp.concatenate(o_b, axis=1)
        results.append((st_f, st_b))
    for s in range(n_seq):
        for h in range(HG_HEADS):
            st_scr[s, 0, h] = results[s][0][h]
            st_scr[s, 1, h] = results[s][1][h]

    if has_sout:
        @pl.when(ci == pl.num_programs(1) - 1)
        def _():
            for s in range(n_seq):
                for d in range(2):
                    for h in range(HG_HEADS):
                        sout_ref[s, d, h] = st_scr[s, d, h].T


def _hgrn(proj, hg_lb, tables, layer, s0, want_state):
    b_sz, n, _ = proj.shape
    nc = n // HG_CHUNK
    a_f, a_b, m_f, m_b = tables
    c = HG_CHUNK
    sb = HG_SEQS
    fwd = lambda col: pl.BlockSpec((sb, c, HG_QK), lambda b, i: (b, i, col))
    bwd = lambda col: pl.BlockSpec((sb, c, HG_QK), lambda b, i: (b, nc - 1 - i, col))
    const = lambda shape: pl.BlockSpec(shape, lambda b, i: (0,) * len(shape))
    in_specs = [fwd(COL_HQ), fwd(COL_HV), fwd(COL_HFF), bwd(COL_HQ), bwd(COL_HV), bwd(COL_HFB),
                const(hg_lb.shape), const(a_f.shape), const(a_b.shape), const(m_f.shape), const(m_b.shape)]
    args = [proj, proj, proj, proj, proj, proj, hg_lb, a_f, a_b, m_f, m_b]
    state_spec = pl.BlockSpec((sb, 2, HG_HEADS, HG_DK, HG_DV), lambda b, i: (b, 0, 0, 0, 0))
    if s0 is not None:
        in_specs.append(pl.BlockSpec((sb, 1, 2, HG_HEADS, HG_DK, HG_DV), lambda b, i: (b, layer, 0, 0, 0, 0)))
        args.append(s0)
    out_specs = [pl.BlockSpec((sb, c, HG_WIDTH), lambda b, i: (b, i, 0)),
                 pl.BlockSpec((sb, c, HG_WIDTH), lambda b, i: (b, nc - 1 - i, 0))]
    out_shape = [jax.ShapeDtypeStruct((b_sz, n, HG_WIDTH), f32)] * 2
    if want_state:
        out_specs.append(state_spec)
        out_shape.append(jax.ShapeDtypeStruct((b_sz, 2, HG_HEADS, HG_DK, HG_DV), f32))
    return pl.pallas_call(
        functools.partial(_hgrn_kernel, layer, s0 is not None, want_state),
        grid=(b_sz // sb, nc),
        in_specs=in_specs,
        out_specs=out_specs,
        out_shape=out_shape,
        scratch_shapes=[pltpu.VMEM((sb, 2, HG_HEADS, HG_DV, HG_DK), f32)],
        compiler_params=_params(("parallel", "arbitrary")),
        name="hgrn",
    )(*args)


def _rope_rotate(x, cos, sin):
    lane = lax.broadcasted_iota(jnp.int32, x.shape, 1)
    first_half = (lane % (MLA_ROPE // 2)) < (MLA_ROPE // 4)
    quarter = MLA_ROPE // 4
    rot = jnp.where(first_half, -pltpu.roll(x, LANES - quarter, 1), pltpu.roll(x, quarter, 1))
    return x * cos + rot * sin


def _mla_kernel(n_ctx, use_rope, want_cache, tq, *refs):
    qa_ref, kvr_ref, wq_ref, wkt_ref, wv_ref, qn_ref, kvn_ref = refs[:7]
    pos = 7
    if n_ctx:
        cckv_ref, ckr_ref = refs[pos], refs[pos + 1]
        pos += 2
    if use_rope:
        cosq_ref, sinq_ref, cosk_ref, sink_ref = refs[pos:pos + 4]
        pos += 4
    y_ref = refs[pos]
    pos += 1
    if want_cache:
        ockv_ref, okr_ref = refs[pos], refs[pos + 1]
        pos += 2
    kcat_scr, v_scr, o_scr = refs[pos:pos + 3]
    n_seq, n = kvr_ref.shape[0], kvr_ref.shape[1]
    sk = n_ctx + n
    total = n_seq * n
    rows = MLA_EXPAND_ROWS if total % MLA_EXPAND_ROWS == 0 else total

    def pieces(r0, nrows):
        out, r = [], r0
        while r < r0 + nrows:
            s, off = divmod(r, n)
            cnt = min(n - off, r0 + nrows - r)
            out.append((s, off, cnt))
            r += cnt
        return out

    def expand(r0, nrows, c_kv, krp):
        k_t = _dot_nt(wkt_ref[0], c_kv)
        eye = (lax.broadcasted_iota(jnp.int32, (LANES, LANES), 0)
               == lax.broadcasted_iota(jnp.int32, (LANES, LANES), 1)).astype(bf16)
        krp_t = _dot_nt(eye, krp).astype(bf16)
        vals = _dot(c_kv, wv_ref[0])
        ones = jnp.ones((nrows, MLA_V), bf16)
        for h in range(MLA_HEADS):
            kcat_scr[h, 0:MLA_NOPE, r0:r0 + nrows] = k_t[h * MLA_NOPE:(h + 1) * MLA_NOPE].astype(bf16)
            kcat_scr[h, MLA_NOPE:QHEAD_PAD, r0:r0 + nrows] = krp_t
            v_scr[h, r0:r0 + nrows, 0:MLA_V] = vals[:, h * MLA_V:(h + 1) * MLA_V].astype(bf16)
            v_scr[h, r0:r0 + nrows, MLA_V:2 * MLA_V] = ones

    @pl.when(pl.program_id(1) == 0)
    def _():
        if n_ctx:
            zeros = jnp.zeros((n_ctx, LANES - MLA_ROPE), f32)
            krp_ctx = jnp.concatenate([ckr_ref[0, 0], zeros], axis=1).astype(bf16)
            expand(0, n_ctx, cckv_ref[0, 0].astype(bf16), krp_ctx)
        for r0 in range(0, total, rows):
            parts = pieces(r0, rows)
            blk = jnp.concatenate([kvr_ref[s, off:off + cnt] for s, off, cnt in parts], axis=0)
            c_kv = _rms(blk[:, :KV_LORA]) * kvn_ref[...]
            krp = blk[:, KV_LORA:]
            if want_cache:
                at = 0
                for s, off, cnt in parts:
                    ockv_ref[s, off:off + cnt] = c_kv[at:at + cnt]
                    okr_ref[s, off:off + cnt] = krp[at:at + cnt, :MLA_ROPE]
                    at += cnt
            if use_rope:
                krp = _rope_rotate(krp, cosk_ref[r0:r0 + rows], sink_ref[r0:r0 + rows])
            expand(n_ctx + r0, rows, c_kv.astype(bf16), krp.astype(bf16))

    scale = float(MLA_NOPE + MLA_ROPE) ** -0.5
    qa = jnp.concatenate([qa_ref[s] for s in range(n_seq)], axis=0)
    qn = (_rms(qa) * qn_ref[...]).astype(bf16)
    n_kc = 1 if sk <= MLA_KEY_CHUNK else sk // (2 * MXU_DIM)
    kc = sk // n_kc

    def head(h):
        q = _dot(qn, wq_ref[0, h]) * scale
        q_rope = q[:, MLA_NOPE:]
        if use_rope:
            q_rope = _rope_rotate(q_rope, cosq_ref[...], sinq_ref[...])
        q_cat = jnp.concatenate([q[:, :MLA_NOPE], q_rope], axis=1).astype(bf16)
        outs = []
        for s in range(n_seq):
            q_s = q_cat[s * tq:(s + 1) * tq]
            m_run = acc = None
            for c in range(n_kc):
                k0 = s * sk + c * kc
                sc = _dot(q_s, kcat_scr[h, :, k0:k0 + kc])
                m_c = jnp.max(sc, axis=-1, keepdims=True)
                m_new = m_c if c == 0 else jnp.maximum(m_run, m_c)
                p = jnp.exp((sc - m_new).astype(bf16))
                pv = _dot(p, v_scr[h, k0:k0 + kc])
                acc = pv if c == 0 else jnp.exp(m_run - m_new) * acc + pv
                m_run = m_new
            outs.append((acc[:, :MLA_V] / acc[:, MLA_V:]).astype(bf16))
        return outs

    if n_kc == 1:
        for h in range(MLA_HEADS):
            for s, o in enumerate(head(h)):
                y_ref[s, :, h * MLA_V:(h + 1) * MLA_V] = o
    else:
        def body(h, carry):
            o_scr[h] = head(h)[0]
            return carry

        lax.fori_loop(0, MLA_HEADS, body, 0, unroll=4)
        for h in range(MLA_HEADS):
            y_ref[0, :, h * MLA_V:(h + 1) * MLA_V] = o_scr[h]


def _mla(proj, wq, wkt, wv, q_norm, kv_norm, ctx, layer, rope, want_cache, tq):
    b_sz, n, _ = proj.shape
    n_ctx = 0 if ctx is None else ctx[0].shape[2]
    sk = n_ctx + n
    sb = MLA_SEQS if (ctx is None and rope is None and n == tq) else 1
    single = pl.Buffered(1)
    const = lambda shape: pl.BlockSpec(shape, lambda b, i: (0,) * len(shape), pipeline_mode=single)
    in_specs = [
        pl.BlockSpec((sb, tq, Q_LORA), lambda b, i: (b, i, COL_QA)),
        pl.BlockSpec((sb, n, KVR_WIDTH), lambda b, i: (b, 0, COL_KVR), pipeline_mode=single),
        pl.BlockSpec((1,) + wq.shape[1:], lambda b, i: (layer, 0, 0, 0), pipeline_mode=single),
        pl.BlockSpec((1,) + wkt.shape[1:], lambda b, i: (layer, 0, 0), pipeline_mode=single),
        pl.BlockSpec((1,) + wv.shape[1:], lambda b, i: (layer, 0, 0), pipeline_mode=single),
        const(q_norm.shape), const(kv_norm.shape),
    ]
    args = [proj, proj, wq, wkt, wv, q_norm, kv_norm]
    if ctx is not None:
        in_specs += [
            pl.BlockSpec((1, 1, n_ctx, KV_LORA), lambda b, i: (b, layer, 0, 0), pipeline_mode=single),
            pl.BlockSpec((1, 1, n_ctx, MLA_ROPE), lambda b, i: (b, layer, 0, 0), pipeline_mode=single)]
        args += [ctx[0], ctx[1]]
    if rope is not None:
        cos, sin = rope
        in_specs += [pl.BlockSpec((tq, LANES), lambda b, i: (i, 0)),
                     pl.BlockSpec((tq, LANES), lambda b, i: (i, 0)),
                     const(cos.shape), const(sin.shape)]
        args += [cos, sin, cos, sin]
    out_specs = [pl.BlockSpec((sb, tq, MLA_WIDTH), lambda b, i: (b, i, 0))]
    out_shape = [jax.ShapeDtypeStruct((b_sz, n, MLA_WIDTH), bf16)]
    if want_cache:
        out_specs += [pl.BlockSpec((sb, n, KV_LORA), lambda b, i: (b, 0, 0)),
                      pl.BlockSpec((sb, n, MLA_ROPE), lambda b, i: (b, 0, 0))]
        out_shape += [jax.ShapeDtypeStruct((b_sz, n, KV_LORA), f32),
                      jax.ShapeDtypeStruct((b_sz, n, MLA_ROPE), f32)]
    return pl.pallas_call(
        functools.partial(_mla_kernel, n_ctx, rope is not None, want_cache, tq),
        grid=(b_sz // sb, n // tq),
        in_specs=in_specs,
        out_specs=out_specs,
        out_shape=out_shape,
        scratch_shapes=[pltpu.VMEM((MLA_HEADS, QHEAD_PAD, sb * sk), bf16),
                        pltpu.VMEM((MLA_HEADS, sb * sk, 2 * MLA_V), bf16),
                        pltpu.VMEM((MLA_HEADS, tq, MLA_V), bf16)],
        compiler_params=_params(("parallel", "arbitrary")),
        name="mla",
    )(*args)


def _rope_tables(n):
    t = np.arange(n)
    r = (t // GRID_W).astype(np.float32)
    col = (t % GRID_W).astype(np.float32)
    nf = MLA_ROPE // 4
    inv = np.float32(ROPE_BASE) ** (-np.arange(nf, dtype=np.float32) / np.float32(nf))
    ar = r[:, None] * inv
    ac = col[:, None] * inv
    ang = np.concatenate([ar, ar, ac, ac], axis=-1).astype(np.float32)
    pad = LANES - MLA_ROPE
    cos = np.concatenate([np.cos(ang), np.ones((n, pad), np.float32)], axis=-1)
    sin = np.concatenate([np.sin(ang), np.zeros((n, pad), np.float32)], axis=-1)
    return jnp.asarray(cos, dtype=f32), jnp.asarray(sin, dtype=f32)


def _outproj_kernel(x_ref, yfn_ref, of_ref, ob_ref, gate_ref, ymla_ref, w_ref, hgg_ref, gpost_ref,
                    mod_ref, o_ref, inv_scr):
    o_hg = of_ref[0] + ob_ref[0]
    gate = gate_ref[0]
    gate = gate * jax.nn.sigmoid(gate)
    parts = [yfn_ref[0]]
    for h in range(HG_HEADS):
        sl = slice(h * HG_DV, (h + 1) * HG_DV)
        parts.append((_rms(o_hg[:, sl]) * hgg_ref[:, sl] * gate[:, sl]).astype(bf16))
    parts.append(ymla_ref[0])
    o_ref[0] = _dot(jnp.concatenate(parts, axis=1), w_ref[0])
    y = o_ref[0]
    inv_scr[...] = lax.rsqrt(jnp.mean(y * y, axis=-1, keepdims=True) + EPS)
    gain = mod_ref[0, 2:3, :] * gpost_ref[...]
    o_ref[0] = x_ref[0] + o_ref[0] * inv_scr[...] * gain


def _outproj(x, y_fn, o_f, o_b, proj, y_mla, w_out, hg_gain, g_post, mod, layer, tm):
    b_sz, n, _ = x.shape
    tok = lambda width, col=0: pl.BlockSpec((1, tm, width), lambda b, i: (b, i, col))
    return pl.pallas_call(
        _outproj_kernel,
        grid=(b_sz, n // tm),
        in_specs=[
            tok(D_MODEL), tok(FN_WIDTH), tok(HG_WIDTH), tok(HG_WIDTH), tok(HG_WIDTH, COL_HGATE),
            tok(MLA_WIDTH),
            pl.BlockSpec((1, D_MODEL, D_MODEL), lambda b, i: (layer, 0, 0)),
            pl.BlockSpec((1, HG_WIDTH), lambda b, i: (0, 0)),
            pl.BlockSpec((1, D_MODEL), lambda b, i: (0, 0)),
            pl.BlockSpec((1, 6, D_MODEL), lambda b, i: (b, 0, 0)),
        ],
        out_specs=tok(D_MODEL),
        out_shape=jax.ShapeDtypeStruct((b_sz, n, D_MODEL), f32),
        scratch_shapes=[pltpu.VMEM((tm, 1), f32)],
        compiler_params=_params(("parallel", "parallel")),
        name="outproj",
    )(x, y_fn, o_f, o_b, proj, y_mla, w_out, hg_gain, g_post, mod)


def _ffn_kernel(x_ref, mod_ref, gpre_ref, gpost_ref, w1_ref, w2_ref, o_ref, h_scr, inv_scr):
    k = pl.program_id(2)

    def hidden_slice():
        a = jnp.maximum(_dot(h_scr[...], w1_ref[0]), 0.0)
        return _dot((a * a).astype(bf16), w2_ref[0])

    @pl.when(k == 0)
    def _():
        _modulated_norm(x_ref, gpre_ref, mod_ref[0, 4:5, :], mod_ref[0, 3:4, :], h_scr, inv_scr)
        o_ref[0] = hidden_slice()

    last = pl.num_programs(2) - 1

    @pl.when((k > 0) & (k < last))
    def _():
        o_ref[0] += hidden_slice()

    @pl.when(k == last)
    def _():
        o_ref[0] += hidden_slice()
        f = o_ref[0]
        inv_scr[...] = lax.rsqrt(jnp.mean(f * f, axis=-1, keepdims=True) + EPS)
        gain = mod_ref[0, 5:6, :] * gpost_ref[...]
        o_ref[0] = x_ref[0] + o_ref[0] * inv_scr[...] * gain


def _ffn(x, mod, g_pre, g_post, w1, w2, layer, tm, tf):
    b_sz, n, _ = x.shape
    return pl.pallas_call(
        _ffn_kernel,
        grid=(b_sz, n // tm, D_FF // tf),
        in_specs=[
            pl.BlockSpec((1, tm, D_MODEL), lambda b, i, k: (b, i, 0)),
            pl.BlockSpec((1, 6, D_MODEL), lambda b, i, k: (b, 0, 0)),
            pl.BlockSpec((1, D_MODEL), lambda b, i, k: (0, 0)),
            pl.BlockSpec((1, D_MODEL), lambda b, i, k: (0, 0)),
            pl.BlockSpec((1, D_MODEL, tf), lambda b, i, k: (layer, 0, k)),
            pl.BlockSpec((1, tf, D_MODEL), lambda b, i, k: (layer, k, 0)),
        ],
        out_specs=pl.BlockSpec((1, tm, D_MODEL), lambda b, i, k: (b, i, 0)),
        out_shape=jax.ShapeDtypeStruct((b_sz, n, D_MODEL), f32),
        scratch_shapes=[pltpu.VMEM((tm, D_MODEL), bf16), pltpu.VMEM((tm, 1), f32)],
        compiler_params=_params(("parallel", "parallel", "arbitrary"), VMEM_LIMIT_WIDE),
        name="ffn",
    )(x, mod, g_pre, g_post, w1, w2)


def _layer(x, mod, wl, hg_tables, dft, layer, n_seq, ctx, s0, rope, is_context, tq):
    b_sz, n, _ = x.shape
    n_seqs = b_sz * n // n_seq
    proj = _inproj(x, mod, wl["g_pre_mix"], wl["w_in"], layer, INPROJ_TM).reshape(n_seqs, n_seq, IN_COLS_PAD)
    y_fn = _fourier(proj, dft[0], dft[1], dft[2], min(n_seq, FOURIER_TR))
    hg = _hgrn(proj, wl["hg_lb"], hg_tables, layer, s0, is_context)
    mla = _mla(proj, wl["wq"], wl["wkt"], wl["wv"], wl["q_norm"], wl["kv_norm"], ctx, layer, rope, is_context,
               tq)
    flat = lambda a: a.reshape(b_sz, n, a.shape[-1])
    x = _outproj(x, flat(y_fn), flat(hg[0]), flat(hg[1]), flat(proj), flat(mla[0]), wl["w_out"],
                 wl["hg_gain"], wl["g_post_mix"], mod, layer, OUTPROJ_TM)
    x = _ffn(x, mod, wl["g_pre_ff"], wl["g_post_ff"], wl["w_ff1"], wl["w_ff2"], layer, FFN_TM, FFN_TF)
    extras = (mla[1], mla[2], hg[2]) if is_context else None
    return x, extras


def kernel(x_prompt, x_sample, c, cache_ckv, cache_krope, state_hgrn, c_ctx, w_ada, b_ada, g_pre_mix,
           g_post_mix, g_pre_ff, g_post_ff, w_in, hg_lb, hg_gain, mla_q_norm, mla_kv_norm, w_q_b, w_kv_b,
           w_out, w_ff1, w_ff2):
    batch, seq, _ = x_prompt.shape
    dec_batch, dec_seq, _ = x_sample.shape

    rows = ADA_ROWS
    cond = jnp.concatenate([c_ctx[None, :], c, jnp.zeros((rows - 1 - dec_batch, D_MODEL), f32)], axis=0)
    mod = _ada(cond, w_ada, b_ada).reshape(DEPTH, rows, 6, D_MODEL)

    hg_tables = _hgrn_tables()
    ccs = _channel_tables()
    dft_p = (ccs,) + _dft_tables(seq)
    dft_s = (ccs,) + _dft_tables(dec_seq)
    rope = _rope_tables(dec_seq)

    yp = x_prompt.reshape(1, batch * seq, D_MODEL)
    ys = x_sample
    ckv_list, kr_list, st_list = [], [], []
    wq = w_q_b.reshape(DEPTH, Q_LORA, MLA_HEADS, MLA_NOPE + MLA_ROPE).astype(bf16)
    wq = jnp.pad(wq, ((0, 0), (0, 0), (0, 0), (0, QHEAD_PAD - MLA_NOPE - MLA_ROPE)))
    wkv = w_kv_b.astype(bf16).reshape(DEPTH, KV_LORA, MLA_HEADS, MLA_NOPE + MLA_V)
    stacked = {
        "w_in": jnp.pad(w_in, ((0, 0), (0, 0), (0, IN_COLS_PAD - IN_COLS))).astype(bf16),
        "wq": wq.transpose(0, 2, 1, 3),
        "wkt": wkv[:, :, :, :MLA_NOPE].reshape(DEPTH, KV_LORA, MLA_HEADS * MLA_NOPE).transpose(0, 2, 1),
        "wv": wkv[:, :, :, MLA_NOPE:].reshape(DEPTH, KV_LORA, MLA_HEADS * MLA_V),
        "w_out": w_out.astype(bf16),
        "w_ff1": w_ff1.astype(bf16),
        "w_ff2": w_ff2.astype(bf16),
    }
    for l in range(DEPTH):
        wl = {
            **stacked,
            "g_pre_mix": g_pre_mix[l][None, :], "g_post_mix": g_post_mix[l][None, :],
            "g_pre_ff": g_pre_ff[l][None, :], "g_post_ff": g_post_ff[l][None, :],
            "hg_lb": hg_lb, "hg_gain": hg_gain[l][None, :],
            "q_norm": mla_q_norm[l][None, :], "kv_norm": mla_kv_norm[l][None, :],
        }
        yp, (ckv_l, kr_l, st_l) = _layer(yp, mod[l, 0:1], wl, hg_tables, dft_p, l, seq, None, None, None,
                                         True, seq)
        ckv_list.append(ckv_l)
        kr_list.append(kr_l)
        st_list.append(st_l)
        ys, _ = _layer(ys, mod[l, 1:1 + dec_batch], wl, hg_tables, dft_s, l, dec_seq,
                       (cache_ckv, cache_krope), state_hgrn, rope, False, 1024)

    return (yp.reshape(batch, seq, D_MODEL), ys, jnp.stack(ckv_list, axis=1), jnp.stack(kr_list, axis=1),
            jnp.stack(st_list, axis=1))
```

```python
import functools

import numpy as np
import jax
import jax.numpy as jnp
from jax import lax
from jax.experimental import pallas as pl
from jax.experimental.pallas import tpu as pltpu

f32 = jnp.float32
bf16 = jnp.bfloat16

D_MODEL = 2048
DEPTH = 2
GRID_W = 64
FN_HEADS = 4
FN_DH = 128
FN_WIDTH = FN_HEADS * FN_DH
HG_HEADS = 4
HG_DK = 128
HG_DV = 128
HG_QK = HG_HEADS * HG_DK
HG_WIDTH = HG_HEADS * HG_DV
MLA_HEADS = 8
MLA_NOPE = 128
MLA_ROPE = 64
MLA_V = 128
Q_LORA = 768
KV_LORA = 512
MLA_WIDTH = MLA_HEADS * MLA_V
D_FF = 4 * D_MODEL
ROPE_BASE = 10000.0
EPS = 1e-6
F_FLOOR = 1e-30

LANES = 128
SUBLANES = 8
MXU_DIM = 256
IN_COLS = 4416
IN_COLS_PAD = 4608
IN_TILE = 2304
COL_HQ, COL_HV, COL_HFF, COL_HFB, COL_HGATE = 1, 2, 3, 4, 5
COL_QA = 4
KVR_WIDTH = 640
COL_KVR = 6
QHEAD_PAD = 256
HG_CHUNK = 128
HG_LEVELS = (64, 32, 16, 8, 4, 2, 1)
HG_SPLIT = 2
HG_SEQS = 4
ADA_ROWS = 2 * SUBLANES
ADA_TN = 2048
FOURIER_TR = 512
MLA_EXPAND_ROWS = 512
INPROJ_TM = 1024
OUTPROJ_TM = 512
MLA_KEY_CHUNK = 1280
MLA_SEQS = 4
FFN_TM = 1024
FFN_TF = 1024
VMEM_LIMIT = 56 * 1024 * 1024
VMEM_LIMIT_WIDE = 62 * 1024 * 1024


def _params(semantics, vmem_limit=VMEM_LIMIT):
    return pltpu.CompilerParams(dimension_semantics=semantics, vmem_limit_bytes=vmem_limit)


def _dot(a, b):
    return jnp.dot(a, b, preferred_element_type=f32)


def _dot_nt(a, b):
    return lax.dot_general(a, b, (((1,), (1,)), ((), ())), preferred_element_type=f32)


def _dot_tn(a, b):
    return lax.dot_general(a, b, (((0,), (0,)), ((), ())), preferred_element_type=f32)


def _rms(x):
    return x * lax.rsqrt(jnp.mean(x * x, axis=-1, keepdims=True) + EPS)


def _modulated_norm(x_ref, g_ref, scale_row, shift_row, h_scr, inv_scr):
    x = x_ref[0]
    inv_scr[...] = lax.rsqrt(jnp.mean(x * x, axis=-1, keepdims=True) + EPS)
    gain = g_ref[...] * (1.0 + scale_row)
    h_scr[...] = (x_ref[0] * inv_scr[...] * gain + shift_row).astype(bf16)


def _ada_kernel(cond_ref, w_ref, b_ref, o_ref):
    cnd = cond_ref[...]
    act = cnd * jax.nn.sigmoid(cnd)
    o_ref[0] = _dot(act.astype(bf16), w_ref[0].astype(bf16)) + b_ref[0]


def _ada(cond, w_ada, b_ada):
    rows = cond.shape[0]
    tn = ADA_TN
    return pl.pallas_call(
        _ada_kernel,
        grid=(DEPTH, 6 * D_MODEL // tn),
        in_specs=[
            pl.BlockSpec((rows, D_MODEL), lambda l, j: (0, 0)),
            pl.BlockSpec((1, D_MODEL, tn), lambda l, j: (l, 0, j)),
            pl.BlockSpec((1, 1, tn), lambda l, j: (l, 0, j)),
        ],
        out_specs=pl.BlockSpec((1, rows, tn), lambda l, j: (l, 0, j)),
        out_shape=jax.ShapeDtypeStruct((DEPTH, rows, 6 * D_MODEL), f32),
        compiler_params=_params(("parallel", "parallel")),
        name="ada_mod",
    )(cond, w_ada, b_ada.reshape(DEPTH, 1, 6 * D_MODEL))


def _inproj_kernel(x_ref, mod_ref, g_ref, w_ref, o_ref, h_scr, inv_scr):
    @pl.when(pl.program_id(2) == 0)
    def _():
        _modulated_norm(x_ref, g_ref, mod_ref[0, 1:2, :], mod_ref[0, 0:1, :], h_scr, inv_scr)
        o_ref[0] = _dot(h_scr[...], w_ref[0])

    @pl.when(pl.program_id(2) > 0)
    def _():
        o_ref[0] = _dot(h_scr[...], w_ref[0])


def _inproj(x, mod, g, w_in, layer, tm):
    b_sz, n, _ = x.shape
    return pl.pallas_call(
        _inproj_kernel,
        grid=(b_sz, n // tm, IN_COLS_PAD // IN_TILE),
        in_specs=[
            pl.BlockSpec((1, tm, D_MODEL), lambda b, i, j: (b, i, 0)),
            pl.BlockSpec((1, 6, D_MODEL), lambda b, i, j: (b, 0, 0)),
            pl.BlockSpec((1, D_MODEL), lambda b, i, j: (0, 0)),
            pl.BlockSpec((1, D_MODEL, IN_TILE), lambda b, i, j: (layer, 0, j)),
        ],
        out_specs=pl.BlockSpec((1, tm, IN_TILE), lambda b, i, j: (b, i, j)),
        out_shape=jax.ShapeDtypeStruct((b_sz, n, IN_COLS_PAD), f32),
        scratch_shapes=[pltpu.VMEM((tm, D_MODEL), bf16), pltpu.VMEM((tm, 1), f32)],
        compiler_params=_params(("parallel", "parallel", "arbitrary"), VMEM_LIMIT_WIDE),
        name="inproj",
    )(x, mod, g, w_in)


def _fourier_kernel(u_ref, ccs_ref, cn_ref, sn_ref, o_ref, ucs_scr):
    @pl.when(pl.program_id(1) == 0)
    def _():
        for h in range(FN_HEADS):
            sl = slice(h * FN_DH, (h + 1) * FN_DH)
            r = _dot(u_ref[0, :, sl].astype(bf16), ccs_ref[...]).astype(bf16)
            ucs_scr[:, sl] = r[:, :FN_DH]
            ucs_scr[:, FN_WIDTH + h * FN_DH:FN_WIDTH + (h + 1) * FN_DH] = r[:, FN_DH:]

    y = _dot(cn_ref[...], ucs_scr[:, :FN_WIDTH]) + _dot(sn_ref[...], ucs_scr[:, FN_WIDTH:])
    o_ref[0] = y.astype(bf16)


def _fourier(proj, ccs, cn, sn_neg, tr):
    b_sz, n, _ = proj.shape
    return pl.pallas_call(
        _fourier_kernel,
        grid=(b_sz, n // tr),
        in_specs=[
            pl.BlockSpec((1, n, FN_WIDTH), lambda b, i: (b, 0, 0)),
            pl.BlockSpec((FN_DH, 2 * FN_DH), lambda b, i: (0, 0)),
            pl.BlockSpec((tr, n), lambda b, i: (i, 0)),
            pl.BlockSpec((tr, n), lambda b, i: (i, 0)),
        ],
        out_specs=pl.BlockSpec((1, tr, FN_WIDTH), lambda b, i: (b, i, 0)),
        out_shape=jax.ShapeDtypeStruct((b_sz, n, FN_WIDTH), bf16),
        scratch_shapes=[pltpu.VMEM((n, 2 * FN_WIDTH), bf16)],
        compiler_params=_params(("parallel", "arbitrary")),
        name="fourier",
    )(proj, ccs, cn, sn_neg)


def _dft_tables(n):
    j = np.arange(n, dtype=np.int64)
    ang = ((j[:, None] * j[None, :]) % n).astype(np.float64) * (2.0 * np.pi / n)
    scale_n = 1.0 / np.sqrt(n)
    cn = jnp.asarray(np.cos(ang) * scale_n, dtype=bf16)
    sn_neg = jnp.asarray(-np.sin(ang) * scale_n, dtype=bf16)
    return cn, sn_neg


def _channel_tables():
    k = np.arange(FN_DH)
    ang = 2.0 * np.pi * ((k[:, None] * k[None, :]) % FN_DH) / FN_DH
    table = np.concatenate([np.cos(ang), np.sin(ang)], axis=1) / np.sqrt(FN_DH)
    return jnp.asarray(table, dtype=bf16)


def _hgrn_tables():
    c = HG_CHUNK
    idx = np.arange(c)
    blocks = [(idx[None, :] <= idx[:, None])]
    masks = []
    for h in HG_LEVELS:
        mid = (idx // (2 * h)) * (2 * h) + h
        upper = idx >= mid
        if h < SUBLANES:
            row_up = (idx[None, :] >= mid[:, None]) & (idx[None, :] <= idx[:, None])
            row_lo = (idx[None, :] > idx[:, None]) & (idx[None, :] < mid[:, None])
            blocks.append(np.where(upper[:, None], row_up, row_lo))
        same = (idx[:, None] // (2 * h)) == (idx[None, :] // (2 * h))
        masks.append(same & upper[:, None] & ~upper[None, :])
    masks.append(np.eye(c, dtype=bool))
    a_f = np.concatenate([b.astype(np.float32) for b in blocks], axis=0)
    a_b = np.concatenate([b[::-1, ::-1].astype(np.float32) for b in blocks], axis=0)
    m_f = np.stack([m.astype(np.float32) for m in masks])
    m_b = np.stack([m[::-1, ::-1].astype(np.float32) for m in masks])
    rep = lambda a: jnp.asarray(np.concatenate([a] * HG_SPLIT, axis=1), dtype=bf16)
    return rep(a_f), rep(a_b), jnp.asarray(m_f), jnp.asarray(m_b)


def _block_row(x, h, row):
    c, w = x.shape
    xb = x.reshape(c // (2 * h), 2 * h, w)
    return jnp.broadcast_to(xb[:, row:row + 1, :], xb.shape).reshape(c, w)


def _hgrn_direction(q, v, pre, lb, a_ref, m_ref, states, backward):
    c = HG_CHUNK
    nl = len(HG_LEVELS)
    f = lb + (1.0 - lb) * jax.nn.sigmoid(pre)
    lf = jnp.log(jnp.maximum(f, F_FLOOR))
    kk = 1.0 - f
    pieces = []
    rem = lf
    for _ in range(HG_SPLIT):
        piece = rem.astype(bf16)
        pieces.append(piece)
        rem = rem - piece.astype(f32)
    sums = _dot(a_ref[...], jnp.concatenate(pieces, axis=0))
    cum = sums[0:c]
    total = cum[0:1] if backward else cum[c - 1:c]
    q16 = q.astype(bf16)
    k16 = kk.astype(bf16)
    v16 = v.astype(bf16)
    decay16 = lambda log_decay: jnp.exp(log_decay.astype(bf16))
    q_inter = q16 * decay16(cum)
    k_state = k16 * decay16(jnp.minimum(total - cum, 0.0))
    e_total = jnp.exp(total)
    q_levels, k_levels = [q16], [k16]
    fine = 0
    for h in HG_LEVELS:
        if h >= SUBLANES:
            ref = _block_row(cum, h, h if backward else h - 1)
            el = decay16(-jnp.abs(cum - ref))
        else:
            fine += 1
            el = decay16(sums[fine * c:(fine + 1) * c])
        q_levels.append(q16 * el)
        k_levels.append(k16 * el)
    outs, new_states = [], []
    for h in range(HG_HEADS):
        sl = slice(h * HG_DK, (h + 1) * HG_DK)
        vh = v16[:, sl]
        sc = m_ref[nl] * _dot_nt(q_levels[0][:, sl], k_levels[0][:, sl])
        for li in range(nl):
            sc = sc + m_ref[li] * _dot_nt(q_levels[li + 1][:, sl], k_levels[li + 1][:, sl])
        st = states[h]
        outs.append(_dot(sc.astype(bf16), vh) + _dot_nt(q_inter[:, sl], st.astype(bf16)))
        new_states.append(e_total[:, sl] * st + _dot_tn(vh, k_state[:, sl]))
    return outs, new_states


def _hgrn_kernel(layer, has_s0, has_sout, *refs):
    (qf_ref, vf_ref, ff_ref, qb_ref, vb_ref, fb_ref, lb_ref, af_ref, ab_ref, mf_ref, mb_ref) = refs[:11]
    pos = 11
    s0_ref = None
    if has_s0:
        s0_ref = refs[pos]
        pos += 1
    of_ref, ob_ref = refs[pos], refs[pos + 1]
    pos += 2
    sout_ref = None
    if has_sout:
        sout_ref = refs[pos]
        pos += 1
    st_scr = refs[pos]
    ci = pl.program_id(1)

    n_seq = qf_ref.shape[0]

    @pl.when(ci == 0)
    def _():
        for s in range(n_seq):
            for d in range(2):
                for h in range(HG_HEADS):
                    if has_s0:
                        st_scr[s, d, h] = s0_ref[s, 0, d, h].T
                    else:
                        st_scr[s, d, h] = jnp.zeros((HG_DV, HG_DK), f32)

    raw = lb_ref[...]
    mx = raw[0]
    for i in range(1, DEPTH):
        mx = jnp.maximum(mx, raw[i])
    ex = [jnp.exp(raw[i] - mx) for i in range(DEPTH)]
    den = ex[0]
    for i in range(1, DEPTH):
        den = den + ex[i]
    lb = jnp.zeros_like(den)
    for i in range(1, layer + 1):
        lb = lb + ex[i] / den

    loaded = [[[st_scr[s, d, h] for h in range(HG_HEADS)] for d in range(2)] for s in range(n_seq)]
    results = []
    for s in range(n_seq):
        o_f, st_f = _hgrn_direction(qf_ref[s], vf_ref[s], ff_ref[s], lb[0:1], af_ref, mf_ref, loaded[s][0], False)
        o_b, st_b = _hgrn_direction(qb_ref[s], vb_ref[s], fb_ref[s], lb[1:2], ab_ref, mb_ref, loaded[s][1], True)
        of_ref[s] = jnp.concatenate(o_f, axis=1)
        ob_ref[s] = jnp.concatenate(o_b, axis=1)
        results.append((st_f, st_b))
    for s in range(n_seq):
        for h in range(HG_HEADS):
            st_scr[s, 0, h] = results[s][0][h]
            st_scr[s, 1, h] = results[s][1][h]

    if has_sout:
        @pl.when(ci == pl.num_programs(1) - 1)
        def _():
            for s in range(n_seq):
                for d in range(2):
                    for h in range(HG_HEADS):
                        sout_ref[s, d, h] = st_scr[s, d, h].T


def _hgrn(proj, hg_lb, tables, layer, s0, want_state):
    b_sz, n, _ = proj.shape
    nc = n // HG_CHUNK
    a_f, a_b, m_f, m_b = tables
    c = HG_CHUNK
    sb = HG_SEQS
    fwd = lambda col: pl.BlockSpec((sb, c, HG_QK), lambda b, i: (b, i, col))
    bwd = lambda col: pl.BlockSpec((sb, c, HG_QK), lambda b, i: (b, nc - 1 - i, col))
    const = lambda shape: pl.BlockSpec(shape, lambda b, i: (0,) * len(shape))
    in_specs = [fwd(COL_HQ), fwd(COL_HV), fwd(COL_HFF), bwd(COL_HQ), bwd(COL_HV), bwd(COL_HFB),
                const(hg_lb.shape), const(a_f.shape), const(a_b.shape), const(m_f.shape), const(m_b.shape)]
    args = [proj, proj, proj, proj, proj, proj, hg_lb, a_f, a_b, m_f, m_b]
    state_spec = pl.BlockSpec((sb, 2, HG_HEADS, HG_DK, HG_DV), lambda b, i: (b, 0, 0, 0, 0))
    if s0 is not None:
        in_specs.append(pl.BlockSpec((sb, 1, 2, HG_HEADS, HG_DK, HG_DV), lambda b, i: (b, layer, 0, 0, 0, 0)))
        args.append(s0)
    out_specs = [pl.BlockSpec((sb, c, HG_WIDTH), lambda b, i: (b, i, 0)),
                 pl.BlockSpec((sb, c, HG_WIDTH), lambda b, i: (b, nc - 1 - i, 0))]
    out_shape = [jax.ShapeDtypeStruct((b_sz, n, HG_WIDTH), f32)] * 2
    if want_state:
        out_specs.append(state_spec)
        out_shape.append(jax.ShapeDtypeStruct((b_sz, 2, HG_HEADS, HG_DK, HG_DV), f32))
    return pl.pallas_call(
        functools.partial(_hgrn_kernel, layer, s0 is not None, want_state),
        grid=(b_sz // sb, nc),
        in_specs=in_specs,
        out_specs=out_specs,
        out_shape=out_shape,
        scratch_shapes=[pltpu.VMEM((sb, 2, HG_HEADS, HG_DV, HG_DK), f32)],
        compiler_params=_params(("parallel", "arbitrary")),
        name="hgrn",
    )(*args)


def _rope_rotate(x, cos, sin):
    lane = lax.broadcasted_iota(jnp.int32, x.shape, 1)
    first_half = (lane % (MLA_ROPE // 2)) < (MLA_ROPE // 4)
    quarter = MLA_ROPE // 4
    rot = jnp.where(first_half, -pltpu.roll(x, LANES - quarter, 1), pltpu.roll(x, quarter, 1))
    return x * cos + rot * sin


def _mla_kernel(n_ctx, use_rope, want_cache, tq, *refs):
    qa_ref, kvr_ref, wq_ref, wkt_ref, wv_ref, qn_ref, kvn_ref = refs[:7]
    pos = 7
    if n_ctx:
        cckv_ref, ckr_ref = refs[pos], refs[pos + 1]
        pos += 2
    if use_rope:
        cosq_ref, sinq_ref, cosk_ref, sink_ref = refs[pos:pos + 4]
        pos += 4
    y_ref = refs[pos]
    pos += 1
    if want_cache:
        ockv_ref, okr_ref = refs[pos], refs[pos + 1]
        pos += 2
    kcat_scr, v_scr, o_scr = refs[pos:pos + 3]
    n_seq, n = kvr_ref.shape[0], kvr_ref.shape[1]
    sk = n_ctx + n
    total = n_seq * n
    rows = MLA_EXPAND_ROWS if total % MLA_EXPAND_ROWS == 0 else total

    def pieces(r0, nrows):
        out, r = [], r0
        while r < r0 + nrows:
            s, off = divmod(r, n)
            cnt = min(n - off, r0 + nrows - r)
            out.append((s, off, cnt))
            r += cnt
        return out

    def expand(r0, nrows, c_kv, krp):
        k_t = _dot_nt(wkt_ref[0], c_kv)
        eye = (lax.broadcasted_iota(jnp.int32, (LANES, LANES), 0)
               == lax.broadcasted_iota(jnp.int32, (LANES, LANES), 1)).astype(bf16)
        krp_t = _dot_nt(eye, krp).astype(bf16)
        vals = _dot(c_kv, wv_ref[0])
        ones = jnp.ones((nrows, MLA_V), bf16)
        for h in range(MLA_HEADS):
            kcat_scr[h, 0:MLA_NOPE, r0:r0 + nrows] = k_t[h * MLA_NOPE:(h + 1) * MLA_NOPE].astype(bf16)
            kcat_scr[h, MLA_NOPE:QHEAD_PAD, r0:r0 + nrows] = krp_t
            v_scr[h, r0:r0 + nrows, 0:MLA_V] = vals[:, h * MLA_V:(h + 1) * MLA_V].astype(bf16)
            v_scr[h, r0:r0 + nrows, MLA_V:2 * MLA_V] = ones

    @pl.when(pl.program_id(1) == 0)
    def _():
        if n_ctx:
            zeros = jnp.zeros((n_ctx, LANES - MLA_ROPE), f32)
            krp_ctx = jnp.concatenate([ckr_ref[0, 0], zeros], axis=1).astype(bf16)
            expand(0, n_ctx, cckv_ref[0, 0].astype(bf16), krp_ctx)
        for r0 in range(0, total, rows):
            parts = pieces(r0, rows)
            blk = jnp.concatenate([kvr_ref[s, off:off + cnt] for s, off, cnt in parts], axis=0)
            c_kv = _rms(blk[:, :KV_LORA]) * kvn_ref[...]
            krp = blk[:, KV_LORA:]
            if want_cache:
                at = 0
                for s, off, cnt in parts:
                    ockv_ref[s, off:off + cnt] = c_kv[at:at + cnt]
                    okr_ref[s, off:off + cnt] = krp[at:at + cnt, :MLA_ROPE]
                    at += cnt
            if use_rope:
                krp = _rope_rotate(krp, cosk_ref[r0:r0 + rows], sink_ref[r0:r0 + rows])
            expand(n_ctx + r0, rows, c_kv.astype(bf16), krp.astype(bf16))

    scale = float(MLA_NOPE + MLA_ROPE) ** -0.5
    qa = jnp.concatenate([qa_ref[s] for s in range(n_seq)], axis=0)
    qn = (_rms(qa) * qn_ref[...]).astype(bf16)
    n_kc = 1 if sk <= MLA_KEY_CHUNK else sk // (2 * MXU_DIM)
    kc = sk // n_kc

    def head(h):
        q = _dot(qn, wq_ref[0, h]) * scale
        q_rope = q[:, MLA_NOPE:]
        if use_rope:
            q_rope = _rope_rotate(q_rope, cosq_ref[...], sinq_ref[...])
        q_cat = jnp.concatenate([q[:, :MLA_NOPE], q_rope], axis=1).astype(bf16)
        outs = []
        for s in range(n_seq):
            q_s = q_cat[s * tq:(s + 1) * tq]
            m_run = acc = None
            for c in range(n_kc):
                k0 = s * sk + c * kc
                sc = _dot(q_s, kcat_scr[h, :, k0:k0 + kc])
                m_c = jnp.max(sc, axis=-1, keepdims=True)
                m_new = m_c if c == 0 else jnp.maximum(m_run, m_c)
                p = jnp.exp((sc - m_new).astype(bf16))
                pv = _dot(p, v_scr[h, k0:k0 + kc])
                acc = pv if c == 0 else jnp.exp(m_run - m_new) * acc + pv
                m_run = m_new
            outs.append((acc[:, :MLA_V] / acc[:, MLA_V:]).astype(bf16))
        return outs

    if n_kc == 1:
        for h in range(MLA_HEADS):
            for s, o in enumerate(head(h)):
                y_ref[s, :, h * MLA_V:(h + 1) * MLA_V] = o
    else:
        def body(h, carry):
            o_scr[h] = head(h)[0]
            return carry

        lax.fori_loop(0, MLA_HEADS, body, 0, unroll=4)
        for h in range(MLA_HEADS):
            y_ref[0, :, h * MLA_V:(h + 1) * MLA_V] = o_scr[h]


def _mla(proj, wq, wkt, wv, q_norm, kv_norm, ctx, layer, rope, want_cache, tq):
    b_sz, n, _ = proj.shape
    n_ctx = 0 if ctx is None else ctx[0].shape[2]
    sk = n_ctx + n
    sb = MLA_SEQS if (ctx is None and rope is None and n == tq) else 1
    single = pl.Buffered(1)
    const = lambda shape: pl.BlockSpec(shape, lambda b, i: (0,) * len(shape), pipeline_mode=single)
    in_specs = [
        pl.BlockSpec((sb, tq, Q_LORA), lambda b, i: (b, i, COL_QA)),
        pl.BlockSpec((sb, n, KVR_WIDTH), lambda b, i: (b, 0, COL_KVR), pipeline_mode=single),
        pl.BlockSpec((1,) + wq.shape[1:], lambda b, i: (layer, 0, 0, 0), pipeline_mode=single),
        pl.BlockSpec((1,) + wkt.shape[1:], lambda b, i: (layer, 0, 0), pipeline_mode=single),
        pl.BlockSpec((1,) + wv.shape[1:], lambda b, i: (layer, 0, 0), pipeline_mode=single),
        const(q_norm.shape), const(kv_norm.shape),
    ]
    args = [proj, proj, wq, wkt, wv, q_norm, kv_norm]
    if ctx is not None:
        in_specs += [
            pl.BlockSpec((1, 1, n_ctx, KV_LORA), lambda b, i: (b, layer, 0, 0), pipeline_mode=single),
            pl.BlockSpec((1, 1, n_ctx, MLA_ROPE), lambda b, i: (b, layer, 0, 0), pipeline_mode=single)]
        args += [ctx[0], ctx[1]]
    if rope is not None:
        cos, sin = rope
        in_specs += [pl.BlockSpec((tq, LANES), lambda b, i: (i, 0)),
                     pl.BlockSpec((tq, LANES), lambda b, i: (i, 0)),
                     const(cos.shape), const(sin.shape)]
        args += [cos, sin, cos, sin]
    out_specs = [pl.BlockSpec((sb, tq, MLA_WIDTH), lambda b, i: (b, i, 0))]
    out_shape = [jax.ShapeDtypeStruct((b_sz, n, MLA_WIDTH), bf16)]
    if want_cache:
        out_specs += [pl.BlockSpec((sb, n, KV_LORA), lambda b, i: (b, 0, 0)),
                      pl.BlockSpec((sb, n, MLA_ROPE), lambda b, i: (b, 0, 0))]
        out_shape += [jax.ShapeDtypeStruct((b_sz, n, KV_LORA), f32),
                      jax.ShapeDtypeStruct((b_sz, n, MLA_ROPE), f32)]
    return pl.pallas_call(
        functools.partial(_mla_kernel, n_ctx, rope is not None, want_cache, tq),
        grid=(b_sz // sb, n // tq),
        in_specs=in_specs,
        out_specs=out_specs,
        out_shape=out_shape,
        scratch_shapes=[pltpu.VMEM((MLA_HEADS, QHEAD_PAD, sb * sk), bf16),
                        pltpu.VMEM((MLA_HEADS, sb * sk, 2 * MLA_V), bf16),
                        pltpu.VMEM((MLA_HEADS, tq, MLA_V), bf16)],
        compiler_params=_params(("parallel", "arbitrary")),
        name="mla",
    )(*args)


def _rope_tables(n):
    t = np.arange(n)
    r = (t // GRID_W).astype(np.float32)
    col = (t % GRID_W).astype(np.float32)
    nf = MLA_ROPE // 4
    inv = np.float32(ROPE_BASE) ** (-np.arange(nf, dtype=np.float32) / np.float32(nf))
    ar = r[:, None] * inv
    ac = col[:, None] * inv
    ang = np.concatenate([ar, ar, ac, ac], axis=-1).astype(np.float32)
    pad = LANES - MLA_ROPE
    cos = np.concatenate([np.cos(ang), np.ones((n, pad), np.float32)], axis=-1)
    sin = np.concatenate([np.sin(ang), np.zeros((n, pad), np.float32)], axis=-1)
    return jnp.asarray(cos, dtype=f32), jnp.asarray(sin, dtype=f32)


def _outproj_kernel(x_ref, yfn_ref, of_ref, ob_ref, gate_ref, ymla_ref, w_ref, hgg_ref, gpost_ref,
                    mod_ref, o_ref, inv_scr):
    o_hg = of_ref[0] + ob_ref[0]
    gate = gate_ref[0]
    gate = gate * jax.nn.sigmoid(gate)
    parts = [yfn_ref[0]]
    for h in range(HG_HEADS):
        sl = slice(h * HG_DV, (h + 1) * HG_DV)
        parts.append((_rms(o_hg[:, sl]) * hgg_ref[:, sl] * gate[:, sl]).astype(bf16))
    parts.append(ymla_ref[0])
    o_ref[0] = _dot(jnp.concatenate(parts, axis=1), w_ref[0])
    y = o_ref[0]
    inv_scr[...] = lax.rsqrt(jnp.mean(y * y, axis=-1, keepdims=True) + EPS)
    gain = mod_ref[0, 2:3, :] * gpost_ref[...]
    o_ref[0] = x_ref[0] + o_ref[0] * inv_scr[...] * gain


def _outproj(x, y_fn, o_f, o_b, proj, y_mla, w_out, hg_gain, g_post, mod, layer, tm):
    b_sz, n, _ = x.shape
    tok = lambda width, col=0: pl.BlockSpec((1, tm, width), lambda b, i: (b, i, col))
    return pl.pallas_call(
        _outproj_kernel,
        grid=(b_sz, n // tm),
        in_specs=[
            tok(D_MODEL), tok(FN_WIDTH), tok(HG_WIDTH), tok(HG_WIDTH), tok(HG_WIDTH, COL_HGATE),
            tok(MLA_WIDTH),
            pl.BlockSpec((1, D_MODEL, D_MODEL), lambda b, i: (layer, 0, 0)),
            pl.BlockSpec((1, HG_WIDTH), lambda b, i: (0, 0)),
            pl.BlockSpec((1, D_MODEL), lambda b, i: (0, 0)),
            pl.BlockSpec((1, 6, D_MODEL), lambda b, i: (b, 0, 0)),
        ],
        out_specs=tok(D_MODEL),
        out_shape=jax.ShapeDtypeStruct((b_sz, n, D_MODEL), f32),
        scratch_shapes=[pltpu.VMEM((tm, 1), f32)],
        compiler_params=_params(("parallel", "parallel")),
        name="outproj",
    )(x, y_fn, o_f, o_b, proj, y_mla, w_out, hg_gain, g_post, mod)


def _ffn_kernel(x_ref, mod_ref, gpre_ref, gpost_ref, w1_ref, w2_ref, o_ref, h_scr, inv_scr):
    k = pl.program_id(2)

    def hidden_slice():
        a = jnp.maximum(_dot(h_scr[...], w1_ref[0]), 0.0)
        return _dot((a * a).astype(bf16), w2_ref[0])

    @pl.when(k == 0)
    def _():
        _modulated_norm(x_ref, gpre_ref, mod_ref[0, 4:5, :], mod_ref[0, 3:4, :], h_scr, inv_scr)
        o_ref[0] = hidden_slice()

    last = pl.num_programs(2) - 1

    @pl.when((k > 0) & (k < last))
    def _():
        o_ref[0] += hidden_slice()

    @pl.when(k == last)
    def _():
        o_ref[0] += hidden_slice()
        f = o_ref[0]
        inv_scr[...] = lax.rsqrt(jnp.mean(f * f, axis=-1, keepdims=True) + EPS)
        gain = mod_ref[0, 5:6, :] * gpost_ref[...]
        o_ref[0] = x_ref[0] + o_ref[0] * inv_scr[...] * gain


def _ffn(x, mod, g_pre, g_post, w1, w2, layer, tm, tf):
    b_sz, n, _ = x.shape
    return pl.pallas_call(
        _ffn_kernel,
        grid=(b_sz, n // tm, D_FF // tf),
        in_specs=[
            pl.BlockSpec((1, tm, D_MODEL), lambda b, i, k: (b, i, 0)),
            pl.BlockSpec((1, 6, D_MODEL), lambda b, i, k: (b, 0, 0)),
            pl.BlockSpec((1, D_MODEL), lambda b, i, k: (0, 0)),
            pl.BlockSpec((1, D_MODEL), lambda b, i, k: (0, 0)),
            pl.BlockSpec((1, D_MODEL, tf), lambda b, i, k: (layer, 0, k)),
            pl.BlockSpec((1, tf, D_MODEL), lambda b, i, k: (layer, k, 0)),
        ],
        out_specs=pl.BlockSpec((1, tm, D_MODEL), lambda b, i, k: (b, i, 0)),
        out_shape=jax.ShapeDtypeStruct((b_sz, n, D_MODEL), f32),
        scratch_shapes=[pltpu.VMEM((tm, D_MODEL), bf16), pltpu.VMEM((tm, 1), f32)],
        compiler_params=_params(("parallel", "parallel", "arbitrary"), VMEM_LIMIT_WIDE),
        name="ffn",
    )(x, mod, g_pre, g_post, w1, w2)


def _layer(x, mod, wl, hg_tables, dft, layer, n_seq, ctx, s0, rope, is_context, tq):
    b_sz, n, _ = x.shape
    n_seqs = b_sz * n // n_seq
    proj = _inproj(x, mod, wl["g_pre_mix"], wl["w_in"], layer, INPROJ_TM).reshape(n_seqs, n_seq, IN_COLS_PAD)
    y_fn = _fourier(proj, dft[0], dft[1], dft[2], min(n_seq, FOURIER_TR))
    hg = _hgrn(proj, wl["hg_lb"], hg_tables, layer, s0, is_context)
    mla = _mla(proj, wl["wq"], wl["wkt"], wl["wv"], wl["q_norm"], wl["kv_norm"], ctx, layer, rope, is_context,
               tq)
    flat = lambda a: a.reshape(b_sz, n, a.shape[-1])
    x = _outproj(x, flat(y_fn), flat(hg[0]), flat(hg[1]), flat(proj), flat(mla[0]), wl["w_out"],
                 wl["hg_gain"], wl["g_post_mix"], mod, layer, OUTPROJ_TM)
    x = _ffn(x, mod, wl["g_pre_ff"], wl["g_post_ff"], wl["w_ff1"], wl["w_ff2"], layer, FFN_TM, FFN_TF)
    extras = (mla[1], mla[2], hg[2]) if is_context else None
    return x, extras


def kernel(x_prompt, x_sample, c, cache_ckv, cache_krope, state_hgrn, c_ctx, w_ada, b_ada, g_pre_mix,
           g_post_mix, g_pre_ff, g_post_ff, w_in, hg_lb, hg_gain, mla_q_norm, mla_kv_norm, w_q_b, w_kv_b,
           w_out, w_ff1, w_ff2):
    batch, seq, _ = x_prompt.shape
    dec_batch, dec_seq, _ = x_sample.shape

    rows = ADA_ROWS
    cond = jnp.concatenate([c_ctx[None, :], c, jnp.zeros((rows - 1 - dec_batch, D_MODEL), f32)], axis=0)
    mod = _ada(cond, w_ada, b_ada).reshape(DEPTH, rows, 6, D_MODEL)

    hg_tables = _hgrn_tables()
    ccs = _channel_tables()
    dft_p = (ccs,) + _dft_tables(seq)
    dft_s = (ccs,) + _dft_tables(dec_seq)
    rope = _rope_tables(dec_seq)

    yp = x_prompt.reshape(1, batch * seq, D_MODEL)
    ys = x_sample
    ckv_list, kr_list, st_list = [], [], []
    wq = w_q_b.reshape(DEPTH, Q_LORA, MLA_HEADS, MLA_NOPE + MLA_ROPE).astype(bf16)
    wq = jnp.pad(wq, ((0, 0), (0, 0), (0, 0), (0, QHEAD_PAD - MLA_NOPE - MLA_ROPE)))
    wkv = w_kv_b.astype(bf16).reshape(DEPTH, KV_LORA, MLA_HEADS, MLA_NOPE + MLA_V)
    stacked = {
        "w_in": jnp.concatenate(
            [w_in.astype(bf16), jnp.zeros((DEPTH, D_MODEL, IN_COLS_PAD - IN_COLS), bf16)], axis=-1),
        "wq": wq.transpose(0, 2, 1, 3),
        "wkt": wkv[:, :, :, :MLA_NOPE].reshape(DEPTH, KV_LORA, MLA_HEADS * MLA_NOPE).transpose(0, 2, 1),
        "wv": wkv[:, :, :, MLA_NOPE:].reshape(DEPTH, KV_LORA, MLA_HEADS * MLA_V),
        "w_out": w_out.astype(bf16),
        "w_ff1": w_ff1.astype(bf16),
        "w_ff2": w_ff2.astype(bf16),
    }
    for l in range(DEPTH):
        wl = {
            **stacked,
            "g_pre_mix": g_pre_mix[l][None, :], "g_post_mix": g_post_mix[l][None, :],
            "g_pre_ff": g_pre_ff[l][None, :], "g_post_ff": g_post_ff[l][None, :],
            "hg_lb": hg_lb, "hg_gain": hg_gain[l][None, :],
            "q_norm": mla_q_norm[l][None, :], "kv_norm": mla_kv_norm[l][None, :],
        }
        yp, (ckv_l, kr_l, st_l) = _layer(yp, mod[l, 0:1], wl, hg_tables, dft_p, l, seq, None, None, None,
                                         True, seq)
        ckv_list.append(ckv_l)
        kr_list.append(kr_l)
        st_list.append(st_l)
        ys, _ = _layer(ys, mod[l, 1:1 + dec_batch], wl, hg_tables, dft_s, l, dec_seq,
                       (cache_ckv, cache_krope), state_hgrn, rope, False, 1024)

    return (yp.reshape(batch, seq, D_MODEL), ys, jnp.stack(ckv_list, axis=1), jnp.stack(kr_list, axis=1),
            jnp.stack(st_list, axis=1))
```
